```python
import math
import jax, jax.numpy as jnp
from jax import lax
import numpy as np

D_MODEL = 1024
BATCH = 2
SEQ = 16384
DEPTH = 2

GRID_W = 64
CTX_LEN = 256
HEAD_DIM = 64
N_MIXERS = 4
N_HEADS = D_MODEL // HEAD_DIM
GROUP_HEADS = N_HEADS // N_MIXERS
GROUP_W = GROUP_HEADS * HEAD_DIM
D_MIX = N_MIXERS * GROUP_W
CHUNK = 64
DN_CONV = 3
NA_KH = 8
NA_KW = 16
DF_HEAD_DIM = HEAD_DIM // 2
Q_BLOCK = 128
ROPE_BASE = 10000.0
D_FF = 2816
FFN_CONV = 3
EPS = 1e-6
DN_W = 4 * GROUP_W + 4 * GROUP_HEADS
NA_W = 3 * GROUP_W
DF_W = 3 * GROUP_W
HG_W = 5 * GROUP_W
P_IN = DN_W + NA_W + DF_W + HG_W

kernel_name = "hybrid_parallel_group_diffusion_block"


def _split(x, sizes, axis=-1):
    return jnp.split(x, np.cumsum(sizes)[:-1].tolist(), axis=axis)


def _rmsnorm(x, w):
    xf = x.astype(jnp.float32)
    y = xf * lax.rsqrt(jnp.mean(jnp.square(xf), axis=-1, keepdims=True) + EPS)
    return (y * w.astype(jnp.float32)).astype(x.dtype)


def _l2norm(x):
    xf = x.astype(jnp.float32)
    return (xf * lax.rsqrt(jnp.sum(jnp.square(xf), axis=-1, keepdims=True) + EPS)).astype(x.dtype)


def _dwconv(x, w):
    k, ch = w.shape
    return lax.conv_general_dilated(x, w[:, None, :].astype(x.dtype), window_strides=(1,),
                                    padding=[(k // 2, k // 2)], dimension_numbers=("NWC", "WIO", "NWC"),
                                    feature_group_count=ch)


def _heads(x, n):
    b, l, _ = x.shape
    return x.reshape(b, l, n, -1).transpose(0, 2, 1, 3)


def _merge(x):
    b, h, l, d = x.shape
    return x.transpose(0, 2, 1, 3).reshape(b, l, h * d)


def _rope_2d_tables(length, dh):
    t = jnp.arange(length)
    row = (t // GRID_W).astype(jnp.float32)
    col = (t % GRID_W).astype(jnp.float32)
    half = dh // 2
    inv = 1.0 / (ROPE_BASE ** (jnp.arange(0, half, 2, dtype=jnp.float32) / half))
    ang_r = row[:, None] * inv
    ang_c = col[:, None] * inv
    return jnp.cos(ang_r), jnp.sin(ang_r), jnp.cos(ang_c), jnp.sin(ang_c)


def _rotate(x, cos, sin):
    x1, x2 = jnp.split(x, 2, axis=-1)
    cos = cos.astype(x.dtype)
    sin = sin.astype(x.dtype)
    return jnp.concatenate([x1 * cos - x2 * sin, x2 * cos + x1 * sin], axis=-1)


def _rope_2d(x, cos_r, sin_r, cos_c, sin_c):
    xr, xc = jnp.split(x, 2, axis=-1)
    return jnp.concatenate([_rotate(xr, cos_r, sin_r), _rotate(xc, cos_c, sin_c)], axis=-1)


def _gated_delta_rule(q, k, v, beta, log_a, s0):
    f32 = jnp.float32
    b, h, length, dk = q.shape
    dv = v.shape[-1]
    n = length // CHUNK
    ch = lambda t: t.astype(f32).reshape(b, h, n, CHUNK, *t.shape[3:])
    q = ch(q) * (dk ** -0.5)
    k = ch(k)
    v = ch(v)
    beta = ch(beta)
    g = jnp.cumsum(ch(log_a), axis=-1)
    incl = jnp.tril(jnp.ones((CHUNK, CHUNK), bool))
    strict = jnp.tril(jnp.ones((CHUNK, CHUNK), bool), -1)
    gdiff = g[..., :, None] - g[..., None, :]
    decay = jnp.where(incl, jnp.exp(jnp.where(incl, gdiff, 0.0)), 0.0)
    kb = k * beta[..., None]
    a_mat = jnp.where(strict, jnp.einsum("bhncd,bhnsd->bhncs", kb, k) * decay, 0.0)
    eye = jnp.eye(CHUNK, dtype=f32)
    t_mat = lax.linalg.triangular_solve(a_mat + eye, jnp.broadcast_to(eye, a_mat.shape),
                                        left_side=True, lower=True, unit_diagonal=True)
    eg = jnp.exp(g)
    w = t_mat @ (kb * eg[..., None])
    u = t_mat @ (v * beta[..., None])
    attn = jnp.einsum("bhncd,bhnsd->bhncs", q, k) * decay
    qg = q * eg[..., None]
    kdec = k * jnp.exp(g[..., -1:] - g)[..., None]
    glast = jnp.exp(g[..., -1])

    def step(s, xs):
        qg_c, w_c, u_c, attn_c, kdec_c, gl_c = xs
        v_new = u_c - w_c @ s
        o = qg_c @ s + attn_c @ v_new
        s = s * gl_c[..., None, None] + jnp.swapaxes(kdec_c, -1, -2) @ v_new
        return s, o

    xs = tuple(jnp.moveaxis(t, 2, 0) for t in (qg, w, u, attn, kdec, glast))
    s_fin, o = lax.scan(step, s0.astype(f32), xs)
    return jnp.moveaxis(o, 0, 2).reshape(b, h, length, dv), s_fin


def _hgrn2_chunked(q, k, v, log_f, s0):
    f32 = jnp.float32
    b, h, length, dk = q.shape
    dv = v.shape[-1]
    n = length // CHUNK
    ch = lambda t: t.astype(f32).reshape(b, h, n, CHUNK, t.shape[-1])
    q = ch(q) * (dk ** -0.5)
    k = ch(k)
    v = ch(v)
    cum = jnp.cumsum(ch(log_f), axis=-2)
    qb = q * jnp.exp(cum)
    kend = k * jnp.exp(cum[..., -1:, :] - cum)
    dend = jnp.exp(cum[..., -1, :])
    incl = jnp.tril(jnp.ones((CHUNK, CHUNK), bool))[:, :, None]

    def step(s, xs):
        q_c, k_c, v_c, b_c, qb_c, ke_c, de_c = xs
        bd = b_c[..., :, None, :] - b_c[..., None, :, :]
        dec = jnp.where(incl, jnp.exp(jnp.where(incl, bd, 0.0)), 0.0)
        att = jnp.einsum("bhtd,bhsd,bhtsd->bhts", q_c, k_c, dec)
        o = qb_c @ s + att @ v_c
        s = s * de_c[..., None] + jnp.swapaxes(ke_c, -1, -2) @ v_c
        return s, o

    xs = tuple(jnp.moveaxis(t, 2, 0) for t in (q, k, v, cum, qb, kend, dend))
    s_fin, o = lax.scan(step, s0.astype(f32), xs)
    return jnp.moveaxis(o, 0, 2).reshape(b, h, length, dv), s_fin


def _flip(t, d):
    return jnp.flip(t, axis=2) if d == 1 else t


def _bidir_scan(scan_fn, ctx_dirs, lat_dirs, s0):
    o_ctx = 0.0
    o_lat = 0.0
    for d in range(2):
        oc, sc = scan_fn(*[_flip(t, d) for t in ctx_dirs[d]], s0)
        ox, _ = scan_fn(*[_flip(t, d) for t in lat_dirs[d]], sc)
        o_ctx = o_ctx + _flip(oc, d)
        o_lat = o_lat + _flip(ox, d)
    return o_lat, o_ctx


def _deltanet_mixer(p_lat, p_ctx, need_ctx, conv_w, a_log, dt_bias, norm_w):
    nh = GROUP_HEADS
    f32 = jnp.float32

    def prep(p):
        b, length, _ = p.shape
        qkv, gate, ba = _split(p, [3 * GROUP_W, GROUP_W, 4 * nh])
        q, k, v = _split(jax.nn.silu(_dwconv(qkv, conv_w)), [GROUP_W] * 3)
        q = _l2norm(_heads(q, nh))
        k = _l2norm(_heads(k, nh))
        v = _heads(v, nh)
        ba = ba.astype(f32).reshape(b, length, 4, nh).transpose(2, 0, 3, 1)
        dirs = []
        for d in range(2):
            beta = jax.nn.sigmoid(ba[d])
            log_a = -jnp.exp(a_log[d].astype(f32))[None, :, None] * jax.nn.softplus(
                ba[2 + d] + dt_bias[d].astype(f32)[None, :, None])
            dirs.append((q, k, v, beta, log_a))
        return dirs, gate

    lat_dirs, g_lat = prep(p_lat)
    ctx_dirs, g_ctx = prep(p_ctx)
    s0 = jnp.zeros((p_lat.shape[0], nh, HEAD_DIM, HEAD_DIM), jnp.float32)
    o_lat, o_ctx = _bidir_scan(_gated_delta_rule, ctx_dirs, lat_dirs, s0)
    post = lambda o, g: (_merge(_rmsnorm(o, norm_w)) * jax.nn.silu(g.astype(f32))).astype(p_lat.dtype)
    return post(o_lat, g_lat), (post(o_ctx, g_ctx) if need_ctx else None)


def _na_mixer(p_lat, p_ctx, need_ctx, rpb):
    nh = GROUP_HEADS
    d = HEAD_DIM
    scale = d ** -0.5
    q, k, v = [_heads(t, nh) for t in _split(p_lat, [GROUP_W] * 3)]
    qc, kc, vc = [_heads(t, nh) for t in _split(p_ctx, [GROUP_W] * 3)]
    b, h, length, _ = q.shape
    rows = length // GRID_W
    kh = min(NA_KH, rows)
    kw = min(NA_KW, GRID_W)
    grid = lambda t: t.reshape(b, h, rows, GRID_W, d)
    q = grid(q) * scale
    k = grid(k)
    v = grid(v)
    cols = np.arange(GRID_W)
    c_start = np.clip(cols - kw // 2, 0, GRID_W - kw)
    col_idx = c_start[:, None] + np.arange(kw)[None, :]
    col_bias_idx = col_idx - cols[:, None] + (NA_KW - 1)
    rpb_cols = rpb[:, :, col_bias_idx]

    def row_block(r):
        r_start = jnp.clip(r - kh // 2, 0, rows - kh)
        k_band = lax.dynamic_slice_in_dim(k, r_start, kh, axis=2)[:, :, :, col_idx, :]
        v_band = lax.dynamic_slice_in_dim(v, r_start, kh, axis=2)[:, :, :, col_idx, :]
        q_r = lax.dynamic_index_in_dim(q, r, axis=2, keepdims=False)
        row_bias_idx = r_start + jnp.arange(kh) - r + (NA_KH - 1)
        bias = jnp.take(rpb_cols, row_bias_idx, axis=1).transpose(0, 2, 1, 3)
        s_loc = jnp.einsum("bhwd,bhiwjd->bhwij", q_r, k_band) + bias[None]
        s_ctx = jnp.einsum("bhwd,bhmd->bhwm", q_r, kc)
        logits = jnp.concatenate([s_loc.reshape(b, h, GRID_W, kh * kw), s_ctx], axis=-1).astype(jnp.float32)
        prob = jax.nn.softmax(logits, axis=-1).astype(v.dtype)
        p_loc = prob[..., :kh * kw].reshape(b, h, GRID_W, kh, kw)
        p_ctx_r = prob[..., kh * kw:]
        return (jnp.einsum("bhwij,bhiwjd->bhwd", p_loc, v_band)
                + jnp.einsum("bhwm,bhmd->bhwd", p_ctx_r, vc))

    o = lax.map(row_block, jnp.arange(rows))
    o_lat = o.transpose(1, 0, 3, 2, 4).reshape(b, length, h * d)
    o_ctx = None
    if need_ctx:
        pc = jax.nn.softmax(jnp.einsum("bhmd,bhnd->bhmn", qc * scale, kc).astype(jnp.float32), axis=-1)
        o_ctx = _merge(jnp.einsum("bhmn,bhnd->bhmd", pc.astype(vc.dtype), vc))
    return o_lat, o_ctx


def _diff_mixer(p_lat, p_ctx, need_ctx, lam_p, norm_w, lam_init, rope):
    nh = GROUP_HEADS
    dh = DF_HEAD_DIM
    scale = dh ** -0.5

    def prep(p):
        b, length, _ = p.shape
        q, k, v = _split(p, [GROUP_W] * 3)
        q = q.reshape(b, length, nh, 2, dh).transpose(0, 2, 3, 1, 4)
        k = k.reshape(b, length, nh, 2, dh).transpose(0, 2, 3, 1, 4)
        return q, k, _heads(v, nh)

    q, k, v = prep(p_lat)
    qc, kc, vc = prep(p_ctx)
    q = _rope_2d(q, *rope) * scale
    k = _rope_2d(k, *rope)
    lp = lam_p.astype(jnp.float32)
    lam = jnp.exp(jnp.sum(lp[0] * lp[1])) - jnp.exp(jnp.sum(lp[2] * lp[3])) + lam_init

    def diff_attend(qb, kk, vv):
        s = jnp.einsum("bhjqd,bhjkd->bhjqk", qb, kk).astype(jnp.float32)
        pr = jax.nn.softmax(s, axis=-1)
        a = pr[:, :, 0] - lam * pr[:, :, 1]
        return jnp.einsum("bhqk,bhkd->bhqd", a.astype(vv.dtype), vv)

    k_all = jnp.concatenate([kc, k], axis=3)
    v_all = jnp.concatenate([vc, v], axis=2)
    b, h, _, length, _ = q.shape
    nb = length // Q_BLOCK
    q_blocks = jnp.moveaxis(q.reshape(b, h, 2, nb, Q_BLOCK, dh), 3, 0)
    o = lax.map(lambda qb: diff_attend(qb, k_all, v_all), q_blocks)
    o = jnp.moveaxis(o, 0, 2).reshape(b, h, length, 2 * dh)
    post = lambda t: _merge(_rmsnorm(t, norm_w) * (1.0 - lam_init))
    o_ctx = post(diff_attend(qc * scale, kc, vc)) if need_ctx else None
    return post(o), o_ctx


def _hgrn2_mixer(p_lat, p_ctx, need_ctx, lb, norm_w):
    nh = GROUP_HEADS
    f32 = jnp.float32

    def prep(p):
        q, i, f_fwd, f_bwd, gate = _split(p, [GROUP_W] * 5)
        q = _heads(q, nh)
        v = _heads(i, nh)
        dirs = []
        for d, f_raw in enumerate((f_fwd, f_bwd)):
            f = lb[d] + (1.0 - lb[d]) * jax.nn.sigmoid(f_raw.astype(f32))
            dirs.append((q, _heads(1.0 - f, nh), v, _heads(jnp.log(f), nh)))
        return dirs, gate

    lat_dirs, g_lat = prep(p_lat)
    ctx_dirs, g_ctx = prep(p_ctx)
    s0 = jnp.zeros((p_lat.shape[0], nh, HEAD_DIM, HEAD_DIM), jnp.float32)
    o_lat, o_ctx = _bidir_scan(_hgrn2_chunked, ctx_dirs, lat_dirs, s0)
    post = lambda o, g: (_merge(_rmsnorm(o, norm_w)) * jax.nn.silu(g.astype(f32))).astype(p_lat.dtype)
    return post(o_lat, g_lat), (post(o_ctx, g_ctx) if need_ctx else None)


def _token_mixers(p_lat, p_ctx, need_ctx, layer, rope, dn_conv_w, dn_a_log, dn_dt_bias, dn_norm_w,
                  na_rpb, df_lambda, df_norm_w, hg_lb, hg_norm_w):
    sizes = [DN_W, NA_W, DF_W, HG_W]
    dn_x, na_x, df_x, hg_x = _split(p_lat, sizes)
    dn_c, na_c, df_c, hg_c = _split(p_ctx, sizes)
    lam_init = 0.8 - 0.6 * math.exp(-0.3 * layer)
    outs = [
        _deltanet_mixer(dn_x, dn_c, need_ctx, dn_conv_w, dn_a_log, dn_dt_bias, dn_norm_w),
        _na_mixer(na_x, na_c, need_ctx, na_rpb),
        _diff_mixer(df_x, df_c, need_ctx, df_lambda, df_norm_w, lam_init, rope),
        _hgrn2_mixer(hg_x, hg_c, need_ctx, hg_lb, hg_norm_w),
    ]
    y_lat = jnp.concatenate([o[0].astype(p_lat.dtype) for o in outs], axis=-1)
    y_ctx = jnp.concatenate([o[1].astype(p_ctx.dtype) for o in outs], axis=-1) if need_ctx else None
    return y_lat, y_ctx


def _conv_ffn(h, w_up, conv_w, w_down):
    gate, val = jnp.split(h @ w_up, 2, axis=-1)
    return (jax.nn.silu(_dwconv(gate, conv_w)) * val) @ w_down


def _modulate(t, w, shift, scale):
    return _rmsnorm(t, w) * (1.0 + scale) + shift


def setup_inputs(seed: int = 0) -> dict:
    key = jax.random.key(seed)
    ks = jax.random.split(key, 24)
    f32 = jnp.float32
    nrm = lambda k, shape, s: jax.random.normal(k, shape, f32) * s
    gain = lambda k, shape: 1.0 + 0.02 * jax.random.normal(k, shape, f32)
    dt = jnp.exp(jax.random.uniform(ks[10], (DEPTH, 2, GROUP_HEADS), f32, math.log(1e-3), math.log(1e-1)))
    return {
        "x": nrm(ks[0], (BATCH, SEQ, D_MODEL), 1.0),
        "c": nrm(ks[1], (BATCH, D_MODEL), 1.0),
        "ctx": nrm(ks[2], (BATCH, CTX_LEN, D_MODEL), 1.0),
        "c_ctx": nrm(ks[3], (D_MODEL,), 1.0),
        "w_ada": nrm(ks[4], (DEPTH, D_MODEL, 6 * D_MODEL), 0.5 * D_MODEL ** -0.5),
        "b_ada": nrm(ks[5], (DEPTH, 6 * D_MODEL), 0.02),
        "norm1_w": gain(ks[6], (DEPTH, D_MODEL)),
        "norm2_w": gain(ks[7], (DEPTH, D_MODEL)),
        "w_in": nrm(ks[8], (DEPTH, D_MODEL, P_IN), D_MODEL ** -0.5),
        "dn_conv_w": nrm(ks[9], (DEPTH, DN_CONV, 3 * GROUP_W), DN_CONV ** -0.5),
        "dn_a_log": jnp.log(jax.random.uniform(ks[11], (DEPTH, 2, GROUP_HEADS), f32, 1.0, 16.0)),
        "dn_dt_bias": dt + jnp.log(-jnp.expm1(-dt)),
        "dn_norm_w": gain(ks[12], (DEPTH, HEAD_DIM)),
        "na_rpb": nrm(ks[13], (DEPTH, GROUP_HEADS, 2 * NA_KH - 1, 2 * NA_KW - 1), 0.02),
        "df_lambda": nrm(ks[14], (DEPTH, 4, DF_HEAD_DIM), 0.1),
        "df_norm_w": gain(ks[15], (DEPTH, 2 * DF_HEAD_DIM)),
        "hg_lb_raw": nrm(ks[16], (DEPTH, 2, GROUP_W), 0.1),
        "hg_norm_w": gain(ks[17], (DEPTH, HEAD_DIM)),
        "w_out": nrm(ks[18], (DEPTH, D_MIX, D_MODEL), D_MIX ** -0.5),
        "w_up": nrm(ks[19], (DEPTH, D_MODEL, 2 * D_FF), D_MODEL ** -0.5),
        "ffn_conv_w": nrm(ks[20], (DEPTH, FFN_CONV, D_FF), FFN_CONV ** -0.5),
        "w_down": nrm(ks[21], (DEPTH, D_FF, D_MODEL), D_FF ** -0.5),
        "final_norm_w": gain(ks[22], (D_MODEL,)),
    }


def reference(x, c, ctx, c_ctx, w_ada, b_ada, norm1_w, norm2_w, w_in, dn_conv_w, dn_a_log, dn_dt_bias,
              dn_norm_w, na_rpb, df_lambda, df_norm_w, hg_lb_raw, hg_norm_w, w_out, w_up, ffn_conv_w,
              w_down, final_norm_w):
    length = x.shape[1]
    rope = _rope_2d_tables(length, DF_HEAD_DIM)
    lb_all = jnp.cumsum(jax.nn.softmax(hg_lb_raw.astype(jnp.float32), axis=0), axis=0)
    lb_all = lb_all - lb_all[0]
    silu_c = jax.nn.silu(c)
    silu_cc = jax.nn.silu(c_ctx)
    h_ctx = ctx
    for l in range(DEPTH):
        need_ctx = l < DEPTH - 1
        mx = jnp.split((silu_c @ w_ada[l] + b_ada[l])[:, None, :], 6, axis=-1)
        mc = jnp.split(silu_cc @ w_ada[l] + b_ada[l], 6, axis=-1)
        p_lat = _modulate(x, norm1_w[l], mx[0], mx[1]) @ w_in[l]
        p_ctx = _modulate(h_ctx, norm1_w[l], mc[0], mc[1]) @ w_in[l]
        y_lat, y_ctx = _token_mixers(p_lat, p_ctx, need_ctx, l, rope, dn_conv_w[l], dn_a_log[l],
                                     dn_dt_bias[l], dn_norm_w[l], na_rpb[l], df_lambda[l], df_norm_w[l],
                                     lb_all[l], hg_norm_w[l])
        x = x + mx[2] * (y_lat @ w_out[l])
        x = x + mx[5] * _conv_ffn(_modulate(x, norm2_w[l], mx[3], mx[4]), w_up[l], ffn_conv_w[l], w_down[l])
        if need_ctx:
            h_ctx = h_ctx + mc[2] * (y_ctx @ w_out[l])
            h_ctx = h_ctx + mc[5] * _conv_ffn(_modulate(h_ctx, norm2_w[l], mc[3], mc[4]),
                                              w_up[l], ffn_conv_w[l], w_down[l])
    return _rmsnorm(x, final_norm_w)
```

```python
import functools
import math

import numpy as np
import jax
import jax.numpy as jnp
from jax import lax
from jax.experimental import pallas as pl
from jax.experimental.pallas import tpu as pltpu

F32 = jnp.float32
BF16 = jnp.bfloat16

GRID_W = 64
HEAD_DIM = 64
GROUP_HEADS = 4
GROUP_W = GROUP_HEADS * HEAD_DIM
CHUNK = 64
NA_KH = 8
NA_KW = 16
DF_HEAD_DIM = HEAD_DIM // 2
ROPE_BASE = 10000.0
EPS = 1e-6
LOG2E = 1.4426950408889634

LANES = 128
SUBLANES = 8
VMEM_LIMIT_BYTES = 56 * 1024 * 1024

ROW_TILE = 256
NA_ROWS = 8
DF_QB = 512
DF_KC = 256
FF_SPLIT = 2

C_DN_Q, C_DN_K, C_DN_V, C_DN_GATE = 0, 1, 2, 3
C_NA_Q, C_NA_K, C_NA_V = 4, 5, 6
C_DF_Q, C_DF_K, C_DF_V = 7, 8, 9
C_HG_Q, C_HG_I, C_HG_FF, C_HG_FB, C_HG_GATE = 10, 11, 12, 13, 14
N_COL_BLOCKS = 15
P_WIDTH = N_COL_BLOCKS * GROUP_W + LANES
C_DN_BA_128 = N_COL_BLOCKS * GROUP_W // LANES


def _cparams(sem):
    return pltpu.CompilerParams(dimension_semantics=sem, vmem_limit_bytes=VMEM_LIMIT_BYTES)


def _mm(a, b):
    return jnp.dot(a, b, preferred_element_type=F32)


def _nt(a, b):
    return lax.dot_general(a, b, (((1,), (1,)), ((), ())), preferred_element_type=F32)


def _silu(x):
    return x * jax.nn.sigmoid(x)


def _split3(x):
    hi = x.astype(BF16)
    r = x - hi.astype(F32)
    mid = r.astype(BF16)
    lo = (r - mid.astype(F32)).astype(BF16)
    return hi, mid, lo


def _split2(x):
    hi = x.astype(BF16)
    lo = (x - hi.astype(F32)).astype(BF16)
    return hi, lo


def _head_masks(n_lanes=GROUP_W):
    lane = lax.broadcasted_iota(jnp.int32, (1, n_lanes), 1)
    return [(lane // HEAD_DIM == h).astype(F32) for h in range(GROUP_HEADS)]


def _mrs(x, hms):
    return jnp.concatenate([x * hm for hm in hms], axis=0)


def _rbs(z):
    n = z.shape[0] // GROUP_HEADS
    return z[0:n] + z[n:2 * n] + z[2 * n:3 * n] + z[3 * n:4 * n]


def _tile4(x):
    return jnp.concatenate([x, x, x, x], axis=0)


def _group_sum(x, g2_ref):
    hi, lo = _split2(x)
    return _mm(jnp.concatenate([hi, lo], axis=1), g2_ref[...])


def _rms_mod(x, nw, shift, scale):
    ms = jnp.mean(x * x, axis=-1, keepdims=True)
    return (x * lax.rsqrt(ms + EPS) * nw) * (1.0 + scale) + shift


def _ada_kernel(c_ref, w_ref, b_ref, o_ref):
    s = _silu(c_ref[...])
    o_ref[0] = _mm(s.astype(BF16), w_ref[0]) + b_ref[0]


def _ada_call(c_all, w_ada_bf, b_ada):
    depth, d, d6 = w_ada_bf.shape
    n_chunks = d6 // d
    return pl.pallas_call(
        _ada_kernel,
        out_shape=jax.ShapeDtypeStruct((depth, SUBLANES, d6), F32),
        grid=(depth, n_chunks),
        in_specs=[
            pl.BlockSpec((SUBLANES, d), lambda l, j: (0, 0)),
            pl.BlockSpec((1, d, d), lambda l, j: (l, 0, j)),
            pl.BlockSpec((1, 1, d), lambda l, j: (l, 0, j)),
        ],
        out_specs=pl.BlockSpec((1, SUBLANES, d), lambda l, j: (l, 0, j)),
        compiler_params=_cparams(("arbitrary", "arbitrary")),
        name="ada_ln",
    )(c_all, w_ada_bf, b_ada.reshape(depth, 1, d6))


def _inproj_kernel(x_ref, mod_ref, nw_ref, w_ref, o_ref):
    h = _rms_mod(x_ref[0], nw_ref[...], mod_ref[0, 0:1, :], mod_ref[0, 1:2, :])
    o_ref[0] = _mm(h.astype(BF16), w_ref[...])


def _inproj_call(xall, mod, nw, w_bf, nt_lat):
    b, t, d = xall.shape
    nt = t // ROW_TILE
    pw = w_bf.shape[1]
    return pl.pallas_call(
        _inproj_kernel,
        out_shape=jax.ShapeDtypeStruct((b, t, pw), F32),
        grid=(b, nt),
        in_specs=[
            pl.BlockSpec((1, ROW_TILE, d), lambda i, j: (i, j, 0)),
            pl.BlockSpec((1, 6, d), lambda i, j: (i * 2 + j // nt_lat, 0, 0)),
            pl.BlockSpec((1, d), lambda i, j: (0, 0)),
            pl.BlockSpec((d, pw), lambda i, j: (0, 0), pipeline_mode=pl.Buffered(1)),
        ],
        out_specs=pl.BlockSpec((1, ROW_TILE, pw), lambda i, j: (i, j, 0)),
        compiler_params=_cparams(("arbitrary", "arbitrary")),
        name="in_proj",
    )(xall, mod, nw, w_bf)


def _shift_rows(x, prev_row, next_row, gpos, seg_starts, seg_ends):
    n = x.shape[0]
    row = lax.broadcasted_iota(jnp.int32, (n, 1), 0)
    xm = jnp.where(row == 0, prev_row, pltpu.roll(x, 1, axis=0))
    xp = jnp.where(row == n - 1, next_row, pltpu.roll(x, n - 1, axis=0))
    at_start = functools.reduce(jnp.logical_or, [gpos == s for s in seg_starts])
    at_end = functools.reduce(jnp.logical_or, [gpos == e - 1 for e in seg_ends])
    return jnp.where(at_start, 0.0, xm), jnp.where(at_end, 0.0, xp)


def _dn_prep_kernel(x_ref, xp_ref, xn_ref, ba_ref, cw_ref, aneg_ref, dtb_ref, e3_ref, g2_ref,
                    qkv_ref, aux_ref, *, n_lat, n_all):
    tm = x_ref.shape[1]
    x = x_ref[0]
    gpos = pl.program_id(1) * tm + lax.broadcasted_iota(jnp.int32, (tm, 1), 0)
    xm, xp = _shift_rows(x, xp_ref[0, SUBLANES - 1:SUBLANES, :], xn_ref[0, 0:1, :], gpos,
                         (0, n_lat), (n_lat, n_all))
    cw = cw_ref[...]
    y = _silu(xm * cw[0:1] + x * cw[1:2] + xp * cw[2:3])
    q, k, v = y[:, 0:GROUP_W], y[:, GROUP_W:2 * GROUP_W], y[:, 2 * GROUP_W:3 * GROUP_W]
    qn = q * lax.rsqrt(_group_sum(q * q, g2_ref) + EPS) * (HEAD_DIM ** -0.5)
    kn = k * lax.rsqrt(_group_sum(k * k, g2_ref) + EPS)
    qkv_ref[0] = jnp.concatenate([qn, kn, v], axis=1)
    ba = ba_ref[0]
    lane = lax.broadcasted_iota(jnp.int32, ba.shape, 1)
    z = ba + dtb_ref[...]
    softplus = jnp.maximum(z, 0.0) + jnp.log(1.0 + jnp.exp(-jnp.abs(z)))
    gates = jnp.where(lane < 2 * GROUP_HEADS, jax.nn.sigmoid(ba), aneg_ref[...] * softplus)
    gates = jnp.where(lane < 4 * GROUP_HEADS, gates, 0.0)
    hi, mid, lo = _split3(gates)
    aux_ref[0] = _mm(jnp.concatenate([hi, mid, lo], axis=1), e3_ref[...])


def _halo_specs(width, col, n_rows_total, tm):
    per = tm // SUBLANES
    last = n_rows_total // SUBLANES - 1
    prev = pl.BlockSpec((1, SUBLANES, width), lambda i, j: (i, jnp.maximum(j * per - 1, 0), col))
    nxt = pl.BlockSpec((1, SUBLANES, width), lambda i, j: (i, jnp.minimum((j + 1) * per, last), col))
    return prev, nxt


def _dn_prep_call(p, conv_w, aneg, dtb, e3, g2, n_lat):
    b, t, _ = p.shape
    nt = t // ROW_TILE
    w3 = 3 * GROUP_W
    prev, nxt = _halo_specs(w3, 0, t, ROW_TILE)
    kern = functools.partial(_dn_prep_kernel, n_lat=n_lat, n_all=t)
    return pl.pallas_call(
        kern,
        out_shape=(jax.ShapeDtypeStruct((b, t, w3), F32), jax.ShapeDtypeStruct((b, t, 4 * GROUP_W), F32)),
        grid=(b, nt),
        in_specs=[
            pl.BlockSpec((1, ROW_TILE, w3), lambda i, j: (i, j, 0)),
            prev, nxt,
            pl.BlockSpec((1, ROW_TILE, LANES), lambda i, j: (i, j, C_DN_BA_128)),
            pl.BlockSpec((3, w3), lambda i, j: (0, 0)),
            pl.BlockSpec((1, LANES), lambda i, j: (0, 0)),
            pl.BlockSpec((1, LANES), lambda i, j: (0, 0)),
            pl.BlockSpec(e3.shape, lambda i, j: (0, 0)),
            pl.BlockSpec(g2.shape, lambda i, j: (0, 0)),
        ],
        out_specs=(pl.BlockSpec((1, ROW_TILE, w3), lambda i, j: (i, j, 0)),
                   pl.BlockSpec((1, ROW_TILE, 4 * GROUP_W), lambda i, j: (i, j, 0))),
        compiler_params=_cparams(("arbitrary", "arbitrary")),
        name="dn_prep",
    )(p, p, p, p, conv_w, aneg, dtb, e3, g2)


def _scan_tile(d, s, nt_lat):
    return jnp.where(s == 0, nt_lat, jnp.where(d == 0, s - 1, nt_lat - s))


def _unit_tri_inverse(a, lv_ref):
    n = a.shape[0]
    eye = (lax.broadcasted_iota(jnp.int32, (n, n), 0) == lax.broadcasted_iota(jnp.int32, (n, n), 1)).astype(F32)
    n1 = a * lv_ref[0]
    n1b = n1.astype(BF16)
    n2 = _mm(n1b, n1b)
    n2b = n2.astype(BF16)
    n4 = _mm(n2b, n2b)
    t = _mm((eye - n1).astype(BF16), _mm((eye + n2).astype(BF16), (eye + n4).astype(BF16)).astype(BF16))
    for lvl in (1, 2, 3):
        tb = t.astype(BF16)
        x = _mm((a * lv_ref[lvl]).astype(BF16), tb)
        t = t - _mm(tb, x.astype(BF16))
    return t


def _dn_scan_kernel(qkv_ref, beta_ref, la_ref, dm_ref, lv_ref, tri_ref, o_ref, s_ref):
    d = pl.program_id(1)

    @pl.when(pl.program_id(2) == 0)
    def _():
        s_ref[...] = jnp.zeros_like(s_ref)

    hms = _head_masks()
    incl = dm_ref[0, 0]
    strict = dm_ref[0, 1]
    n_chunks = qkv_ref.shape[1] // CHUNK
    for i in range(n_chunks):
        r0 = pl.multiple_of(jnp.where(d == 0, i, n_chunks - 1 - i) * CHUNK, CHUNK)
        rows = pl.ds(r0, CHUNK)
        q = qkv_ref[0, rows, 0:GROUP_W]
        k = qkv_ref[0, rows, GROUP_W:2 * GROUP_W]
        v = qkv_ref[0, rows, 2 * GROUP_W:3 * GROUP_W]
        beta = beta_ref[0, rows, :]
        hi, mid, lo = _split3(la_ref[0, rows, :])
        g = _mm(tri_ref[0], jnp.concatenate([hi, mid, lo], axis=0))
        gtot = jnp.where(d == 0, g[CHUNK - 1:CHUNK, :], g[0:1, :])
        eg = jnp.exp(g)
        rg = _tile4(g)
        decay = jnp.exp((rg - rg.T) * incl) * incl
        mk = _mrs(k, hms).astype(BF16)
        kk = _nt(mk, mk)
        qk = _nt(_mrs(q, hms).astype(BF16), mk)
        a = strict * _tile4(beta) * kk * decay
        tinv = _unit_tri_inverse(a, lv_ref).astype(BF16)
        w_bd = _mm(tinv, _mrs(k * beta * eg, hms).astype(BF16))
        u_bd = _mm(tinv, _mrs(v * beta, hms).astype(BF16))
        sv = s_ref[...]
        svb = sv.astype(BF16)
        v_new = u_bd - _mm(w_bd.astype(BF16), svb)
        v_newb = v_new.astype(BF16)
        o_bd = _mm(_mrs(q * eg, hms).astype(BF16), svb) + _mm((qk * decay).astype(BF16), v_newb)
        o_ref[0, 0, rows, :] = _rbs(o_bd)
        kdec = _mrs(k * jnp.exp(gtot - g), hms)
        s_ref[...] = sv * jnp.exp(gtot) + _mm(kdec.T.astype(BF16), v_newb)


def _dn_scan_call(qkv, aux, dmask, lvmask, tri3, nt_lat):
    b, t, w3 = qkv.shape
    ns = t // ROW_TILE
    return pl.pallas_call(
        _dn_scan_kernel,
        out_shape=jax.ShapeDtypeStruct((b, 2, t, GROUP_W), F32),
        grid=(b, 2, ns),
        in_specs=[
            pl.BlockSpec((1, ROW_TILE, w3), lambda i, d, s: (i, _scan_tile(d, s, nt_lat), 0)),
            pl.BlockSpec((1, ROW_TILE, GROUP_W), lambda i, d, s: (i, _scan_tile(d, s, nt_lat), d)),
            pl.BlockSpec((1, ROW_TILE, GROUP_W), lambda i, d, s: (i, _scan_tile(d, s, nt_lat), 2 + d)),
            pl.BlockSpec((1,) + dmask.shape[1:], lambda i, d, s: (d, 0, 0, 0)),
            pl.BlockSpec(lvmask.shape, lambda i, d, s: (0, 0, 0)),
            pl.BlockSpec((1,) + tri3.shape[1:], lambda i, d, s: (d, 0, 0)),
        ],
        out_specs=pl.BlockSpec((1, 1, ROW_TILE, GROUP_W), lambda i, d, s: (i, d, _scan_tile(d, s, nt_lat), 0)),
        scratch_shapes=[pltpu.VMEM((GROUP_W, GROUP_W), F32)],
        compiler_params=_cparams(("arbitrary", "arbitrary", "arbitrary")),
        name="dn_scan",
    )(qkv, aux, aux, dmask, lvmask, tri3)


HG_LEVELS = (1, 2, 4, 8, 16, 32)


def _hg_scan_kernel(q_ref, i_ref, f_ref, lb_ref, w3_ref, lm_ref, same_ref, o_ref, st_ref, qb_ref, ke_ref, oi_ref):
    d = pl.program_id(1)

    @pl.when(pl.program_id(2) == 0)
    def _():
        st_ref[...] = jnp.zeros_like(st_ref)

    hms = _head_masks()
    same = same_ref[...]
    lb = lb_ref[0]
    tm = q_ref.shape[1]
    n_chunks = tm // CHUNK
    nl = len(HG_LEVELS)
    tots = []
    for c in range(n_chunks):
        rows = slice(c * CHUNK, (c + 1) * CHUNK)
        q = q_ref[0, rows, :] * (HEAD_DIM ** -0.5)
        v = i_ref[0, rows, :]
        f = lb + (1.0 - lb) * jax.nn.sigmoid(f_ref[0, rows, :])
        kx = 1.0 - f
        hi, mid, lo = _split3(jnp.log(f))
        e = _mm(w3_ref[0], jnp.concatenate([hi, mid, lo], axis=0))
        cum = e[0:CHUNK]
        tot = e[CHUNK:2 * CHUNK]
        att = lm_ref[0, 0] * _nt(_mrs(q, hms).astype(BF16), kx.astype(BF16))
        for li in range(nl):
            ea = e[(2 + li) * CHUNK:(3 + li) * CHUNK]
            eb = e[(2 + nl + li) * CHUNK:(3 + nl + li) * CHUNK]
            att = att + lm_ref[0, 1 + li] * _nt(_mrs(q * jnp.exp(ea), hms).astype(BF16),
                                                 (kx * jnp.exp(eb)).astype(BF16))
        oi_ref[rows, :] = _rbs(_mm(att.astype(BF16), v.astype(BF16)) * same)
        qb_ref[rows, :] = q * jnp.exp(cum)
        ke_ref[rows, :] = kx * jnp.exp(tot - cum)
        tots.append(tot[0:1, :])
    vt = i_ref[0].T
    lane = lax.broadcasted_iota(jnp.int32, (1, tm), 1)
    keb = ke_ref[...].astype(BF16)
    for i in range(n_chunks):
        c = jnp.where(d == 0, i, n_chunks - 1 - i)
        rows = pl.ds(pl.multiple_of(c * CHUNK, CHUNK), CHUNK)
        st = st_ref[...]
        o_ref[0, 0, rows, :] = oi_ref[rows, :] + _nt(qb_ref[rows, :].astype(BF16), st.astype(BF16))
        tot = functools.reduce(lambda acc, j: jnp.where(c == j, tots[j], acc), range(1, n_chunks), tots[0])
        vt_c = (vt * (lane // CHUNK == c).astype(F32)).astype(BF16)
        st_ref[...] = st * jnp.exp(tot) + same * _mm(vt_c, keb)


def _hg_scan_call(p, lb, w3, lmask, same, nt_lat):
    b, t, _ = p.shape
    ns = t // ROW_TILE
    blk = lambda col: pl.BlockSpec((1, ROW_TILE, GROUP_W), lambda i, d, s: (i, _scan_tile(d, s, nt_lat), col))
    return pl.pallas_call(
        _hg_scan_kernel,
        out_shape=jax.ShapeDtypeStruct((b, 2, t, GROUP_W), F32),
        grid=(b, 2, ns),
        in_specs=[
            blk(C_HG_Q), blk(C_HG_I),
            pl.BlockSpec((1, ROW_TILE, GROUP_W), lambda i, d, s: (i, _scan_tile(d, s, nt_lat), C_HG_FF + d)),
            pl.BlockSpec((1, 1, GROUP_W), lambda i, d, s: (d, 0, 0)),
            pl.BlockSpec((1,) + w3.shape[1:], lambda i, d, s: (d, 0, 0)),
            pl.BlockSpec((1,) + lmask.shape[1:], lambda i, d, s: (d, 0, 0, 0)),
            pl.BlockSpec(same.shape, lambda i, d, s: (0, 0)),
        ],
        out_specs=pl.BlockSpec((1, 1, ROW_TILE, GROUP_W), lambda i, d, s: (i, d, _scan_tile(d, s, nt_lat), 0)),
        scratch_shapes=[pltpu.VMEM((GROUP_W, GROUP_W), F32), pltpu.VMEM((ROW_TILE, GROUP_W), F32),
                        pltpu.VMEM((ROW_TILE, GROUP_W), F32), pltpu.VMEM((ROW_TILE, GROUP_W), F32)],
        compiler_params=_cparams(("arbitrary", "arbitrary", "arbitrary")),
        name="hg_scan",
    )(p, p, p, lb, w3, lmask, same)


def _na_kernel(q_ref, kp_ref, kc_ref, kn_ref, vp_ref, vc_ref, vn_ref, kx_ref, vx_ref, bias_ref, same_ref,
               o_ref, kw_ref, vw_ref, *, n_rows):
    i = pl.program_id(1)
    blk = q_ref.shape[1]
    for j, (kr, vr) in enumerate(((kp_ref, vp_ref), (kc_ref, vc_ref), (kn_ref, vn_ref))):
        kw_ref[j * blk:(j + 1) * blk, :] = kr[0].astype(BF16)
        vw_ref[j * blk:(j + 1) * blk, :] = vr[0].astype(BF16)
    kx = kx_ref[0].astype(BF16)
    vx = vx_ref[0].astype(BF16)
    hms = _head_masks()
    same = same_ref[...]
    win = NA_KH * GRID_W

    def row_body(rr, carry):
        r = i * NA_ROWS + rr
        rs = jnp.clip(r - NA_KH // 2, 0, n_rows - NA_KH)
        start = pl.multiple_of((rs - i * NA_ROWS + NA_ROWS) * GRID_W, GRID_W)
        rows = pl.ds(pl.multiple_of(rr * GRID_W, GRID_W), GRID_W)
        mq = _mrs(q_ref[0, rows, :] * (HEAD_DIM ** -0.5), hms).astype(BF16)
        s_loc = _nt(mq, kw_ref[pl.ds(start, win), :]) + bias_ref[rs - r + NA_KH - 1]
        s_ctx = _nt(mq, kx)
        m = jnp.maximum(jnp.max(s_loc, axis=-1, keepdims=True), jnp.max(s_ctx, axis=-1, keepdims=True))
        p_loc = jnp.exp(s_loc - m)
        p_ctx = jnp.exp(s_ctx - m)
        den = jnp.sum(p_loc, axis=-1, keepdims=True) + jnp.sum(p_ctx, axis=-1, keepdims=True)
        o = (_mm(p_loc.astype(BF16), vw_ref[pl.ds(start, win), :]) + _mm(p_ctx.astype(BF16), vx)) / den
        o_ref[0, rows, :] = _rbs(o * same)
        return carry

    lax.fori_loop(0, NA_ROWS, row_body, 0)


def _na_call(p, bias, same, n_lat, n_ctx):
    b, t, _ = p.shape
    blk = NA_ROWS * GRID_W
    nb = n_lat // blk
    spec = lambda col, f: pl.BlockSpec((1, blk, GROUP_W), lambda i, j: (i, f(j), col))
    prev = lambda j: jnp.maximum(j - 1, 0)
    cur = lambda j: j
    nxt = lambda j: jnp.minimum(j + 1, nb - 1)
    ctx_spec = lambda col: pl.BlockSpec((1, n_ctx, GROUP_W), lambda i, j: (i, n_lat // n_ctx, col))
    kern = functools.partial(_na_kernel, n_rows=n_lat // GRID_W)
    return pl.pallas_call(
        kern,
        out_shape=jax.ShapeDtypeStruct((b, t, GROUP_W), F32),
        grid=(b, nb),
        in_specs=[spec(C_NA_Q, cur),
                  spec(C_NA_K, prev), spec(C_NA_K, cur), spec(C_NA_K, nxt),
                  spec(C_NA_V, prev), spec(C_NA_V, cur), spec(C_NA_V, nxt),
                  ctx_spec(C_NA_K), ctx_spec(C_NA_V),
                  pl.BlockSpec(bias.shape, lambda i, j: (0, 0, 0), pipeline_mode=pl.Buffered(1)),
                  pl.BlockSpec(same.shape, lambda i, j: (0, 0))],
        out_specs=pl.BlockSpec((1, blk, GROUP_W), lambda i, j: (i, j, 0)),
        scratch_shapes=[pltpu.VMEM((3 * blk, GROUP_W), BF16), pltpu.VMEM((3 * blk, GROUP_W), BF16)],
        compiler_params=_cparams(("arbitrary", "arbitrary")),
        name="na_attn",
    )(p, p, p, p, p, p, p, p, p, bias, same)


def _na_ctx_kernel(q_ref, k_ref, v_ref, y_in_ref, o_ref):
    del y_in_ref
    hms = _head_masks()
    n = q_ref.shape[1]
    mq = _mrs(q_ref[0] * (HEAD_DIM ** -0.5), hms).astype(BF16)
    s = _nt(mq, k_ref[0].astype(BF16))
    p = jnp.exp(s - jnp.max(s, axis=-1, keepdims=True))
    o = _mm(p.astype(BF16), v_ref[0].astype(BF16)) / jnp.sum(p, axis=-1, keepdims=True)
    o_ref[0] = _rbs(o * jnp.concatenate([jnp.broadcast_to(hm, (n, GROUP_W)) for hm in hms], axis=0))


def _na_ctx_call(p, y_na, n_lat, n_ctx):
    b, t, _ = p.shape
    spec = lambda col: pl.BlockSpec((1, n_ctx, GROUP_W), lambda i: (i, n_lat // n_ctx, col))
    return pl.pallas_call(
        _na_ctx_kernel,
        out_shape=jax.ShapeDtypeStruct(y_na.shape, y_na.dtype),
        grid=(b,),
        in_specs=[spec(C_NA_Q), spec(C_NA_K), spec(C_NA_V),
                  pl.BlockSpec(memory_space=pl.ANY)],
        out_specs=pl.BlockSpec((1, n_ctx, GROUP_W), lambda i: (i, n_lat // n_ctx, 0)),
        input_output_aliases={3: 0},
        compiler_params=_cparams(("arbitrary",)),
        name="na_ctx",
    )(p, p, p, y_na)


def _df_prep_kernel(q_ref, k_ref, v_ref, cos_ref, sin_ref, qt_ref, kr_ref, vt_ref):
    cos = cos_ref[...]
    sin = sin_ref[...]
    lane = lax.broadcasted_iota(jnp.int32, cos.shape, 1)
    first = (lane % 16) < 8

    def rope(x):
        w = x.shape[1]
        partner = jnp.where(first, pltpu.roll(x, w - 8, axis=1), pltpu.roll(x, 8, axis=1))
        return x * cos + partner * sin

    tm = q_ref.shape[1]
    qt = (rope(q_ref[0]) * (DF_HEAD_DIM ** -0.5 * LOG2E)).T
    row = lax.broadcasted_iota(jnp.int32, (GROUP_W, 1), 0)
    for s in range(2 * GROUP_HEADS):
        keep = (row // DF_HEAD_DIM == s).astype(F32)
        qt_ref[0, s] = (qt * keep).astype(BF16)
    kr_ref[0, 0] = rope(k_ref[0]).astype(BF16)
    vt = v_ref[0].T
    ext = (lax.broadcasted_iota(jnp.int32, (2 * SUBLANES, tm), 0) == 0).astype(F32)
    for h in range(GROUP_HEADS):
        vt_ref[0, h, 0] = jnp.concatenate([vt[h * HEAD_DIM:(h + 1) * HEAD_DIM], ext], axis=0).astype(BF16)


def _df_prep_call(p, cos, sin):
    b, t, _ = p.shape
    nt = t // DF_KC
    vrows = HEAD_DIM + 2 * SUBLANES
    blk = lambda col: pl.BlockSpec((1, DF_KC, GROUP_W), lambda i, j: (i, j, col))
    return pl.pallas_call(
        _df_prep_kernel,
        out_shape=(jax.ShapeDtypeStruct((b, 2 * GROUP_HEADS, GROUP_W, t), BF16),
                   jax.ShapeDtypeStruct((b, nt, DF_KC, GROUP_W), BF16),
                   jax.ShapeDtypeStruct((b, GROUP_HEADS, nt, vrows, DF_KC), BF16)),
        grid=(b, nt),
        in_specs=[blk(C_DF_Q), blk(C_DF_K), blk(C_DF_V),
                  pl.BlockSpec((DF_KC, GROUP_W), lambda i, j: (j, 0)),
                  pl.BlockSpec((DF_KC, GROUP_W), lambda i, j: (j, 0))],
        out_specs=(pl.BlockSpec((1, 2 * GROUP_HEADS, GROUP_W, DF_KC), lambda i, j: (i, 0, 0, j)),
                   pl.BlockSpec((1, 1, DF_KC, GROUP_W), lambda i, j: (i, j, 0, 0)),
                   pl.BlockSpec((1, GROUP_HEADS, 1, vrows, DF_KC), lambda i, j: (i, 0, j, 0, 0))),
        compiler_params=_cparams(("arbitrary", "arbitrary")),
        name="df_prep",
    )(p, p, p, cos, sin)


def _df_attn_kernel(qt_ref, k_ref, vt_ref, lam_ref, nw_ref, y_in_ref, o_ref, m_ref, acc_ref, *, lam_init):
    del y_in_ref
    nk = k_ref.shape[1]
    m_ref[...] = jnp.full(m_ref.shape, -jnp.inf, F32)
    acc_ref[...] = jnp.zeros_like(acc_ref)

    def body(c, carry):
        kc = k_ref[0, c]
        vt = vt_ref[0, 0, c]
        for j in range(2):
            s = _mm(kc, qt_ref[0, j])
            m_old = m_ref[j]
            m_new = jnp.maximum(m_old, jnp.max(s, axis=0, keepdims=True))
            p = jnp.exp2(s - m_new)
            acc_ref[j] = acc_ref[j] * jnp.exp2(m_old - m_new) + _mm(vt, p.astype(BF16))
            m_ref[j] = m_new
        return carry

    lax.fori_loop(0, nk, body, 0)
    lp = lam_ref[...]
    lam = (jnp.exp(jnp.sum(lp[0:1] * lp[1:2], keepdims=True)) - jnp.exp(jnp.sum(lp[2:3] * lp[3:4], keepdims=True))
           + lam_init)
    a0 = acc_ref[0]
    a1 = acc_ref[1]
    o = a0[0:HEAD_DIM] / a0[HEAD_DIM:HEAD_DIM + 1] - lam * (a1[0:HEAD_DIM] / a1[HEAD_DIM:HEAD_DIM + 1])
    ms = jnp.mean(o * o, axis=0, keepdims=True)
    o_ref[0] = o * lax.rsqrt(ms + EPS) * nw_ref[...] * (1.0 - lam_init)


def _df_attn_call(qt, kr, vt, lam_p, nw_col, y_t, lam_init, q_off, n_q, qb, k_off, nk):
    b = qt.shape[0]
    vrows = vt.shape[3]
    kern = functools.partial(_df_attn_kernel, lam_init=lam_init)
    out_shape = jax.ShapeDtypeStruct((b, GROUP_W, qt.shape[3]), F32)
    in_specs = [
        pl.BlockSpec((1, 2, GROUP_W, qb), lambda i, h, q: (i, h, 0, q_off // qb + q)),
        pl.BlockSpec((1, nk, DF_KC, GROUP_W), lambda i, h, q: (i, k_off // nk, 0, 0), pipeline_mode=pl.Buffered(1)),
        pl.BlockSpec((1, 1, nk, vrows, DF_KC), lambda i, h, q: (i, h, k_off // nk, 0, 0)),
        pl.BlockSpec(lam_p.shape, lambda i, h, q: (0, 0)),
        pl.BlockSpec(nw_col.shape, lambda i, h, q: (0, 0)),
    ]
    args = [qt, kr, vt, lam_p, nw_col]
    aliases = {}
    if y_t is None:
        kern_fn = lambda a, b_, c, d, e, o, m, acc: kern(a, b_, c, d, e, None, o, m, acc)
    else:
        kern_fn = kern
        in_specs.append(pl.BlockSpec(memory_space=pl.ANY))
        args.append(y_t)
        aliases = {5: 0}
    return pl.pallas_call(
        kern_fn,
        out_shape=out_shape,
        grid=(b, GROUP_HEADS, n_q // qb),
        in_specs=in_specs,
        out_specs=pl.BlockSpec((1, HEAD_DIM, qb), lambda i, h, q: (i, h, q_off // qb + q)),
        scratch_shapes=[pltpu.VMEM((2, 1, qb), F32), pltpu.VMEM((2, vrows, qb), F32)],
        input_output_aliases=aliases,
        compiler_params=_cparams(("arbitrary", "arbitrary", "arbitrary")),
        name="df_attn",
    )(*args)


def _outproj_kernel(x_ref, mod_ref, dno_f, dno_b, dng_ref, na_ref, dft_ref, hgo_f, hgo_b, hgg_ref,
                    dnw_ref, hgw_ref, g2_ref, w_ref, o_ref):
    inv = 1.0 / HEAD_DIM

    def gated_norm(o, nw, gate):
        ms = _group_sum(o * o, g2_ref) * inv
        return o * lax.rsqrt(ms + EPS) * nw * _silu(gate)

    y_dn = gated_norm(dno_f[0, 0] + dno_b[0, 0], dnw_ref[...], dng_ref[0])
    y_hg = gated_norm(hgo_f[0, 0] + hgo_b[0, 0], hgw_ref[...], hgg_ref[0])
    y_df = dft_ref[0].T
    acc = _mm(y_dn.astype(BF16), w_ref[0:GROUP_W, :])
    acc += _mm(na_ref[0].astype(BF16), w_ref[GROUP_W:2 * GROUP_W, :])
    acc += _mm(y_df.astype(BF16), w_ref[2 * GROUP_W:3 * GROUP_W, :])
    acc += _mm(y_hg.astype(BF16), w_ref[3 * GROUP_W:4 * GROUP_W, :])
    o_ref[0] = x_ref[0] + mod_ref[0, 2:3, :] * acc


def _outproj_call(xall, mod, dn_o, p, y_na, y_dft, hg_o, dnw, hgw, g2, w_bf, nt_lat, n_tiles):
    b, t, d = xall.shape
    row = lambda w, col=0: pl.BlockSpec((1, ROW_TILE, w), lambda i, j: (i, j, col))
    dirspec = lambda dd: pl.BlockSpec((1, 1, ROW_TILE, GROUP_W), lambda i, j: (i, dd, j, 0))
    const = lambda a: pl.BlockSpec(a.shape, lambda i, j: (0,) * a.ndim)
    return pl.pallas_call(
        _outproj_kernel,
        out_shape=jax.ShapeDtypeStruct((b, t, d), F32),
        grid=(b, n_tiles),
        in_specs=[row(d),
                  pl.BlockSpec((1, 6, d), lambda i, j: (i * 2 + j // nt_lat, 0, 0)),
                  dirspec(0), dirspec(1), row(GROUP_W, C_DN_GATE),
                  row(GROUP_W),
                  pl.BlockSpec((1, GROUP_W, ROW_TILE), lambda i, j: (i, 0, j)),
                  dirspec(0), dirspec(1), row(GROUP_W, C_HG_GATE),
                  const(dnw), const(hgw), const(g2),
                  pl.BlockSpec(w_bf.shape, lambda i, j: (0, 0), pipeline_mode=pl.Buffered(1))],
        out_specs=row(d),
        compiler_params=_cparams(("arbitrary", "arbitrary")),
        name="out_proj",
    )(xall, mod, dn_o, dn_o, p, y_na, y_dft, hg_o, hg_o, p, dnw, hgw, g2, w_bf)


def _ffn_kernel(x_ref, xp_ref, xn_ref, mod_ref, nw_ref, wup_ref, cw_ref, wdn_ref, fw_ref, o_ref,
                *, n_lat, n_all, final):
    tm = x_ref.shape[1]
    d_ff = wdn_ref.shape[0]
    fc = d_ff // FF_SPLIT
    x = x_ref[0]
    xe = jnp.concatenate([x, xp_ref[0], xn_ref[0]], axis=0)
    hb = _rms_mod(xe, nw_ref[...], mod_ref[0, 3:4, :], mod_ref[0, 4:5, :]).astype(BF16)
    gpos = pl.program_id(1) * tm + lax.broadcasted_iota(jnp.int32, (tm, 1), 0)
    acc = jnp.zeros((tm, x.shape[1]), F32)
    for f in range(FF_SPLIT):
        g_all = _mm(hb, wup_ref[:, f * fc:(f + 1) * fc])
        val = _mm(hb[0:tm], wup_ref[:, d_ff + f * fc:d_ff + (f + 1) * fc])
        g = g_all[0:tm]
        gm, gp = _shift_rows(g, g_all[tm + SUBLANES - 1:tm + SUBLANES], g_all[tm + SUBLANES:tm + SUBLANES + 1],
                             gpos, (0, n_lat), (n_lat, n_all))
        cw = cw_ref[:, f * fc:(f + 1) * fc]
        a = _silu(gm * cw[0:1] + g * cw[1:2] + gp * cw[2:3]) * val
        acc += _mm(a.astype(BF16), wdn_ref[f * fc:(f + 1) * fc, :])
    y = x + mod_ref[0, 5:6, :] * acc
    if final:
        ms = jnp.mean(y * y, axis=-1, keepdims=True)
        y = y * lax.rsqrt(ms + EPS) * fw_ref[...]
    o_ref[0] = y


def _ffn_call(xmid, mod, nw, wup_bf, cw, wdn_bf, fw, nt_lat, n_tiles, n_lat, final):
    b, t, d = xmid.shape
    out_rows = n_tiles * ROW_TILE
    prev, nxt = _halo_specs(d, 0, out_rows, ROW_TILE)
    kern = functools.partial(_ffn_kernel, n_lat=n_lat, n_all=t, final=final)
    const = lambda a, **kw: pl.BlockSpec(a.shape, lambda i, j: (0,) * a.ndim, **kw)
    return pl.pallas_call(
        kern,
        out_shape=jax.ShapeDtypeStruct((b, out_rows, d), F32),
        grid=(b, n_tiles),
        in_specs=[pl.BlockSpec((1, ROW_TILE, d), lambda i, j: (i, j, 0)), prev, nxt,
                  pl.BlockSpec((1, 6, d), lambda i, j: (i * 2 + j // nt_lat, 0, 0)),
                  const(nw), const(wup_bf, pipeline_mode=pl.Buffered(1)), const(cw),
                  const(wdn_bf, pipeline_mode=pl.Buffered(1)), const(fw)],
        out_specs=pl.BlockSpec((1, ROW_TILE, d), lambda i, j: (i, j, 0)),
        compiler_params=_cparams(("arbitrary", "arbitrary")),
        name="conv_ffn",
    )(xmid, xmid, xmid, mod, nw, wup_bf, cw, wdn_bf, fw)


def _scan_constants():
    c = CHUNK
    n = GROUP_W
    t = np.arange(c)
    tri_f = (t[None, :] <= t[:, None]).astype(np.float32)
    tri = np.stack([tri_f, tri_f[::-1, ::-1]])
    tri3 = np.concatenate([tri] * 3, axis=2)

    idx = np.arange(n)
    hh, tt = idx // c, idx % c
    same = hh[:, None] == hh[None, :]
    ti, si = tt[:, None], tt[None, :]
    dmask = np.stack([np.stack([same & (si <= ti), same & (si < ti)]),
                      np.stack([same & (si >= ti), same & (si > ti)])]).astype(np.float32)
    lv = [same & (ti // 8 == si // 8)]
    for m in (8, 16, 32):
        lv.append(same & (ti // (2 * m) == si // (2 * m)) & (ti // m != si // m))
    lvmask = np.stack(lv).astype(np.float32)

    u = t[None, :]
    ws = [tri_f, np.ones((c, c), np.float32)]
    for m in HG_LEVELS:
        r = (t // m) * m
        ws.append(((u > r[:, None]) & (u <= t[:, None])).astype(np.float32))
    for m in HG_LEVELS:
        r2 = np.minimum((t // m + 1) * m, c - 1)
        ws.append(((u > t[:, None]) & (u <= r2[:, None])).astype(np.float32))
    w_f = np.concatenate(ws, axis=0)
    w_b = np.concatenate([w[::-1, ::-1] for w in ws], axis=0)
    w3 = np.stack([np.concatenate([w_f] * 3, axis=1), np.concatenate([w_b] * 3, axis=1)])

    lms = [np.eye(c, dtype=bool)]
    for m in HG_LEVELS:
        lms.append(((t[:, None] // m) % 2 == 1) & (t[None, :] // m == t[:, None] // m - 1))
    lm_f = np.stack(lms)
    lm_b = lm_f[:, ::-1, ::-1]
    lmask = np.stack([np.tile(lm_f, (1, GROUP_HEADS, 1)), np.tile(lm_b, (1, GROUP_HEADS, 1))]).astype(np.float32)

    g = same.astype(np.float32)
    g2 = np.concatenate([g, g], axis=0)
    e = np.zeros((LANES, 4 * GROUP_W), np.float32)
    for j in range(4 * GROUP_HEADS):
        e[j, j * HEAD_DIM:(j + 1) * HEAD_DIM] = 1.0
    e3 = np.concatenate([e, e, e], axis=0)
    return dict(tri3=jnp.asarray(tri3, BF16), dmask=jnp.asarray(dmask), lvmask=jnp.asarray(lvmask),
                w3=jnp.asarray(w3, BF16), lmask=jnp.asarray(lmask), same=jnp.asarray(g),
                g2=jnp.asarray(g2, BF16), e3=jnp.asarray(e3, BF16))


def _na_bias_table(rpb):
    cols = np.arange(GRID_W)
    c_start = np.clip(cols - NA_KW // 2, 0, GRID_W - NA_KW)
    kc = np.arange(GRID_W)
    valid = (kc[None, :] >= c_start[:, None]) & (kc[None, :] < c_start[:, None] + NA_KW)
    dc = np.clip(kc[None, :] - cols[:, None] + NA_KW - 1, 0, 2 * NA_KW - 2)
    variants = []
    for var in range(NA_KH):
        dr = var - (NA_KH - 1) + np.arange(NA_KH) + (NA_KH - 1)
        t = rpb[:, dr][:, :, dc]
        t = jnp.where(valid[None, None], t, -jnp.inf)
        variants.append(t.transpose(0, 2, 1, 3).reshape(GROUP_HEADS * GRID_W, NA_KH * GRID_W))
    return jnp.stack(variants).astype(F32)


def _rope_tables(n_lat, n_ctx):
    pos = jnp.arange(n_lat)
    row = (pos // GRID_W).astype(F32)
    col = (pos % GRID_W).astype(F32)
    half = DF_HEAD_DIM // 2
    inv = 1.0 / (ROPE_BASE ** (jnp.arange(0, half, 2, dtype=F32) / half))
    lane = np.arange(GROUP_W)
    use_row = jnp.asarray((lane % DF_HEAD_DIM) < half)
    ang = jnp.where(use_row[None, :], row[:, None], col[:, None]) * inv[lane % (half // 2)][None, :]
    sign = jnp.asarray(np.where((lane % half) < half // 2, -1.0, 1.0), F32)
    cos = jnp.concatenate([jnp.cos(ang), jnp.ones((n_ctx, GROUP_W), F32)], axis=0)
    sin = jnp.concatenate([jnp.sin(ang) * sign[None, :], jnp.zeros((n_ctx, GROUP_W), F32)], axis=0)
    return cos, sin


def _reorder_w_in(w_in_l):
    d = w_in_l.shape[0]
    a = 4 * GROUP_W
    ba = w_in_l[:, a:a + 4 * GROUP_HEADS]
    rest = w_in_l[:, a + 4 * GROUP_HEADS:]
    pad = jnp.zeros((d, LANES - 4 * GROUP_HEADS), w_in_l.dtype)
    return jnp.concatenate([w_in_l[:, :a], rest, ba, pad], axis=1)


def kernel(x, c, ctx, c_ctx, w_ada, b_ada, norm1_w, norm2_w, w_in, dn_conv_w, dn_a_log, dn_dt_bias, dn_norm_w,
           na_rpb, df_lambda, df_norm_w, hg_lb_raw, hg_norm_w, w_out, w_up, ffn_conv_w, w_down, final_norm_w):
    b, n_lat, d = x.shape
    n_ctx = ctx.shape[1]
    depth = w_ada.shape[0]
    t = n_lat + n_ctx
    assert n_ctx == ROW_TILE == DF_KC and n_lat % (NA_ROWS * GRID_W) == 0 and n_lat % DF_QB == 0
    assert b <= SUBLANES - 1
    nt_lat = n_lat // ROW_TILE
    nt_all = t // ROW_TILE

    consts = _scan_constants()
    cos, sin = _rope_tables(n_lat, n_ctx)

    c_all = jnp.zeros((SUBLANES, d), F32).at[:b].set(c).at[b].set(c_ctx)
    ada = _ada_call(c_all, w_ada.astype(BF16), b_ada).reshape(depth, SUBLANES, 6, d)

    lb_all = jnp.cumsum(jax.nn.softmax(hg_lb_raw.astype(F32), axis=0), axis=0)
    lb_all = (lb_all - lb_all[0]).reshape(depth, 2, 1, GROUP_W)

    xall = jnp.concatenate([x, ctx], axis=1)
    out = None
    for l in range(depth):
        need_ctx = l < depth - 1
        n_tiles = nt_all if need_ctx else nt_lat
        lam_init = 0.8 - 0.6 * math.exp(-0.3 * l)
        mod = jnp.stack([ada[l, :b], jnp.broadcast_to(ada[l, b], (b, 6, d))], axis=1).reshape(2 * b, 6, d)

        p = _inproj_call(xall, mod, norm1_w[l].reshape(1, d), _reorder_w_in(w_in[l]).astype(BF16), nt_lat)

        lane = np.arange(LANES)
        gate_idx = np.clip(lane - 2 * GROUP_HEADS, 0, 2 * GROUP_HEADS - 1)
        is_decay = jnp.asarray((lane >= 2 * GROUP_HEADS) & (lane < 4 * GROUP_HEADS))
        aneg = jnp.where(is_decay, -jnp.exp(dn_a_log[l].astype(F32).reshape(-1)[gate_idx]), 0.0).reshape(1, LANES)
        dtb = jnp.where(is_decay, dn_dt_bias[l].astype(F32).reshape(-1)[gate_idx], 0.0).reshape(1, LANES)
        dn_qkv, dn_aux = _dn_prep_call(p, dn_conv_w[l], aneg, dtb, consts["e3"], consts["g2"], n_lat)
        dn_o = _dn_scan_call(dn_qkv, dn_aux, consts["dmask"], consts["lvmask"], consts["tri3"], nt_lat)

        hg_o = _hg_scan_call(p, lb_all[l], consts["w3"], consts["lmask"], consts["same"], nt_lat)

        y_na = _na_call(p, _na_bias_table(na_rpb[l]), consts["same"], n_lat, n_ctx)
        if need_ctx:
            y_na = _na_ctx_call(p, y_na, n_lat, n_ctx)

        qt, kr, vt = _df_prep_call(p, cos, sin)
        nw_col = df_norm_w[l].reshape(HEAD_DIM, 1)
        y_dft = _df_attn_call(qt, kr, vt, df_lambda[l], nw_col, None, lam_init, 0, n_lat, DF_QB, 0, t // DF_KC)
        if need_ctx:
            y_dft = _df_attn_call(qt, kr, vt, df_lambda[l], nw_col, y_dft, lam_init, n_lat, n_ctx, n_ctx,
                                  n_lat // DF_KC, 1)

        xmid = _outproj_call(xall, mod, dn_o, p, y_na, y_dft, hg_o,
                             jnp.tile(dn_norm_w[l], GROUP_HEADS).reshape(1, GROUP_W),
                             jnp.tile(hg_norm_w[l], GROUP_HEADS).reshape(1, GROUP_W),
                             consts["g2"], w_out[l].astype(BF16), nt_lat, n_tiles)
        out = _ffn_call(xmid, mod, norm2_w[l].reshape(1, d), w_up[l].astype(BF16), ffn_conv_w[l],
                        w_down[l].astype(BF16), final_norm_w.reshape(1, d), nt_lat, n_tiles, n_lat,
                        final=not need_ctx)
        xall = out
    return out
```

```python
import functools
import math

import numpy as np
import jax
import jax.numpy as jnp
from jax import lax
from jax.experimental import pallas as pl
from jax.experimental.pallas import tpu as pltpu

F32 = jnp.float32
BF16 = jnp.bfloat16

GRID_W = 64
HEAD_DIM = 64
GROUP_HEADS = 4
GROUP_W = GROUP_HEADS * HEAD_DIM
CHUNK = 64
NA_KH = 8
NA_KW = 16
DF_HEAD_DIM = HEAD_DIM // 2
ROPE_BASE = 10000.0
EPS = 1e-6
LOG2E = 1.4426950408889634

LANES = 128
SUBLANES = 8
VMEM_LIMIT_BYTES = 56 * 1024 * 1024

ROW_TILE = 256
NA_ROWS = 8
DF_QB = 512
DF_KC = 256
DF_UNROLL = 4
FF_SPLIT = 2

C_DN_Q, C_DN_K, C_DN_V, C_DN_GATE = 0, 1, 2, 3
C_NA_Q, C_NA_K, C_NA_V = 4, 5, 6
C_DF_Q, C_DF_K, C_DF_V = 7, 8, 9
C_HG_Q, C_HG_I, C_HG_FF, C_HG_FB, C_HG_GATE = 10, 11, 12, 13, 14
N_COL_BLOCKS = 15
P_WIDTH = N_COL_BLOCKS * GROUP_W + LANES
C_DN_BA_128 = N_COL_BLOCKS * GROUP_W // LANES


def _cparams(sem):
    return pltpu.CompilerParams(dimension_semantics=sem, vmem_limit_bytes=VMEM_LIMIT_BYTES)


def _mm(a, b):
    return jnp.dot(a, b, preferred_element_type=F32)


def _nt(a, b):
    return lax.dot_general(a, b, (((1,), (1,)), ((), ())), preferred_element_type=F32)


def _silu(x):
    return x * jax.nn.sigmoid(x)


def _split3(x):
    hi = x.astype(BF16)
    r = x - hi.astype(F32)
    mid = r.astype(BF16)
    lo = (r - mid.astype(F32)).astype(BF16)
    return hi, mid, lo


def _split2(x):
    hi = x.astype(BF16)
    lo = (x - hi.astype(F32)).astype(BF16)
    return hi, lo


def _head_masks(n_lanes=GROUP_W):
    lane = lax.broadcasted_iota(jnp.int32, (1, n_lanes), 1)
    return [(lane // HEAD_DIM == h).astype(F32) for h in range(GROUP_HEADS)]


def _mrs(x, hms):
    return jnp.concatenate([x * hm for hm in hms], axis=0)


def _rbs(z):
    n = z.shape[0] // GROUP_HEADS
    return z[0:n] + z[n:2 * n] + z[2 * n:3 * n] + z[3 * n:4 * n]


def _tile4(x):
    return jnp.concatenate([x, x, x, x], axis=0)


def _group_sum(x, g2_ref):
    hi, lo = _split2(x)
    return _mm(jnp.concatenate([hi, lo], axis=1), g2_ref[...])


def _rms_mod(x, nw, shift, scale):
    ms = jnp.mean(x * x, axis=-1, keepdims=True)
    return (x * lax.rsqrt(ms + EPS) * nw) * (1.0 + scale) + shift


def _ada_kernel(c_ref, w_ref, b_ref, o_ref):
    s = _silu(c_ref[...])
    o_ref[0] = _mm(s.astype(BF16), w_ref[0]) + b_ref[0]


def _ada_call(c_all, w_ada_bf, b_ada):
    depth, d, d6 = w_ada_bf.shape
    n_chunks = d6 // d
    return pl.pallas_call(
        _ada_kernel,
        out_shape=jax.ShapeDtypeStruct((depth, SUBLANES, d6), F32),
        grid=(depth, n_chunks),
        in_specs=[
            pl.BlockSpec((SUBLANES, d), lambda l, j: (0, 0)),
            pl.BlockSpec((1, d, d), lambda l, j: (l, 0, j)),
            pl.BlockSpec((1, 1, d), lambda l, j: (l, 0, j)),
        ],
        out_specs=pl.BlockSpec((1, SUBLANES, d), lambda l, j: (l, 0, j)),
        compiler_params=_cparams(("arbitrary", "arbitrary")),
        name="ada_ln",
    )(c_all, w_ada_bf, b_ada.reshape(depth, 1, d6))


def _inproj_kernel(x_ref, mod_ref, nw_ref, w_ref, o_ref):
    h = _rms_mod(x_ref[0], nw_ref[...], mod_ref[0, 0:1, :], mod_ref[0, 1:2, :])
    o_ref[0] = _mm(h.astype(BF16), w_ref[...])


def _inproj_call(xall, mod, nw, w_bf, nt_lat):
    b, t, d = xall.shape
    nt = t // ROW_TILE
    pw = w_bf.shape[1]
    return pl.pallas_call(
        _inproj_kernel,
        out_shape=jax.ShapeDtypeStruct((b, t, pw), F32),
        grid=(b, nt),
        in_specs=[
            pl.BlockSpec((1, ROW_TILE, d), lambda i, j: (i, j, 0)),
            pl.BlockSpec((1, 6, d), lambda i, j: (i * 2 + j // nt_lat, 0, 0)),
            pl.BlockSpec((1, d), lambda i, j: (0, 0)),
            pl.BlockSpec((d, pw), lambda i, j: (0, 0), pipeline_mode=pl.Buffered(1)),
        ],
        out_specs=pl.BlockSpec((1, ROW_TILE, pw), lambda i, j: (i, j, 0)),
        compiler_params=_cparams(("arbitrary", "arbitrary")),
        name="in_proj",
    )(xall, mod, nw, w_bf)


def _shift_rows(x, prev_row, next_row, gpos, seg_starts, seg_ends):
    n = x.shape[0]
    row = lax.broadcasted_iota(jnp.int32, (n, 1), 0)
    xm = jnp.where(row == 0, prev_row, pltpu.roll(x, 1, axis=0))
    xp = jnp.where(row == n - 1, next_row, pltpu.roll(x, n - 1, axis=0))
    at_start = functools.reduce(jnp.logical_or, [gpos == s for s in seg_starts])
    at_end = functools.reduce(jnp.logical_or, [gpos == e - 1 for e in seg_ends])
    return jnp.where(at_start, 0.0, xm), jnp.where(at_end, 0.0, xp)


def _dn_prep_kernel(x_ref, xp_ref, xn_ref, ba_ref, cw_ref, aneg_ref, dtb_ref, e3_ref, g2_ref,
                    qkv_ref, aux_ref, *, n_lat, n_all):
    tm = x_ref.shape[1]
    x = x_ref[0]
    gpos = pl.program_id(1) * tm + lax.broadcasted_iota(jnp.int32, (tm, 1), 0)
    xm, xp = _shift_rows(x, xp_ref[0, SUBLANES - 1:SUBLANES, :], xn_ref[0, 0:1, :], gpos,
                         (0, n_lat), (n_lat, n_all))
    cw = cw_ref[...]
    y = _silu(xm * cw[0:1] + x * cw[1:2] + xp * cw[2:3])
    q, k, v = y[:, 0:GROUP_W], y[:, GROUP_W:2 * GROUP_W], y[:, 2 * GROUP_W:3 * GROUP_W]
    qn = q * lax.rsqrt(_group_sum(q * q, g2_ref) + EPS) * (HEAD_DIM ** -0.5)
    kn = k * lax.rsqrt(_group_sum(k * k, g2_ref) + EPS)
    qkv_ref[0] = jnp.concatenate([qn, kn, v], axis=1)
    ba = ba_ref[0]
    lane = lax.broadcasted_iota(jnp.int32, ba.shape, 1)
    z = ba + dtb_ref[...]
    softplus = jnp.maximum(z, 0.0) + jnp.log(1.0 + jnp.exp(-jnp.abs(z)))
    gates = jnp.where(lane < 2 * GROUP_HEADS, jax.nn.sigmoid(ba), aneg_ref[...] * softplus)
    gates = jnp.where(lane < 4 * GROUP_HEADS, gates, 0.0)
    hi, mid, lo = _split3(gates)
    aux_ref[0] = _mm(jnp.concatenate([hi, mid, lo], axis=1), e3_ref[...])


def _halo_specs(width, col, n_rows_total, tm):
    per = tm // SUBLANES
    last = n_rows_total // SUBLANES - 1
    prev = pl.BlockSpec((1, SUBLANES, width), lambda i, j: (i, jnp.maximum(j * per - 1, 0), col))
    nxt = pl.BlockSpec((1, SUBLANES, width), lambda i, j: (i, jnp.minimum((j + 1) * per, last), col))
    return prev, nxt


def _dn_prep_call(p, conv_w, aneg, dtb, e3, g2, n_lat):
    b, t, _ = p.shape
    nt = t // ROW_TILE
    w3 = 3 * GROUP_W
    prev, nxt = _halo_specs(w3, 0, t, ROW_TILE)
    kern = functools.partial(_dn_prep_kernel, n_lat=n_lat, n_all=t)
    return pl.pallas_call(
        kern,
        out_shape=(jax.ShapeDtypeStruct((b, t, w3), F32), jax.ShapeDtypeStruct((b, t, 4 * GROUP_W), F32)),
        grid=(b, nt),
        in_specs=[
            pl.BlockSpec((1, ROW_TILE, w3), lambda i, j: (i, j, 0)),
            prev, nxt,
            pl.BlockSpec((1, ROW_TILE, LANES), lambda i, j: (i, j, C_DN_BA_128)),
            pl.BlockSpec((3, w3), lambda i, j: (0, 0)),
            pl.BlockSpec((1, LANES), lambda i, j: (0, 0)),
            pl.BlockSpec((1, LANES), lambda i, j: (0, 0)),
            pl.BlockSpec(e3.shape, lambda i, j: (0, 0)),
            pl.BlockSpec(g2.shape, lambda i, j: (0, 0)),
        ],
        out_specs=(pl.BlockSpec((1, ROW_TILE, w3), lambda i, j: (i, j, 0)),
                   pl.BlockSpec((1, ROW_TILE, 4 * GROUP_W), lambda i, j: (i, j, 0))),
        compiler_params=_cparams(("arbitrary", "arbitrary")),
        name="dn_prep",
    )(p, p, p, p, conv_w, aneg, dtb, e3, g2)


def _scan_tile(d, s, nt_lat):
    return jnp.where(s == 0, nt_lat, jnp.where(d == 0, s - 1, nt_lat - s))


def _unit_tri_inverse(mats, lv_ref):
    n = mats[0].shape[0]
    eye = (lax.broadcasted_iota(jnp.int32, (n, n), 0) == lax.broadcasted_iota(jnp.int32, (n, n), 1)).astype(F32)
    n1 = [a * lv_ref[0] for a in mats]
    n1b = [x.astype(BF16) for x in n1]
    n2 = [_mm(x, x) for x in n1b]
    n2b = [x.astype(BF16) for x in n2]
    n4 = [_mm(x, x) for x in n2b]
    inner = [_mm((eye + x2).astype(BF16), (eye + x4).astype(BF16)) for x2, x4 in zip(n2, n4)]
    ts = [_mm((eye - x1).astype(BF16), y.astype(BF16)) for x1, y in zip(n1, inner)]
    for lvl in (1, 2, 3):
        tb = [t.astype(BF16) for t in ts]
        xs = [_mm((a * lv_ref[lvl]).astype(BF16), t) for a, t in zip(mats, tb)]
        ts = [t - _mm(t16, x.astype(BF16)) for t, t16, x in zip(ts, tb, xs)]
    return ts


def _dn_scan_kernel(qkv_ref, beta_ref, la_ref, dm_ref, lv_ref, tri_ref, o_ref, s_ref):
    d = pl.program_id(1)

    @pl.when(pl.program_id(2) == 0)
    def _():
        s_ref[...] = jnp.zeros_like(s_ref)

    hms = _head_masks()
    incl = dm_ref[0, 0]
    strict = dm_ref[0, 1]
    n_chunks = qkv_ref.shape[1] // CHUNK
    rows, a_mats, attn, w_in, u_in, qg, kdec_t, glast = [], [], [], [], [], [], [], []
    for i in range(n_chunks):
        r0 = pl.multiple_of(jnp.where(d == 0, i, n_chunks - 1 - i) * CHUNK, CHUNK)
        rows.append(pl.ds(r0, CHUNK))
        q = qkv_ref[0, rows[i], 0:GROUP_W]
        k = qkv_ref[0, rows[i], GROUP_W:2 * GROUP_W]
        v = qkv_ref[0, rows[i], 2 * GROUP_W:3 * GROUP_W]
        beta = beta_ref[0, rows[i], :]
        hi, mid, lo = _split3(la_ref[0, rows[i], :])
        g = _mm(tri_ref[0], jnp.concatenate([hi, mid, lo], axis=0))
        gtot = jnp.where(d == 0, g[CHUNK - 1:CHUNK, :], g[0:1, :])
        eg = jnp.exp(g)
        rg = _tile4(g)
        decay = jnp.exp((rg - rg.T) * incl) * incl
        mk = _mrs(k, hms).astype(BF16)
        a_mats.append(strict * _tile4(beta) * _nt(mk, mk) * decay)
        attn.append((_nt(_mrs(q, hms).astype(BF16), mk) * decay).astype(BF16))
        w_in.append(_mrs(k * beta * eg, hms).astype(BF16))
        u_in.append(_mrs(v * beta, hms).astype(BF16))
        qg.append(_mrs(q * eg, hms).astype(BF16))
        kdec_t.append(_mrs(k * jnp.exp(gtot - g), hms).T.astype(BF16))
        glast.append(jnp.exp(gtot))
    tinv = [t.astype(BF16) for t in _unit_tri_inverse(a_mats, lv_ref)]
    w_bd = [_mm(t, x).astype(BF16) for t, x in zip(tinv, w_in)]
    u_bd = [_mm(t, x) for t, x in zip(tinv, u_in)]
    for i in range(n_chunks):
        sv = s_ref[...]
        svb = sv.astype(BF16)
        v_newb = (u_bd[i] - _mm(w_bd[i], svb)).astype(BF16)
        o_ref[0, 0, rows[i], :] = _rbs(_mm(qg[i], svb) + _mm(attn[i], v_newb))
        s_ref[...] = sv * glast[i] + _mm(kdec_t[i], v_newb)


def _dn_scan_call(qkv, aux, dmask, lvmask, tri3, nt_lat):
    b, t, w3 = qkv.shape
    ns = t // ROW_TILE
    return pl.pallas_call(
        _dn_scan_kernel,
        out_shape=jax.ShapeDtypeStruct((b, 2, t, GROUP_W), F32),
        grid=(b, 2, ns),
        in_specs=[
            pl.BlockSpec((1, ROW_TILE, w3), lambda i, d, s: (i, _scan_tile(d, s, nt_lat), 0)),
            pl.BlockSpec((1, ROW_TILE, GROUP_W), lambda i, d, s: (i, _scan_tile(d, s, nt_lat), d)),
            pl.BlockSpec((1, ROW_TILE, GROUP_W), lambda i, d, s: (i, _scan_tile(d, s, nt_lat), 2 + d)),
            pl.BlockSpec((1,) + dmask.shape[1:], lambda i, d, s: (d, 0, 0, 0)),
            pl.BlockSpec(lvmask.shape, lambda i, d, s: (0, 0, 0)),
            pl.BlockSpec((1,) + tri3.shape[1:], lambda i, d, s: (d, 0, 0)),
        ],
        out_specs=pl.BlockSpec((1, 1, ROW_TILE, GROUP_W), lambda i, d, s: (i, d, _scan_tile(d, s, nt_lat), 0)),
        scratch_shapes=[pltpu.VMEM((GROUP_W, GROUP_W), F32)],
        compiler_params=_cparams(("arbitrary", "arbitrary", "arbitrary")),
        name="dn_scan",
    )(qkv, aux, aux, dmask, lvmask, tri3)


HG_LEVELS = (1, 2, 4, 8, 16, 32)


def _hg_scan_kernel(q_ref, i_ref, f_ref, lb_ref, w3_ref, lm_ref, same_ref, o_ref, st_ref, qb_ref, ke_ref, oi_ref):
    d = pl.program_id(1)

    @pl.when(pl.program_id(2) == 0)
    def _():
        st_ref[...] = jnp.zeros_like(st_ref)

    hms = _head_masks()
    same = same_ref[...]
    lb = lb_ref[0]
    tm = q_ref.shape[1]
    n_chunks = tm // CHUNK
    nl = len(HG_LEVELS)
    tots = []
    for c in range(n_chunks):
        rows = slice(c * CHUNK, (c + 1) * CHUNK)
        q = q_ref[0, rows, :] * (HEAD_DIM ** -0.5)
        v = i_ref[0, rows, :]
        f = lb + (1.0 - lb) * jax.nn.sigmoid(f_ref[0, rows, :])
        kx = 1.0 - f
        hi, mid, lo = _split3(jnp.log(f))
        e = _mm(w3_ref[0], jnp.concatenate([hi, mid, lo], axis=0))
        cum = e[0:CHUNK]
        tot = e[CHUNK:2 * CHUNK]
        att = lm_ref[0, 0] * _nt(_mrs(q, hms).astype(BF16), kx.astype(BF16))
        for li in range(nl):
            ea = e[(2 + li) * CHUNK:(3 + li) * CHUNK]
            eb = e[(2 + nl + li) * CHUNK:(3 + nl + li) * CHUNK]
            att = att + lm_ref[0, 1 + li] * _nt(_mrs(q * jnp.exp(ea), hms).astype(BF16),
                                                 (kx * jnp.exp(eb)).astype(BF16))
        oi_ref[rows, :] = _rbs(_mm(att.astype(BF16), v.astype(BF16)) * same)
        qb_ref[rows, :] = q * jnp.exp(cum)
        ke_ref[rows, :] = kx * jnp.exp(tot - cum)
        tots.append(tot[0:1, :])
    vt = i_ref[0].T
    lane = lax.broadcasted_iota(jnp.int32, (1, tm), 1)
    keb = ke_ref[...].astype(BF16)
    for i in range(n_chunks):
        c = jnp.where(d == 0, i, n_chunks - 1 - i)
        rows = pl.ds(pl.multiple_of(c * CHUNK, CHUNK), CHUNK)
        st = st_ref[...]
        o_ref[0, 0, rows, :] = oi_ref[rows, :] + _nt(qb_ref[rows, :].astype(BF16), st.astype(BF16))
        tot = functools.reduce(lambda acc, j: jnp.where(c == j, tots[j], acc), range(1, n_chunks), tots[0])
        vt_c = (vt * (lane // CHUNK == c).astype(F32)).astype(BF16)
        st_ref[...] = st * jnp.exp(tot) + same * _mm(vt_c, keb)


def _hg_scan_call(p, lb, w3, lmask, same, nt_lat):
    b, t, _ = p.shape
    ns = t // ROW_TILE
    blk = lambda col: pl.BlockSpec((1, ROW_TILE, GROUP_W), lambda i, d, s: (i, _scan_tile(d, s, nt_lat), col))
    return pl.pallas_call(
        _hg_scan_kernel,
        out_shape=jax.ShapeDtypeStruct((b, 2, t, GROUP_W), F32),
        grid=(b, 2, ns),
        in_specs=[
            blk(C_HG_Q), blk(C_HG_I),
            pl.BlockSpec((1, ROW_TILE, GROUP_W), lambda i, d, s: (i, _scan_tile(d, s, nt_lat), C_HG_FF + d)),
            pl.BlockSpec((1, 1, GROUP_W), lambda i, d, s: (d, 0, 0)),
            pl.BlockSpec((1,) + w3.shape[1:], lambda i, d, s: (d, 0, 0)),
            pl.BlockSpec((1,) + lmask.shape[1:], lambda i, d, s: (d, 0, 0, 0)),
            pl.BlockSpec(same.shape, lambda i, d, s: (0, 0)),
        ],
        out_specs=pl.BlockSpec((1, 1, ROW_TILE, GROUP_W), lambda i, d, s: (i, d, _scan_tile(d, s, nt_lat), 0)),
        scratch_shapes=[pltpu.VMEM((GROUP_W, GROUP_W), F32), pltpu.VMEM((ROW_TILE, GROUP_W), F32),
                        pltpu.VMEM((ROW_TILE, GROUP_W), F32), pltpu.VMEM((ROW_TILE, GROUP_W), F32)],
        compiler_params=_cparams(("arbitrary", "arbitrary", "arbitrary")),
        name="hg_scan",
    )(p, p, p, lb, w3, lmask, same)


def _na_kernel(q_ref, kp_ref, kc_ref, kn_ref, vp_ref, vc_ref, vn_ref, kx_ref, vx_ref, bias_ref, same_ref,
               o_ref, kw_ref, vw_ref, *, n_rows):
    i = pl.program_id(1)
    blk = q_ref.shape[1]
    for j, (kr, vr) in enumerate(((kp_ref, vp_ref), (kc_ref, vc_ref), (kn_ref, vn_ref))):
        kw_ref[j * blk:(j + 1) * blk, :] = kr[0].astype(BF16)
        vw_ref[j * blk:(j + 1) * blk, :] = vr[0].astype(BF16)
    kx = kx_ref[0].astype(BF16)
    vx = vx_ref[0].astype(BF16)
    hms = _head_masks()
    same = same_ref[...]
    win = NA_KH * GRID_W

    def row_body(rr, carry):
        r = i * NA_ROWS + rr
        rs = jnp.clip(r - NA_KH // 2, 0, n_rows - NA_KH)
        start = pl.multiple_of((rs - i * NA_ROWS + NA_ROWS) * GRID_W, GRID_W)
        rows = pl.ds(pl.multiple_of(rr * GRID_W, GRID_W), GRID_W)
        mq = _mrs(q_ref[0, rows, :] * (HEAD_DIM ** -0.5), hms).astype(BF16)
        s_loc = _nt(mq, kw_ref[pl.ds(start, win), :]) + bias_ref[rs - r + NA_KH - 1]
        s_ctx = _nt(mq, kx)
        m = jnp.maximum(jnp.max(s_loc, axis=-1, keepdims=True), jnp.max(s_ctx, axis=-1, keepdims=True))
        p_loc = jnp.exp(s_loc - m)
        p_ctx = jnp.exp(s_ctx - m)
        den = jnp.sum(p_loc, axis=-1, keepdims=True) + jnp.sum(p_ctx, axis=-1, keepdims=True)
        o = (_mm(p_loc.astype(BF16), vw_ref[pl.ds(start, win), :]) + _mm(p_ctx.astype(BF16), vx)) / den
        o_ref[0, rows, :] = _rbs(o * same)
        return carry

    lax.fori_loop(0, NA_ROWS, row_body, 0, unroll=2)


def _na_call(p, bias, same, n_lat, n_ctx):
    b, t, _ = p.shape
    blk = NA_ROWS * GRID_W
    nb = n_lat // blk
    spec = lambda col, f: pl.BlockSpec((1, blk, GROUP_W), lambda i, j: (i, f(j), col))
    prev = lambda j: jnp.maximum(j - 1, 0)
    cur = lambda j: j
    nxt = lambda j: jnp.minimum(j + 1, nb - 1)
    ctx_spec = lambda col: pl.BlockSpec((1, n_ctx, GROUP_W), lambda i, j: (i, n_lat // n_ctx, col))
    kern = functools.partial(_na_kernel, n_rows=n_lat // GRID_W)
    return pl.pallas_call(
        kern,
        out_shape=jax.ShapeDtypeStruct((b, t, GROUP_W), F32),
        grid=(b, nb),
        in_specs=[spec(C_NA_Q, cur),
                  spec(C_NA_K, prev), spec(C_NA_K, cur), spec(C_NA_K, nxt),
                  spec(C_NA_V, prev), spec(C_NA_V, cur), spec(C_NA_V, nxt),
                  ctx_spec(C_NA_K), ctx_spec(C_NA_V),
                  pl.BlockSpec(bias.shape, lambda i, j: (0, 0, 0), pipeline_mode=pl.Buffered(1)),
                  pl.BlockSpec(same.shape, lambda i, j: (0, 0))],
        out_specs=pl.BlockSpec((1, blk, GROUP_W), lambda i, j: (i, j, 0)),
        scratch_shapes=[pltpu.VMEM((3 * blk, GROUP_W), BF16), pltpu.VMEM((3 * blk, GROUP_W), BF16)],
        compiler_params=_cparams(("arbitrary", "arbitrary")),
        name="na_attn",
    )(p, p, p, p, p, p, p, p, p, bias, same)


def _na_ctx_kernel(q_ref, k_ref, v_ref, y_in_ref, o_ref):
    del y_in_ref
    hms = _head_masks()
    n = q_ref.shape[1]
    mq = _mrs(q_ref[0] * (HEAD_DIM ** -0.5), hms).astype(BF16)
    s = _nt(mq, k_ref[0].astype(BF16))
    p = jnp.exp(s - jnp.max(s, axis=-1, keepdims=True))
    o = _mm(p.astype(BF16), v_ref[0].astype(BF16)) / jnp.sum(p, axis=-1, keepdims=True)
    o_ref[0] = _rbs(o * jnp.concatenate([jnp.broadcast_to(hm, (n, GROUP_W)) for hm in hms], axis=0))


def _na_ctx_call(p, y_na, n_lat, n_ctx):
    b, t, _ = p.shape
    spec = lambda col: pl.BlockSpec((1, n_ctx, GROUP_W), lambda i: (i, n_lat // n_ctx, col))
    return pl.pallas_call(
        _na_ctx_kernel,
        out_shape=jax.ShapeDtypeStruct(y_na.shape, y_na.dtype),
        grid=(b,),
        in_specs=[spec(C_NA_Q), spec(C_NA_K), spec(C_NA_V),
                  pl.BlockSpec(memory_space=pl.ANY)],
        out_specs=pl.BlockSpec((1, n_ctx, GROUP_W), lambda i: (i, n_lat // n_ctx, 0)),
        input_output_aliases={3: 0},
        compiler_params=_cparams(("arbitrary",)),
        name="na_ctx",
    )(p, p, p, y_na)


def _df_prep_kernel(q_ref, k_ref, v_ref, cos_ref, sin_ref, qt_ref, kr_ref, vt_ref):
    cos = cos_ref[...]
    sin = sin_ref[...]
    lane = lax.broadcasted_iota(jnp.int32, cos.shape, 1)
    first = (lane % 16) < 8

    def rope(x):
        w = x.shape[1]
        partner = jnp.where(first, pltpu.roll(x, w - 8, axis=1), pltpu.roll(x, 8, axis=1))
        return x * cos + partner * sin

    tm = q_ref.shape[1]
    qt = (rope(q_ref[0]) * (DF_HEAD_DIM ** -0.5 * LOG2E)).T
    row = lax.broadcasted_iota(jnp.int32, (GROUP_W, 1), 0)
    for s in range(2 * GROUP_HEADS):
        keep = (row // DF_HEAD_DIM == s).astype(F32)
        qt_ref[0, s] = (qt * keep).astype(BF16)
    kr_ref[0, 0] = rope(k_ref[0]).astype(BF16)
    vt = v_ref[0].T
    ext = (lax.broadcasted_iota(jnp.int32, (2 * SUBLANES, tm), 0) == 0).astype(F32)
    for h in range(GROUP_HEADS):
        vt_ref[0, h, 0] = jnp.concatenate([vt[h * HEAD_DIM:(h + 1) * HEAD_DIM], ext], axis=0).astype(BF16)


def _df_prep_call(p, cos, sin):
    b, t, _ = p.shape
    nt = t // DF_KC
    vrows = HEAD_DIM + 2 * SUBLANES
    blk = lambda col: pl.BlockSpec((1, DF_KC, GROUP_W), lambda i, j: (i, j, col))
    return pl.pallas_call(
        _df_prep_kernel,
        out_shape=(jax.ShapeDtypeStruct((b, 2 * GROUP_HEADS, GROUP_W, t), BF16),
                   jax.ShapeDtypeStruct((b, nt, DF_KC, GROUP_W), BF16),
                   jax.ShapeDtypeStruct((b, GROUP_HEADS, nt, vrows, DF_KC), BF16)),
        grid=(b, nt),
        in_specs=[blk(C_DF_Q), blk(C_DF_K), blk(C_DF_V),
                  pl.BlockSpec((DF_KC, GROUP_W), lambda i, j: (j, 0)),
                  pl.BlockSpec((DF_KC, GROUP_W), lambda i, j: (j, 0))],
        out_specs=(pl.BlockSpec((1, 2 * GROUP_HEADS, GROUP_W, DF_KC), lambda i, j: (i, 0, 0, j)),
                   pl.BlockSpec((1, 1, DF_KC, GROUP_W), lambda i, j: (i, j, 0, 0)),
                   pl.BlockSpec((1, GROUP_HEADS, 1, vrows, DF_KC), lambda i, j: (i, 0, j, 0, 0))),
        compiler_params=_cparams(("arbitrary", "arbitrary")),
        name="df_prep",
    )(p, p, p, cos, sin)


def _df_attn_kernel(qt_ref, k_ref, vt_ref, lam_ref, nw_ref, y_in_ref, o_ref,
                    m_ref, acc_ref, s_ref, cm_ref, p_ref, al_ref, *, lam_init):
    del y_in_ref
    nk = k_ref.shape[1]
    m_ref[...] = jnp.full(m_ref.shape, -jnp.inf, F32)
    acc_ref[...] = jnp.zeros_like(acc_ref)

    def scores(c, slot):
        kc = k_ref[0, c]
        for j in range(2):
            s = _mm(kc, qt_ref[0, j])
            s_ref[slot, j] = s
            cm_ref[slot, j] = jnp.max(s, axis=0, keepdims=True)

    def probs(slot):
        for j in range(2):
            m_old = m_ref[j]
            m_new = jnp.maximum(m_old, cm_ref[slot, j])
            p_ref[slot, j] = jnp.exp2(s_ref[slot, j] - m_new).astype(BF16)
            al_ref[slot, j] = jnp.exp2(m_old - m_new)
            m_ref[j] = m_new

    def accum(c, slot):
        vt = vt_ref[0, 0, c]
        for j in range(2):
            acc_ref[j] = acc_ref[j] * al_ref[slot, j] + _mm(vt, p_ref[slot, j])

    def iteration(i, slot, do_scores=True, do_probs=True, do_accum=True):
        if do_accum:
            accum(i - 2, slot)
        if do_probs:
            probs(1 - slot)
        if do_scores:
            scores(i, slot)

    for i in range(min(2, nk + 2)):
        iteration(i, i % 2, i < nk, 1 <= i <= nk, False)
    n_steady = max(nk - 2, 0)

    def group(k, carry):
        for u in range(DF_UNROLL):
            iteration(2 + DF_UNROLL * k + u, u % 2)
        return carry

    n_groups = n_steady // DF_UNROLL
    lax.fori_loop(0, n_groups, group, 0)
    for i in range(2 + DF_UNROLL * n_groups, nk + 2):
        iteration(i, i % 2, i < nk, i <= nk, True)
    lp = lam_ref[...]
    lam = (jnp.exp(jnp.sum(lp[0:1] * lp[1:2], keepdims=True)) - jnp.exp(jnp.sum(lp[2:3] * lp[3:4], keepdims=True))
           + lam_init)
    a0 = acc_ref[0]
    a1 = acc_ref[1]
    o = a0[0:HEAD_DIM] / a0[HEAD_DIM:HEAD_DIM + 1] - lam * (a1[0:HEAD_DIM] / a1[HEAD_DIM:HEAD_DIM + 1])
    ms = jnp.mean(o * o, axis=0, keepdims=True)
    o_ref[0] = o * lax.rsqrt(ms + EPS) * nw_ref[...] * (1.0 - lam_init)


def _df_attn_call(qt, kr, vt, lam_p, nw_col, y_t, lam_init, q_off, n_q, qb, k_off, nk):
    b = qt.shape[0]
    vrows = vt.shape[3]
    kern = functools.partial(_df_attn_kernel, lam_init=lam_init)
    out_shape = jax.ShapeDtypeStruct((b, GROUP_W, qt.shape[3]), F32)
    in_specs = [
        pl.BlockSpec((1, 2, GROUP_W, qb), lambda i, h, q: (i, h, 0, q_off // qb + q)),
        pl.BlockSpec((1, nk, DF_KC, GROUP_W), lambda i, h, q: (i, k_off // nk, 0, 0), pipeline_mode=pl.Buffered(1)),
        pl.BlockSpec((1, 1, nk, vrows, DF_KC), lambda i, h, q: (i, h, k_off // nk, 0, 0)),
        pl.BlockSpec(lam_p.shape, lambda i, h, q: (0, 0)),
        pl.BlockSpec(nw_col.shape, lambda i, h, q: (0, 0)),
    ]
    args = [qt, kr, vt, lam_p, nw_col]
    aliases = {}
    if y_t is None:
        kern_fn = lambda a, b_, c, d, e, o, *scratch: kern(a, b_, c, d, e, None, o, *scratch)
    else:
        kern_fn = kern
        in_specs.append(pl.BlockSpec(memory_space=pl.ANY))
        args.append(y_t)
        aliases = {5: 0}
    return pl.pallas_call(
        kern_fn,
        out_shape=out_shape,
        grid=(b, GROUP_HEADS, n_q // qb),
        in_specs=in_specs,
        out_specs=pl.BlockSpec((1, HEAD_DIM, qb), lambda i, h, q: (i, h, q_off // qb + q)),
        scratch_shapes=[pltpu.VMEM((2, 1, qb), F32), pltpu.VMEM((2, vrows, qb), F32),
                        pltpu.VMEM((2, 2, DF_KC, qb), F32), pltpu.VMEM((2, 2, 1, qb), F32),
                        pltpu.VMEM((2, 2, DF_KC, qb), BF16), pltpu.VMEM((2, 2, 1, qb), F32)],
        input_output_aliases=aliases,
        compiler_params=_cparams(("arbitrary", "arbitrary", "arbitrary")),
        name="df_attn",
    )(*args)


def _outproj_kernel(x_ref, mod_ref, dno_f, dno_b, dng_ref, na_ref, dft_ref, hgo_f, hgo_b, hgg_ref,
                    dnw_ref, hgw_ref, g2_ref, w_ref, o_ref):
    inv = 1.0 / HEAD_DIM

    def gated_norm(o, nw, gate):
        ms = _group_sum(o * o, g2_ref) * inv
        return o * lax.rsqrt(ms + EPS) * nw * _silu(gate)

    y_dn = gated_norm(dno_f[0, 0] + dno_b[0, 0], dnw_ref[...], dng_ref[0])
    y_hg = gated_norm(hgo_f[0, 0] + hgo_b[0, 0], hgw_ref[...], hgg_ref[0])
    y_df = dft_ref[0].T
    acc = _mm(y_dn.astype(BF16), w_ref[0:GROUP_W, :])
    acc += _mm(na_ref[0].astype(BF16), w_ref[GROUP_W:2 * GROUP_W, :])
    acc += _mm(y_df.astype(BF16), w_ref[2 * GROUP_W:3 * GROUP_W, :])
    acc += _mm(y_hg.astype(BF16), w_ref[3 * GROUP_W:4 * GROUP_W, :])
    o_ref[0] = x_ref[0] + mod_ref[0, 2:3, :] * acc


def _outproj_call(xall, mod, dn_o, p, y_na, y_dft, hg_o, dnw, hgw, g2, w_bf, nt_lat, n_tiles):
    b, t, d = xall.shape
    row = lambda w, col=0: pl.BlockSpec((1, ROW_TILE, w), lambda i, j: (i, j, col))
    dirspec = lambda dd: pl.BlockSpec((1, 1, ROW_TILE, GROUP_W), lambda i, j: (i, dd, j, 0))
    const = lambda a: pl.BlockSpec(a.shape, lambda i, j: (0,) * a.ndim)
    return pl.pallas_call(
        _outproj_kernel,
        out_shape=jax.ShapeDtypeStruct((b, t, d), F32),
        grid=(b, n_tiles),
        in_specs=[row(d),
                  pl.BlockSpec((1, 6, d), lambda i, j: (i * 2 + j // nt_lat, 0, 0)),
                  dirspec(0), dirspec(1), row(GROUP_W, C_DN_GATE),
                  row(GROUP_W),
                  pl.BlockSpec((1, GROUP_W, ROW_TILE), lambda i, j: (i, 0, j)),
                  dirspec(0), dirspec(1), row(GROUP_W, C_HG_GATE),
                  const(dnw), const(hgw), const(g2),
                  pl.BlockSpec(w_bf.shape, lambda i, j: (0, 0), pipeline_mode=pl.Buffered(1))],
        out_specs=row(d),
        compiler_params=_cparams(("arbitrary", "arbitrary")),
        name="out_proj",
    )(xall, mod, dn_o, dn_o, p, y_na, y_dft, hg_o, hg_o, p, dnw, hgw, g2, w_bf)


def _ffn_kernel(x_ref, xp_ref, xn_ref, mod_ref, nw_ref, wup_ref, cw_ref, wdn_ref, fw_ref, o_ref,
                *, n_lat, n_all, final):
    tm = x_ref.shape[1]
    d_ff = wdn_ref.shape[0]
    fc = d_ff // FF_SPLIT
    x = x_ref[0]
    xe = jnp.concatenate([x, xp_ref[0], xn_ref[0]], axis=0)
    hb = _rms_mod(xe, nw_ref[...], mod_ref[0, 3:4, :], mod_ref[0, 4:5, :]).astype(BF16)
    gpos = pl.program_id(1) * tm + lax.broadcasted_iota(jnp.int32, (tm, 1), 0)
    acc = jnp.zeros((tm, x.shape[1]), F32)
    for f in range(FF_SPLIT):
        g_all = _mm(hb, wup_ref[:, f * fc:(f + 1) * fc])
        val = _mm(hb[0:tm], wup_ref[:, d_ff + f * fc:d_ff + (f + 1) * fc])
        g = g_all[0:tm]
        gm, gp = _shift_rows(g, g_all[tm + SUBLANES - 1:tm + SUBLANES], g_all[tm + SUBLANES:tm + SUBLANES + 1],
                             gpos, (0, n_lat), (n_lat, n_all))
        cw = cw_ref[:, f * fc:(f + 1) * fc]
        a = _silu(gm * cw[0:1] + g * cw[1:2] + gp * cw[2:3]) * val
        acc += _mm(a.astype(BF16), wdn_ref[f * fc:(f + 1) * fc, :])
    y = x + mod_ref[0, 5:6, :] * acc
    if final:
        ms = jnp.mean(y * y, axis=-1, keepdims=True)
        y = y * lax.rsqrt(ms + EPS) * fw_ref[...]
    o_ref[0] = y


def _ffn_call(xmid, mod, nw, wup_bf, cw, wdn_bf, fw, nt_lat, n_tiles, n_lat, final):
    b, t, d = xmid.shape
    out_rows = n_tiles * ROW_TILE
    prev, nxt = _halo_specs(d, 0, out_rows, ROW_TILE)
    kern = functools.partial(_ffn_kernel, n_lat=n_lat, n_all=t, final=final)
    const = lambda a, **kw: pl.BlockSpec(a.shape, lambda i, j: (0,) * a.ndim, **kw)
    return pl.pallas_call(
        kern,
        out_shape=jax.ShapeDtypeStruct((b, out_rows, d), F32),
        grid=(b, n_tiles),
        in_specs=[pl.BlockSpec((1, ROW_TILE, d), lambda i, j: (i, j, 0)), prev, nxt,
                  pl.BlockSpec((1, 6, d), lambda i, j: (i * 2 + j // nt_lat, 0, 0)),
                  const(nw), const(wup_bf, pipeline_mode=pl.Buffered(1)), const(cw),
                  const(wdn_bf, pipeline_mode=pl.Buffered(1)), const(fw)],
        out_specs=pl.BlockSpec((1, ROW_TILE, d), lambda i, j: (i, j, 0)),
        compiler_params=_cparams(("arbitrary", "arbitrary")),
        name="conv_ffn",
    )(xmid, xmid, xmid, mod, nw, wup_bf, cw, wdn_bf, fw)


def _scan_constants():
    c = CHUNK
    n = GROUP_W
    t = np.arange(c)
    tri_f = (t[None, :] <= t[:, None]).astype(np.float32)
    tri = np.stack([tri_f, tri_f[::-1, ::-1]])
    tri3 = np.concatenate([tri] * 3, axis=2)

    idx = np.arange(n)
    hh, tt = idx // c, idx % c
    same = hh[:, None] == hh[None, :]
    ti, si = tt[:, None], tt[None, :]
    dmask = np.stack([np.stack([same & (si <= ti), same & (si < ti)]),
                      np.stack([same & (si >= ti), same & (si > ti)])]).astype(np.float32)
    lv = [same & (ti // 8 == si // 8)]
    for m in (8, 16, 32):
        lv.append(same & (ti // (2 * m) == si // (2 * m)) & (ti // m != si // m))
    lvmask = np.stack(lv).astype(np.float32)

    u = t[None, :]
    ws = [tri_f, np.ones((c, c), np.float32)]
    for m in HG_LEVELS:
        r = (t // m) * m
        ws.append(((u > r[:, None]) & (u <= t[:, None])).astype(np.float32))
    for m in HG_LEVELS:
        r2 = np.minimum((t // m + 1) * m, c - 1)
        ws.append(((u > t[:, None]) & (u <= r2[:, None])).astype(np.float32))
    w_f = np.concatenate(ws, axis=0)
    w_b = np.concatenate([w[::-1, ::-1] for w in ws], axis=0)
    w3 = np.stack([np.concatenate([w_f] * 3, axis=1), np.concatenate([w_b] * 3, axis=1)])

    lms = [np.eye(c, dtype=bool)]
    for m in HG_LEVELS:
        lms.append(((t[:, None] // m) % 2 == 1) & (t[None, :] // m == t[:, None] // m - 1))
    lm_f = np.stack(lms)
    lm_b = lm_f[:, ::-1, ::-1]
    lmask = np.stack([np.tile(lm_f, (1, GROUP_HEADS, 1)), np.tile(lm_b, (1, GROUP_HEADS, 1))]).astype(np.float32)

    g = same.astype(np.float32)
    g2 = np.concatenate([g, g], axis=0)
    e = np.zeros((LANES, 4 * GROUP_W), np.float32)
    for j in range(4 * GROUP_HEADS):
        e[j, j * HEAD_DIM:(j + 1) * HEAD_DIM] = 1.0
    e3 = np.concatenate([e, e, e], axis=0)
    return dict(tri3=jnp.asarray(tri3, BF16), dmask=jnp.asarray(dmask), lvmask=jnp.asarray(lvmask),
                w3=jnp.asarray(w3, BF16), lmask=jnp.asarray(lmask), same=jnp.asarray(g),
                g2=jnp.asarray(g2, BF16), e3=jnp.asarray(e3, BF16))


def _na_bias_table(rpb):
    cols = np.arange(GRID_W)
    c_start = np.clip(cols - NA_KW // 2, 0, GRID_W - NA_KW)
    kc = np.arange(GRID_W)
    valid = (kc[None, :] >= c_start[:, None]) & (kc[None, :] < c_start[:, None] + NA_KW)
    dc = np.clip(kc[None, :] - cols[:, None] + NA_KW - 1, 0, 2 * NA_KW - 2)
    variants = []
    for var in range(NA_KH):
        dr = var - (NA_KH - 1) + np.arange(NA_KH) + (NA_KH - 1)
        t = rpb[:, dr][:, :, dc]
        t = jnp.where(valid[None, None], t, -jnp.inf)
        variants.append(t.transpose(0, 2, 1, 3).reshape(GROUP_HEADS * GRID_W, NA_KH * GRID_W))
    return jnp.stack(variants).astype(F32)


def _rope_tables(n_lat, n_ctx):
    pos = jnp.arange(n_lat)
    row = (pos // GRID_W).astype(F32)
    col = (pos % GRID_W).astype(F32)
    half = DF_HEAD_DIM // 2
    inv = 1.0 / (ROPE_BASE ** (jnp.arange(0, half, 2, dtype=F32) / half))
    lane = np.arange(GROUP_W)
    use_row = jnp.asarray((lane % DF_HEAD_DIM) < half)
    ang = jnp.where(use_row[None, :], row[:, None], col[:, None]) * inv[lane % (half // 2)][None, :]
    sign = jnp.asarray(np.where((lane % half) < half // 2, -1.0, 1.0), F32)
    cos = jnp.concatenate([jnp.cos(ang), jnp.ones((n_ctx, GROUP_W), F32)], axis=0)
    sin = jnp.concatenate([jnp.sin(ang) * sign[None, :], jnp.zeros((n_ctx, GROUP_W), F32)], axis=0)
    return cos, sin


def _reorder_w_in(w_in_l):
    d = w_in_l.shape[0]
    a = 4 * GROUP_W
    ba = w_in_l[:, a:a + 4 * GROUP_HEADS]
    rest = w_in_l[:, a + 4 * GROUP_HEADS:]
    pad = jnp.zeros((d, LANES - 4 * GROUP_HEADS), w_in_l.dtype)
    return jnp.concatenate([w_in_l[:, :a], rest, ba, pad], axis=1)


def kernel(x, c, ctx, c_ctx, w_ada, b_ada, norm1_w, norm2_w, w_in, dn_conv_w, dn_a_log, dn_dt_bias, dn_norm_w,
           na_rpb, df_lambda, df_norm_w, hg_lb_raw, hg_norm_w, w_out, w_up, ffn_conv_w, w_down, final_norm_w):
    b, n_lat, d = x.shape
    n_ctx = ctx.shape[1]
    depth = w_ada.shape[0]
    t = n_lat + n_ctx
    assert n_ctx == ROW_TILE == DF_KC and n_lat % (NA_ROWS * GRID_W) == 0 and n_lat % DF_QB == 0
    assert b <= SUBLANES - 1
    nt_lat = n_lat // ROW_TILE
    nt_all = t // ROW_TILE

    consts = _scan_constants()
    cos, sin = _rope_tables(n_lat, n_ctx)

    c_all = jnp.zeros((SUBLANES, d), F32).at[:b].set(c).at[b].set(c_ctx)
    ada = _ada_call(c_all, w_ada.astype(BF16), b_ada).reshape(depth, SUBLANES, 6, d)

    lb_all = jnp.cumsum(jax.nn.softmax(hg_lb_raw.astype(F32), axis=0), axis=0)
    lb_all = (lb_all - lb_all[0]).reshape(depth, 2, 1, GROUP_W)

    xall = jnp.concatenate([x, ctx], axis=1)
    out = None
    for l in range(depth):
        need_ctx = l < depth - 1
        n_tiles = nt_all if need_ctx else nt_lat
        lam_init = 0.8 - 0.6 * math.exp(-0.3 * l)
        mod = jnp.stack([ada[l, :b], jnp.broadcast_to(ada[l, b], (b, 6, d))], axis=1).reshape(2 * b, 6, d)

        p = _inproj_call(xall, mod, norm1_w[l].reshape(1, d), _reorder_w_in(w_in[l]).astype(BF16), nt_lat)

        lane = np.arange(LANES)
        gate_idx = np.clip(lane - 2 * GROUP_HEADS, 0, 2 * GROUP_HEADS - 1)
        is_decay = jnp.asarray((lane >= 2 * GROUP_HEADS) & (lane < 4 * GROUP_HEADS))
        aneg = jnp.where(is_decay, -jnp.exp(dn_a_log[l].astype(F32).reshape(-1)[gate_idx]), 0.0).reshape(1, LANES)
        dtb = jnp.where(is_decay, dn_dt_bias[l].astype(F32).reshape(-1)[gate_idx], 0.0).reshape(1, LANES)
        dn_qkv, dn_aux = _dn_prep_call(p, dn_conv_w[l], aneg, dtb, consts["e3"], consts["g2"], n_lat)
        dn_o = _dn_scan_call(dn_qkv, dn_aux, consts["dmask"], consts["lvmask"], consts["tri3"], nt_lat)

        hg_o = _hg_scan_call(p, lb_all[l], consts["w3"], consts["lmask"], consts["same"], nt_lat)

        y_na = _na_call(p, _na_bias_table(na_rpb[l]), consts["same"], n_lat, n_ctx)
        if need_ctx:
            y_na = _na_ctx_call(p, y_na, n_lat, n_ctx)

        qt, kr, vt = _df_prep_call(p, cos, sin)
        nw_col = df_norm_w[l].reshape(HEAD_DIM, 1)
        y_dft = _df_attn_call(qt, kr, vt, df_lambda[l], nw_col, None, lam_init, 0, n_lat, DF_QB, 0, t // DF_KC)
        if need_ctx:
            y_dft = _df_attn_call(qt, kr, vt, df_lambda[l], nw_col, y_dft, lam_init, n_lat, n_ctx, n_ctx,
                                  n_lat // DF_KC, 1)

        xmid = _outproj_call(xall, mod, dn_o, p, y_na, y_dft, hg_o,
                             jnp.tile(dn_norm_w[l], GROUP_HEADS).reshape(1, GROUP_W),
                             jnp.tile(hg_norm_w[l], GROUP_HEADS).reshape(1, GROUP_W),
                             consts["g2"], w_out[l].astype(BF16), nt_lat, n_tiles)
        out = _ffn_call(xmid, mod, norm2_w[l].reshape(1, d), w_up[l].astype(BF16), ffn_conv_w[l],
                        w_down[l].astype(BF16), final_norm_w.reshape(1, d), nt_lat, n_tiles, n_lat,
                        final=not need_ctx)
        xall = out
    return out
```

```python
import functools
import math

import numpy as np
import jax
import jax.numpy as jnp
from jax import lax
from jax.experimental import pallas as pl
from jax.experimental.pallas import tpu as pltpu

F32 = jnp.float32
BF16 = jnp.bfloat16

GRID_W = 64
HEAD_DIM = 64
GROUP_HEADS = 4
GROUP_W = GROUP_HEADS * HEAD_DIM
CHUNK = 64
NA_KH = 8
NA_KW = 16
DF_HEAD_DIM = HEAD_DIM // 2
ROPE_BASE = 10000.0
EPS = 1e-6
LOG2E = 1.4426950408889634

LANES = 128
SUBLANES = 8
VMEM_LIMIT_BYTES = 56 * 1024 * 1024

ROW_TILE = 256
NA_ROWS = 8
DF_QB = 512
DF_KC = 256
DF_UNROLL = 8
FF_SPLIT = 2

C_DN_Q, C_DN_K, C_DN_V, C_DN_GATE = 0, 1, 2, 3
C_NA_Q, C_NA_K, C_NA_V = 4, 5, 6
C_DF_Q, C_DF_K, C_DF_V = 7, 8, 9
C_HG_Q, C_HG_I, C_HG_FF, C_HG_FB, C_HG_GATE = 10, 11, 12, 13, 14
N_COL_BLOCKS = 15
P_WIDTH = N_COL_BLOCKS * GROUP_W + LANES
C_DN_BA_128 = N_COL_BLOCKS * GROUP_W // LANES


def _cparams(sem):
    return pltpu.CompilerParams(dimension_semantics=sem, vmem_limit_bytes=VMEM_LIMIT_BYTES)


def _mm(a, b):
    return jnp.dot(a, b, preferred_element_type=F32)


def _nt(a, b):
    return lax.dot_general(a, b, (((1,), (1,)), ((), ())), preferred_element_type=F32)


def _silu(x):
    return x * jax.nn.sigmoid(x)


def _split3(x):
    hi = x.astype(BF16)
    r = x - hi.astype(F32)
    mid = r.astype(BF16)
    lo = (r - mid.astype(F32)).astype(BF16)
    return hi, mid, lo


def _split2(x):
    hi = x.astype(BF16)
    lo = (x - hi.astype(F32)).astype(BF16)
    return hi, lo


def _head_masks(n_lanes=GROUP_W):
    lane = lax.broadcasted_iota(jnp.int32, (1, n_lanes), 1)
    return [(lane // HEAD_DIM == h).astype(F32) for h in range(GROUP_HEADS)]


def _mrs(x, hms):
    return jnp.concatenate([x * hm for hm in hms], axis=0)


def _rbs(z):
    n = z.shape[0] // GROUP_HEADS
    return z[0:n] + z[n:2 * n] + z[2 * n:3 * n] + z[3 * n:4 * n]


def _tile4(x):
    return jnp.concatenate([x, x, x, x], axis=0)


def _group_sum(x, g2_ref):
    hi, lo = _split2(x)
    return _mm(jnp.concatenate([hi, lo], axis=1), g2_ref[...])


def _rms_mod(x, nw, shift, scale):
    ms = jnp.mean(x * x, axis=-1, keepdims=True)
    return (x * lax.rsqrt(ms + EPS) * nw) * (1.0 + scale) + shift


def _ada_kernel(c_ref, w_ref, b_ref, o_ref):
    s = _silu(c_ref[...])
    o_ref[0] = _mm(s.astype(BF16), w_ref[0]) + b_ref[0]


def _ada_call(c_all, w_ada_bf, b_ada):
    depth, d, d6 = w_ada_bf.shape
    n_chunks = d6 // d
    return pl.pallas_call(
        _ada_kernel,
        out_shape=jax.ShapeDtypeStruct((depth, SUBLANES, d6), F32),
        grid=(depth, n_chunks),
        in_specs=[
            pl.BlockSpec((SUBLANES, d), lambda l, j: (0, 0)),
            pl.BlockSpec((1, d, d), lambda l, j: (l, 0, j)),
            pl.BlockSpec((1, 1, d), lambda l, j: (l, 0, j)),
        ],
        out_specs=pl.BlockSpec((1, SUBLANES, d), lambda l, j: (l, 0, j)),
        compiler_params=_cparams(("arbitrary", "arbitrary")),
        name="ada_ln",
    )(c_all, w_ada_bf, b_ada.reshape(depth, 1, d6))


def _inproj_kernel(x_ref, mod_ref, nw_ref, w_ref, o_ref):
    h = _rms_mod(x_ref[0], nw_ref[...], mod_ref[0, 0:1, :], mod_ref[0, 1:2, :])
    o_ref[0] = _mm(h.astype(BF16), w_ref[...])


def _inproj_call(xall, mod, nw, w_bf, nt_lat):
    b, t, d = xall.shape
    nt = t // ROW_TILE
    pw = w_bf.shape[1]
    return pl.pallas_call(
        _inproj_kernel,
        out_shape=jax.ShapeDtypeStruct((b, t, pw), F32),
        grid=(b, nt),
        in_specs=[
            pl.BlockSpec((1, ROW_TILE, d), lambda i, j: (i, j, 0)),
            pl.BlockSpec((1, 6, d), lambda i, j: (i * 2 + j // nt_lat, 0, 0)),
            pl.BlockSpec((1, d), lambda i, j: (0, 0)),
            pl.BlockSpec((d, pw), lambda i, j: (0, 0), pipeline_mode=pl.Buffered(1)),
        ],
        out_specs=pl.BlockSpec((1, ROW_TILE, pw), lambda i, j: (i, j, 0)),
        compiler_params=_cparams(("arbitrary", "arbitrary")),
        name="in_proj",
    )(xall, mod, nw, w_bf)


def _shift_rows(x, prev_row, next_row, gpos, seg_starts, seg_ends):
    n = x.shape[0]
    row = lax.broadcasted_iota(jnp.int32, (n, 1), 0)
    xm = jnp.where(row == 0, prev_row, pltpu.roll(x, 1, axis=0))
    xp = jnp.where(row == n - 1, next_row, pltpu.roll(x, n - 1, axis=0))
    at_start = functools.reduce(jnp.logical_or, [gpos == s for s in seg_starts])
    at_end = functools.reduce(jnp.logical_or, [gpos == e - 1 for e in seg_ends])
    return jnp.where(at_start, 0.0, xm), jnp.where(at_end, 0.0, xp)


def _dn_prep_kernel(x_ref, xp_ref, xn_ref, ba_ref, cw_ref, aneg_ref, dtb_ref, e3_ref, g2_ref,
                    qkv_ref, aux_ref, *, n_lat, n_all):
    tm = x_ref.shape[1]
    x = x_ref[0]
    gpos = pl.program_id(1) * tm + lax.broadcasted_iota(jnp.int32, (tm, 1), 0)
    xm, xp = _shift_rows(x, xp_ref[0, SUBLANES - 1:SUBLANES, :], xn_ref[0, 0:1, :], gpos,
                         (0, n_lat), (n_lat, n_all))
    cw = cw_ref[...]
    y = _silu(xm * cw[0:1] + x * cw[1:2] + xp * cw[2:3])
    q, k, v = y[:, 0:GROUP_W], y[:, GROUP_W:2 * GROUP_W], y[:, 2 * GROUP_W:3 * GROUP_W]
    qn = q * lax.rsqrt(_group_sum(q * q, g2_ref) + EPS) * (HEAD_DIM ** -0.5)
    kn = k * lax.rsqrt(_group_sum(k * k, g2_ref) + EPS)
    qkv_ref[0] = jnp.concatenate([qn, kn, v], axis=1)
    ba = ba_ref[0]
    lane = lax.broadcasted_iota(jnp.int32, ba.shape, 1)
    z = ba + dtb_ref[...]
    softplus = jnp.maximum(z, 0.0) + jnp.log(1.0 + jnp.exp(-jnp.abs(z)))
    gates = jnp.where(lane < 2 * GROUP_HEADS, jax.nn.sigmoid(ba), aneg_ref[...] * softplus)
    gates = jnp.where(lane < 4 * GROUP_HEADS, gates, 0.0)
    hi, mid, lo = _split3(gates)
    aux_ref[0] = _mm(jnp.concatenate([hi, mid, lo], axis=1), e3_ref[...])


def _halo_specs(width, col, n_rows_total, tm):
    per = tm // SUBLANES
    last = n_rows_total // SUBLANES - 1
    prev = pl.BlockSpec((1, SUBLANES, width), lambda i, j: (i, jnp.maximum(j * per - 1, 0), col))
    nxt = pl.BlockSpec((1, SUBLANES, width), lambda i, j: (i, jnp.minimum((j + 1) * per, last), col))
    return prev, nxt


def _dn_prep_call(p, conv_w, aneg, dtb, e3, g2, n_lat):
    b, t, _ = p.shape
    nt = t // ROW_TILE
    w3 = 3 * GROUP_W
    prev, nxt = _halo_specs(w3, 0, t, ROW_TILE)
    kern = functools.partial(_dn_prep_kernel, n_lat=n_lat, n_all=t)
    return pl.pallas_call(
        kern,
        out_shape=(jax.ShapeDtypeStruct((b, t, w3), F32), jax.ShapeDtypeStruct((b, t, 4 * GROUP_W), F32)),
        grid=(b, nt),
        in_specs=[
            pl.BlockSpec((1, ROW_TILE, w3), lambda i, j: (i, j, 0)),
            prev, nxt,
            pl.BlockSpec((1, ROW_TILE, LANES), lambda i, j: (i, j, C_DN_BA_128)),
            pl.BlockSpec((3, w3), lambda i, j: (0, 0)),
            pl.BlockSpec((1, LANES), lambda i, j: (0, 0)),
            pl.BlockSpec((1, LANES), lambda i, j: (0, 0)),
            pl.BlockSpec(e3.shape, lambda i, j: (0, 0)),
            pl.BlockSpec(g2.shape, lambda i, j: (0, 0)),
        ],
        out_specs=(pl.BlockSpec((1, ROW_TILE, w3), lambda i, j: (i, j, 0)),
                   pl.BlockSpec((1, ROW_TILE, 4 * GROUP_W), lambda i, j: (i, j, 0))),
        compiler_params=_cparams(("arbitrary", "arbitrary")),
        name="dn_prep",
    )(p, p, p, p, conv_w, aneg, dtb, e3, g2)


def _scan_tile(d, s, nt_lat):
    return jnp.where(s == 0, nt_lat, jnp.where(d == 0, s - 1, nt_lat - s))


def _unit_tri_inverse(mats, lv_ref):
    n = mats[0].shape[0]
    eye = (lax.broadcasted_iota(jnp.int32, (n, n), 0) == lax.broadcasted_iota(jnp.int32, (n, n), 1)).astype(F32)
    n1 = [a * lv_ref[0] for a in mats]
    n1b = [x.astype(BF16) for x in n1]
    n2 = [_mm(x, x) for x in n1b]
    n2b = [x.astype(BF16) for x in n2]
    n4 = [_mm(x, x) for x in n2b]
    inner = [_mm((eye + x2).astype(BF16), (eye + x4).astype(BF16)) for x2, x4 in zip(n2, n4)]
    ts = [_mm((eye - x1).astype(BF16), y.astype(BF16)) for x1, y in zip(n1, inner)]
    for lvl in (1, 2, 3):
        tb = [t.astype(BF16) for t in ts]
        xs = [_mm((a * lv_ref[lvl]).astype(BF16), t) for a, t in zip(mats, tb)]
        ts = [t - _mm(t16, x.astype(BF16)) for t, t16, x in zip(ts, tb, xs)]
    return ts


def _dn_scan_kernel(qkv_ref, beta_ref, la_ref, dm_ref, lv_ref, tri_ref, o_ref, s_ref):
    d = pl.program_id(1)

    @pl.when(pl.program_id(2) == 0)
    def _():
        s_ref[...] = jnp.zeros_like(s_ref)

    hms = _head_masks()
    incl = dm_ref[0, 0]
    strict = dm_ref[0, 1]
    n_chunks = qkv_ref.shape[1] // CHUNK
    rows, a_mats, attn, w_in, u_in, qg, kdec_t, glast = [], [], [], [], [], [], [], []
    for i in range(n_chunks):
        r0 = pl.multiple_of(jnp.where(d == 0, i, n_chunks - 1 - i) * CHUNK, CHUNK)
        rows.append(pl.ds(r0, CHUNK))
        q = qkv_ref[0, rows[i], 0:GROUP_W]
        k = qkv_ref[0, rows[i], GROUP_W:2 * GROUP_W]
        v = qkv_ref[0, rows[i], 2 * GROUP_W:3 * GROUP_W]
        beta = beta_ref[0, rows[i], :]
        hi, mid, lo = _split3(la_ref[0, rows[i], :])
        g = _mm(tri_ref[0], jnp.concatenate([hi, mid, lo], axis=0))
        gtot = jnp.where(d == 0, g[CHUNK - 1:CHUNK, :], g[0:1, :])
        eg = jnp.exp(g)
        rg = _tile4(g)
        decay = jnp.exp((rg - rg.T) * incl) * incl
        mk = _mrs(k, hms).astype(BF16)
        a_mats.append(strict * _tile4(beta) * _nt(mk, mk) * decay)
        attn.append((_nt(_mrs(q, hms).astype(BF16), mk) * decay).astype(BF16))
        w_in.append(_mrs(k * beta * eg, hms).astype(BF16))
        u_in.append(_mrs(v * beta, hms).astype(BF16))
        qg.append(_mrs(q * eg, hms).astype(BF16))
        kdec_t.append(_mrs(k * jnp.exp(gtot - g), hms).T.astype(BF16))
        glast.append(jnp.exp(gtot))
    tinv = [t.astype(BF16) for t in _unit_tri_inverse(a_mats, lv_ref)]
    w_bd = [_mm(t, x).astype(BF16) for t, x in zip(tinv, w_in)]
    u_bd = [_mm(t, x) for t, x in zip(tinv, u_in)]
    for i in range(n_chunks):
        sv = s_ref[...]
        svb = sv.astype(BF16)
        v_newb = (u_bd[i] - _mm(w_bd[i], svb)).astype(BF16)
        o_ref[0, 0, rows[i], :] = _rbs(_mm(qg[i], svb) + _mm(attn[i], v_newb))
        s_ref[...] = sv * glast[i] + _mm(kdec_t[i], v_newb)


def _dn_scan_call(qkv, aux, dmask, lvmask, tri3, nt_lat):
    b, t, w3 = qkv.shape
    ns = t // ROW_TILE
    return pl.pallas_call(
        _dn_scan_kernel,
        out_shape=jax.ShapeDtypeStruct((b, 2, t, GROUP_W), F32),
        grid=(b, 2, ns),
        in_specs=[
            pl.BlockSpec((1, ROW_TILE, w3), lambda i, d, s: (i, _scan_tile(d, s, nt_lat), 0)),
            pl.BlockSpec((1, ROW_TILE, GROUP_W), lambda i, d, s: (i, _scan_tile(d, s, nt_lat), d)),
            pl.BlockSpec((1, ROW_TILE, GROUP_W), lambda i, d, s: (i, _scan_tile(d, s, nt_lat), 2 + d)),
            pl.BlockSpec((1,) + dmask.shape[1:], lambda i, d, s: (d, 0, 0, 0)),
            pl.BlockSpec(lvmask.shape, lambda i, d, s: (0, 0, 0)),
            pl.BlockSpec((1,) + tri3.shape[1:], lambda i, d, s: (d, 0, 0)),
        ],
        out_specs=pl.BlockSpec((1, 1, ROW_TILE, GROUP_W), lambda i, d, s: (i, d, _scan_tile(d, s, nt_lat), 0)),
        scratch_shapes=[pltpu.VMEM((GROUP_W, GROUP_W), F32)],
        compiler_params=_cparams(("arbitrary", "arbitrary", "arbitrary")),
        name="dn_scan",
    )(qkv, aux, aux, dmask, lvmask, tri3)


HG_LEVELS = (1, 2, 4, 8, 16, 32)


def _hg_scan_kernel(q_ref, i_ref, f_ref, lb_ref, w3_ref, lm_ref, same_ref, o_ref, st_ref, qb_ref, ke_ref, oi_ref):
    d = pl.program_id(1)

    @pl.when(pl.program_id(2) == 0)
    def _():
        st_ref[...] = jnp.zeros_like(st_ref)

    hms = _head_masks()
    same = same_ref[...]
    lb = lb_ref[0]
    tm = q_ref.shape[1]
    n_chunks = tm // CHUNK
    nl = len(HG_LEVELS)
    chunks = range(n_chunks)
    rows = [slice(c * CHUNK, (c + 1) * CHUNK) for c in chunks]
    qs = [q_ref[0, r, :] * (HEAD_DIM ** -0.5) for r in rows]
    fs = [lb + (1.0 - lb) * jax.nn.sigmoid(f_ref[0, r, :]) for r in rows]
    kxs = [1.0 - f for f in fs]
    es = [_mm(w3_ref[0], jnp.concatenate(_split3(jnp.log(f)), axis=0)) for f in fs]
    atts = [lm_ref[0, 0] * _nt(_mrs(q, hms).astype(BF16), kx.astype(BF16)) for q, kx in zip(qs, kxs)]
    for li in range(nl):
        ea = slice((2 + li) * CHUNK, (3 + li) * CHUNK)
        eb = slice((2 + nl + li) * CHUNK, (3 + nl + li) * CHUNK)
        terms = [_nt(_mrs(q * jnp.exp(e[ea]), hms).astype(BF16), (kx * jnp.exp(e[eb])).astype(BF16))
                 for q, kx, e in zip(qs, kxs, es)]
        atts = [a + lm_ref[0, 1 + li] * t for a, t in zip(atts, terms)]
    tots = []
    for c in chunks:
        cum = es[c][0:CHUNK]
        tot = es[c][CHUNK:2 * CHUNK]
        oi_ref[rows[c], :] = _rbs(_mm(atts[c].astype(BF16), i_ref[0, rows[c], :].astype(BF16)) * same)
        qb_ref[rows[c], :] = qs[c] * jnp.exp(cum)
        ke_ref[rows[c], :] = kxs[c] * jnp.exp(tot - cum)
        tots.append(tot[0:1, :])
    vt = i_ref[0].T
    lane = lax.broadcasted_iota(jnp.int32, (1, tm), 1)
    keb = ke_ref[...].astype(BF16)
    for i in range(n_chunks):
        c = jnp.where(d == 0, i, n_chunks - 1 - i)
        rows = pl.ds(pl.multiple_of(c * CHUNK, CHUNK), CHUNK)
        st = st_ref[...]
        o_ref[0, 0, rows, :] = oi_ref[rows, :] + _nt(qb_ref[rows, :].astype(BF16), st.astype(BF16))
        tot = functools.reduce(lambda acc, j: jnp.where(c == j, tots[j], acc), range(1, n_chunks), tots[0])
        vt_c = (vt * (lane // CHUNK == c).astype(F32)).astype(BF16)
        st_ref[...] = st * jnp.exp(tot) + same * _mm(vt_c, keb)


def _hg_scan_call(p, lb, w3, lmask, same, nt_lat):
    b, t, _ = p.shape
    ns = t // ROW_TILE
    blk = lambda col: pl.BlockSpec((1, ROW_TILE, GROUP_W), lambda i, d, s: (i, _scan_tile(d, s, nt_lat), col))
    return pl.pallas_call(
        _hg_scan_kernel,
        out_shape=jax.ShapeDtypeStruct((b, 2, t, GROUP_W), F32),
        grid=(b, 2, ns),
        in_specs=[
            blk(C_HG_Q), blk(C_HG_I),
            pl.BlockSpec((1, ROW_TILE, GROUP_W), lambda i, d, s: (i, _scan_tile(d, s, nt_lat), C_HG_FF + d)),
            pl.BlockSpec((1, 1, GROUP_W), lambda i, d, s: (d, 0, 0)),
            pl.BlockSpec((1,) + w3.shape[1:], lambda i, d, s: (d, 0, 0)),
            pl.BlockSpec((1,) + lmask.shape[1:], lambda i, d, s: (d, 0, 0, 0)),
            pl.BlockSpec(same.shape, lambda i, d, s: (0, 0)),
        ],
        out_specs=pl.BlockSpec((1, 1, ROW_TILE, GROUP_W), lambda i, d, s: (i, d, _scan_tile(d, s, nt_lat), 0)),
        scratch_shapes=[pltpu.VMEM((GROUP_W, GROUP_W), F32), pltpu.VMEM((ROW_TILE, GROUP_W), F32),
                        pltpu.VMEM((ROW_TILE, GROUP_W), F32), pltpu.VMEM((ROW_TILE, GROUP_W), F32)],
        compiler_params=_cparams(("arbitrary", "arbitrary", "arbitrary")),
        name="hg_scan",
    )(p, p, p, lb, w3, lmask, same)


def _na_kernel(q_ref, kp_ref, kc_ref, kn_ref, vp_ref, vc_ref, vn_ref, kx_ref, vx_ref, bias_ref, same_ref,
               o_ref, kw_ref, vw_ref, *, n_rows):
    i = pl.program_id(1)
    blk = q_ref.shape[1]
    for j, (kr, vr) in enumerate(((kp_ref, vp_ref), (kc_ref, vc_ref), (kn_ref, vn_ref))):
        kw_ref[j * blk:(j + 1) * blk, :] = kr[0].astype(BF16)
        vw_ref[j * blk:(j + 1) * blk, :] = vr[0].astype(BF16)
    kx = kx_ref[0].astype(BF16)
    vx = vx_ref[0].astype(BF16)
    hms = _head_masks()
    same = same_ref[...]
    win = NA_KH * GRID_W

    def row_body(rr, carry):
        r = i * NA_ROWS + rr
        rs = jnp.clip(r - NA_KH // 2, 0, n_rows - NA_KH)
        start = pl.multiple_of((rs - i * NA_ROWS + NA_ROWS) * GRID_W, GRID_W)
        rows = pl.ds(pl.multiple_of(rr * GRID_W, GRID_W), GRID_W)
        mq = _mrs(q_ref[0, rows, :] * (HEAD_DIM ** -0.5), hms).astype(BF16)
        s_loc = _nt(mq, kw_ref[pl.ds(start, win), :]) + bias_ref[rs - r + NA_KH - 1]
        s_ctx = _nt(mq, kx)
        m = jnp.maximum(jnp.max(s_loc, axis=-1, keepdims=True), jnp.max(s_ctx, axis=-1, keepdims=True))
        p_loc = jnp.exp(s_loc - m)
        p_ctx = jnp.exp(s_ctx - m)
        den = jnp.sum(p_loc, axis=-1, keepdims=True) + jnp.sum(p_ctx, axis=-1, keepdims=True)
        o = (_mm(p_loc.astype(BF16), vw_ref[pl.ds(start, win), :]) + _mm(p_ctx.astype(BF16), vx)) / den
        o_ref[0, rows, :] = _rbs(o * same)
        return carry

    lax.fori_loop(0, NA_ROWS, row_body, 0, unroll=2)


def _na_call(p, bias, same, n_lat, n_ctx):
    b, t, _ = p.shape
    blk = NA_ROWS * GRID_W
    nb = n_lat // blk
    spec = lambda col, f: pl.BlockSpec((1, blk, GROUP_W), lambda i, j: (i, f(j), col))
    prev = lambda j: jnp.maximum(j - 1, 0)
    cur = lambda j: j
    nxt = lambda j: jnp.minimum(j + 1, nb - 1)
    ctx_spec = lambda col: pl.BlockSpec((1, n_ctx, GROUP_W), lambda i, j: (i, n_lat // n_ctx, col))
    kern = functools.partial(_na_kernel, n_rows=n_lat // GRID_W)
    return pl.pallas_call(
        kern,
        out_shape=jax.ShapeDtypeStruct((b, n_lat, GROUP_W), F32),
        grid=(b, nb),
        in_specs=[spec(C_NA_Q, cur),
                  spec(C_NA_K, prev), spec(C_NA_K, cur), spec(C_NA_K, nxt),
                  spec(C_NA_V, prev), spec(C_NA_V, cur), spec(C_NA_V, nxt),
                  ctx_spec(C_NA_K), ctx_spec(C_NA_V),
                  pl.BlockSpec(bias.shape, lambda i, j: (0, 0, 0), pipeline_mode=pl.Buffered(1)),
                  pl.BlockSpec(same.shape, lambda i, j: (0, 0))],
        out_specs=pl.BlockSpec((1, blk, GROUP_W), lambda i, j: (i, j, 0)),
        scratch_shapes=[pltpu.VMEM((3 * blk, GROUP_W), BF16), pltpu.VMEM((3 * blk, GROUP_W), BF16)],
        compiler_params=_cparams(("arbitrary", "arbitrary")),
        name="na_attn",
    )(p, p, p, p, p, p, p, p, p, bias, same)


def _na_ctx_kernel(q_ref, k_ref, v_ref, o_ref):
    hms = _head_masks()
    n = q_ref.shape[1]
    mq = _mrs(q_ref[0] * (HEAD_DIM ** -0.5), hms).astype(BF16)
    s = _nt(mq, k_ref[0].astype(BF16))
    p = jnp.exp(s - jnp.max(s, axis=-1, keepdims=True))
    o = _mm(p.astype(BF16), v_ref[0].astype(BF16)) / jnp.sum(p, axis=-1, keepdims=True)
    o_ref[0] = _rbs(o * jnp.concatenate([jnp.broadcast_to(hm, (n, GROUP_W)) for hm in hms], axis=0))


def _na_ctx_call(p, n_lat, n_ctx):
    b = p.shape[0]
    spec = lambda col: pl.BlockSpec((1, n_ctx, GROUP_W), lambda i: (i, n_lat // n_ctx, col))
    return pl.pallas_call(
        _na_ctx_kernel,
        out_shape=jax.ShapeDtypeStruct((b, n_ctx, GROUP_W), F32),
        grid=(b,),
        in_specs=[spec(C_NA_Q), spec(C_NA_K), spec(C_NA_V)],
        out_specs=pl.BlockSpec((1, n_ctx, GROUP_W), lambda i: (i, 0, 0)),
        compiler_params=_cparams(("arbitrary",)),
        name="na_ctx",
    )(p, p, p)


def _df_prep_kernel(q_ref, k_ref, v_ref, cos_ref, sin_ref, qt_ref, kr_ref, vt_ref):
    cos = cos_ref[...]
    sin = sin_ref[...]
    lane = lax.broadcasted_iota(jnp.int32, cos.shape, 1)
    first = (lane % 16) < 8

    def rope(x):
        w = x.shape[1]
        partner = jnp.where(first, pltpu.roll(x, w - 8, axis=1), pltpu.roll(x, 8, axis=1))
        return x * cos + partner * sin

    tm = q_ref.shape[1]
    qt = (rope(q_ref[0]) * (DF_HEAD_DIM ** -0.5 * LOG2E)).T
    row = lax.broadcasted_iota(jnp.int32, (GROUP_W, 1), 0)
    for s in range(2 * GROUP_HEADS):
        keep = (row // DF_HEAD_DIM == s).astype(F32)
        qt_ref[0, s] = (qt * keep).astype(BF16)
    kr_ref[0, 0] = rope(k_ref[0]).astype(BF16)
    vt = v_ref[0].T
    ext = (lax.broadcasted_iota(jnp.int32, (2 * SUBLANES, tm), 0) == 0).astype(F32)
    for h in range(GROUP_HEADS):
        vt_ref[0, h, 0] = jnp.concatenate([vt[h * HEAD_DIM:(h + 1) * HEAD_DIM], ext], axis=0).astype(BF16)


def _df_prep_call(p, cos, sin):
    b, t, _ = p.shape
    nt = t // DF_KC
    vrows = HEAD_DIM + 2 * SUBLANES
    blk = lambda col: pl.BlockSpec((1, DF_KC, GROUP_W), lambda i, j: (i, j, col))
    return pl.pallas_call(
        _df_prep_kernel,
        out_shape=(jax.ShapeDtypeStruct((b, 2 * GROUP_HEADS, GROUP_W, t), BF16),
                   jax.ShapeDtypeStruct((b, nt, DF_KC, GROUP_W), BF16),
                   jax.ShapeDtypeStruct((b, GROUP_HEADS, nt, vrows, DF_KC), BF16)),
        grid=(b, nt),
        in_specs=[blk(C_DF_Q), blk(C_DF_K), blk(C_DF_V),
                  pl.BlockSpec((DF_KC, GROUP_W), lambda i, j: (j, 0)),
                  pl.BlockSpec((DF_KC, GROUP_W), lambda i, j: (j, 0))],
        out_specs=(pl.BlockSpec((1, 2 * GROUP_HEADS, GROUP_W, DF_KC), lambda i, j: (i, 0, 0, j)),
                   pl.BlockSpec((1, 1, DF_KC, GROUP_W), lambda i, j: (i, j, 0, 0)),
                   pl.BlockSpec((1, GROUP_HEADS, 1, vrows, DF_KC), lambda i, j: (i, 0, j, 0, 0))),
        compiler_params=_cparams(("arbitrary", "arbitrary")),
        name="df_prep",
    )(p, p, p, cos, sin)


def _df_attn_kernel(qt_ref, k_ref, vt_ref, lam_ref, nw_ref, o_ref,
                    m_ref, acc_ref, s_ref, cm_ref, p_ref, al_ref, *, lam_init):
    nk = k_ref.shape[1]
    m_ref[...] = jnp.full(m_ref.shape, -jnp.inf, F32)
    acc_ref[...] = jnp.zeros_like(acc_ref)

    def scores(c, slot):
        kc = k_ref[0, c]
        for j in range(2):
            s = _mm(kc, qt_ref[0, j])
            s_ref[slot, j] = s
            cm_ref[slot, j] = jnp.max(s, axis=0, keepdims=True)

    def probs(slot):
        for j in range(2):
            m_old = m_ref[j]
            m_new = jnp.maximum(m_old, cm_ref[slot, j])
            p_ref[slot, j] = jnp.exp2(s_ref[slot, j] - m_new).astype(BF16)
            al_ref[slot, j] = jnp.exp2(m_old - m_new)
            m_ref[j] = m_new

    def accum(c, slot):
        vt = vt_ref[0, 0, c]
        for j in range(2):
            acc_ref[j] = acc_ref[j] * al_ref[slot, j] + _mm(vt, p_ref[slot, j])

    def iteration(i, slot, do_scores=True, do_probs=True, do_accum=True):
        if do_accum:
            accum(i - 2, slot)
        if do_probs:
            probs(1 - slot)
        if do_scores:
            scores(i, slot)

    for i in range(min(2, nk + 2)):
        iteration(i, i % 2, i < nk, 1 <= i <= nk, False)
    n_steady = max(nk - 2, 0)

    def group(k, carry):
        for u in range(DF_UNROLL):
            iteration(2 + DF_UNROLL * k + u, u % 2)
        return carry

    n_groups = n_steady // DF_UNROLL
    lax.fori_loop(0, n_groups, group, 0)
    for i in range(2 + DF_UNROLL * n_groups, nk + 2):
        iteration(i, i % 2, i < nk, i <= nk, True)
    lp = lam_ref[...]
    lam = (jnp.exp(jnp.sum(lp[0:1] * lp[1:2], keepdims=True)) - jnp.exp(jnp.sum(lp[2:3] * lp[3:4], keepdims=True))
           + lam_init)
    a0 = acc_ref[0]
    a1 = acc_ref[1]
    o = a0[0:HEAD_DIM] / a0[HEAD_DIM:HEAD_DIM + 1] - lam * (a1[0:HEAD_DIM] / a1[HEAD_DIM:HEAD_DIM + 1])
    ms = jnp.mean(o * o, axis=0, keepdims=True)
    o_ref[0] = o * lax.rsqrt(ms + EPS) * nw_ref[...] * (1.0 - lam_init)


def _df_attn_call(qt, kr, vt, lam_p, nw_col, lam_init, q_off, n_q, qb, k_off, nk):
    b = qt.shape[0]
    vrows = vt.shape[3]
    return pl.pallas_call(
        functools.partial(_df_attn_kernel, lam_init=lam_init),
        out_shape=jax.ShapeDtypeStruct((b, GROUP_W, n_q), F32),
        grid=(b, GROUP_HEADS, n_q // qb),
        in_specs=[
            pl.BlockSpec((1, 2, GROUP_W, qb), lambda i, h, q: (i, h, 0, q_off // qb + q)),
            pl.BlockSpec((1, nk, DF_KC, GROUP_W), lambda i, h, q: (i, k_off // nk, 0, 0),
                         pipeline_mode=pl.Buffered(1)),
            pl.BlockSpec((1, 1, nk, vrows, DF_KC), lambda i, h, q: (i, h, k_off // nk, 0, 0)),
            pl.BlockSpec(lam_p.shape, lambda i, h, q: (0, 0)),
            pl.BlockSpec(nw_col.shape, lambda i, h, q: (0, 0)),
        ],
        out_specs=pl.BlockSpec((1, HEAD_DIM, qb), lambda i, h, q: (i, h, q)),
        scratch_shapes=[pltpu.VMEM((2, 1, qb), F32), pltpu.VMEM((2, vrows, qb), F32),
                        pltpu.VMEM((2, 2, DF_KC, qb), F32), pltpu.VMEM((2, 2, 1, qb), F32),
                        pltpu.VMEM((2, 2, DF_KC, qb), BF16), pltpu.VMEM((2, 2, 1, qb), F32)],
        compiler_params=_cparams(("arbitrary", "arbitrary", "arbitrary")),
        name="df_attn",
    )(qt, kr, vt, lam_p, nw_col)


def _outproj_kernel(x_ref, mod_ref, dno_f, dno_b, dng_ref, na_ref, nac_ref, dft_ref, dftc_ref, hgo_f, hgo_b, hgg_ref,
                    dnw_ref, hgw_ref, g2_ref, w_ref, o_ref, *, nt_lat):
    inv = 1.0 / HEAD_DIM
    is_ctx = pl.program_id(1) == nt_lat

    def gated_norm(o, nw, gate):
        ms = _group_sum(o * o, g2_ref) * inv
        return o * lax.rsqrt(ms + EPS) * nw * _silu(gate)

    y_dn = gated_norm(dno_f[0, 0] + dno_b[0, 0], dnw_ref[...], dng_ref[0])
    y_hg = gated_norm(hgo_f[0, 0] + hgo_b[0, 0], hgw_ref[...], hgg_ref[0])
    y_df = jnp.where(is_ctx, dftc_ref[0], dft_ref[0]).T
    y_na = jnp.where(is_ctx, nac_ref[0], na_ref[0])
    acc = _mm(y_dn.astype(BF16), w_ref[0:GROUP_W, :])
    acc += _mm(y_na.astype(BF16), w_ref[GROUP_W:2 * GROUP_W, :])
    acc += _mm(y_df.astype(BF16), w_ref[2 * GROUP_W:3 * GROUP_W, :])
    acc += _mm(y_hg.astype(BF16), w_ref[3 * GROUP_W:4 * GROUP_W, :])
    o_ref[0] = x_ref[0] + mod_ref[0, 2:3, :] * acc


def _outproj_call(xall, mod, dn_o, p, y_na, y_na_ctx, y_dft, y_dft_ctx, hg_o, dnw, hgw, g2, w_bf, nt_lat, n_tiles):
    b, t, d = xall.shape
    row = lambda w, col=0: pl.BlockSpec((1, ROW_TILE, w), lambda i, j: (i, j, col))
    dirspec = lambda dd: pl.BlockSpec((1, 1, ROW_TILE, GROUP_W), lambda i, j: (i, dd, j, 0))
    const = lambda a: pl.BlockSpec(a.shape, lambda i, j: (0,) * a.ndim)
    lat = lambda j: jnp.minimum(j, nt_lat - 1)
    return pl.pallas_call(
        functools.partial(_outproj_kernel, nt_lat=nt_lat),
        out_shape=jax.ShapeDtypeStruct((b, n_tiles * ROW_TILE, d), F32),
        grid=(b, n_tiles),
        in_specs=[row(d),
                  pl.BlockSpec((1, 6, d), lambda i, j: (i * 2 + j // nt_lat, 0, 0)),
                  dirspec(0), dirspec(1), row(GROUP_W, C_DN_GATE),
                  pl.BlockSpec((1, ROW_TILE, GROUP_W), lambda i, j: (i, lat(j), 0)),
                  pl.BlockSpec((1, ROW_TILE, GROUP_W), lambda i, j: (i, 0, 0)),
                  pl.BlockSpec((1, GROUP_W, ROW_TILE), lambda i, j: (i, 0, lat(j))),
                  pl.BlockSpec((1, GROUP_W, ROW_TILE), lambda i, j: (i, 0, 0)),
                  dirspec(0), dirspec(1), row(GROUP_W, C_HG_GATE),
                  const(dnw), const(hgw), const(g2),
                  pl.BlockSpec(w_bf.shape, lambda i, j: (0, 0), pipeline_mode=pl.Buffered(1))],
        out_specs=row(d),
        compiler_params=_cparams(("arbitrary", "arbitrary")),
        name="out_proj",
    )(xall, mod, dn_o, dn_o, p, y_na, y_na_ctx, y_dft, y_dft_ctx, hg_o, hg_o, p, dnw, hgw, g2, w_bf)


def _ffn_kernel(x_ref, xp_ref, xn_ref, mod_ref, nw_ref, wup_ref, cw_ref, wdn_ref, fw_ref, o_ref,
                *, n_lat, n_all, final):
    tm = x_ref.shape[1]
    d_ff = wdn_ref.shape[0]
    fc = d_ff // FF_SPLIT
    x = x_ref[0]
    xe = jnp.concatenate([x, xp_ref[0], xn_ref[0]], axis=0)
    hb = _rms_mod(xe, nw_ref[...], mod_ref[0, 3:4, :], mod_ref[0, 4:5, :]).astype(BF16)
    gpos = pl.program_id(1) * tm + lax.broadcasted_iota(jnp.int32, (tm, 1), 0)
    acc = jnp.zeros((tm, x.shape[1]), F32)
    for f in range(FF_SPLIT):
        g_all = _mm(hb, wup_ref[:, f * fc:(f + 1) * fc])
        val = _mm(hb[0:tm], wup_ref[:, d_ff + f * fc:d_ff + (f + 1) * fc])
        g = g_all[0:tm]
        gm, gp = _shift_rows(g, g_all[tm + SUBLANES - 1:tm + SUBLANES], g_all[tm + SUBLANES:tm + SUBLANES + 1],
                             gpos, (0, n_lat), (n_lat, n_all))
        cw = cw_ref[:, f * fc:(f + 1) * fc]
        a = _silu(gm * cw[0:1] + g * cw[1:2] + gp * cw[2:3]) * val
        acc += _mm(a.astype(BF16), wdn_ref[f * fc:(f + 1) * fc, :])
    y = x + mod_ref[0, 5:6, :] * acc
    if final:
        ms = jnp.mean(y * y, axis=-1, keepdims=True)
        y = y * lax.rsqrt(ms + EPS) * fw_ref[...]
    o_ref[0] = y


def _ffn_call(xmid, mod, nw, wup_bf, cw, wdn_bf, fw, nt_lat, n_tiles, n_lat, final):
    b, t, d = xmid.shape
    out_rows = n_tiles * ROW_TILE
    prev, nxt = _halo_specs(d, 0, out_rows, ROW_TILE)
    kern = functools.partial(_ffn_kernel, n_lat=n_lat, n_all=t, final=final)
    const = lambda a, **kw: pl.BlockSpec(a.shape, lambda i, j: (0,) * a.ndim, **kw)
    return pl.pallas_call(
        kern,
        out_shape=jax.ShapeDtypeStruct((b, out_rows, d), F32),
        grid=(b, n_tiles),
        in_specs=[pl.BlockSpec((1, ROW_TILE, d), lambda i, j: (i, j, 0)), prev, nxt,
                  pl.BlockSpec((1, 6, d), lambda i, j: (i * 2 + j // nt_lat, 0, 0)),
                  const(nw), const(wup_bf, pipeline_mode=pl.Buffered(1)), const(cw),
                  const(wdn_bf, pipeline_mode=pl.Buffered(1)), const(fw)],
        out_specs=pl.BlockSpec((1, ROW_TILE, d), lambda i, j: (i, j, 0)),
        compiler_params=_cparams(("arbitrary", "arbitrary")),
        name="conv_ffn",
    )(xmid, xmid, xmid, mod, nw, wup_bf, cw, wdn_bf, fw)


def _scan_constants():
    c = CHUNK
    n = GROUP_W
    t = np.arange(c)
    tri_f = (t[None, :] <= t[:, None]).astype(np.float32)
    tri = np.stack([tri_f, tri_f[::-1, ::-1]])
    tri3 = np.concatenate([tri] * 3, axis=2)

    idx = np.arange(n)
    hh, tt = idx // c, idx % c
    same = hh[:, None] == hh[None, :]
    ti, si = tt[:, None], tt[None, :]
    dmask = np.stack([np.stack([same & (si <= ti), same & (si < ti)]),
                      np.stack([same & (si >= ti), same & (si > ti)])]).astype(np.float32)
    lv = [same & (ti // 8 == si // 8)]
    for m in (8, 16, 32):
        lv.append(same & (ti // (2 * m) == si // (2 * m)) & (ti // m != si // m))
    lvmask = np.stack(lv).astype(np.float32)

    u = t[None, :]
    ws = [tri_f, np.ones((c, c), np.float32)]
    for m in HG_LEVELS:
        r = (t // m) * m
        ws.append(((u > r[:, None]) & (u <= t[:, None])).astype(np.float32))
    for m in HG_LEVELS:
        r2 = np.minimum((t // m + 1) * m, c - 1)
        ws.append(((u > t[:, None]) & (u <= r2[:, None])).astype(np.float32))
    w_f = np.concatenate(ws, axis=0)
    w_b = np.concatenate([w[::-1, ::-1] for w in ws], axis=0)
    w3 = np.stack([np.concatenate([w_f] * 3, axis=1), np.concatenate([w_b] * 3, axis=1)])

    lms = [np.eye(c, dtype=bool)]
    for m in HG_LEVELS:
        lms.append(((t[:, None] // m) % 2 == 1) & (t[None, :] // m == t[:, None] // m - 1))
    lm_f = np.stack(lms)
    lm_b = lm_f[:, ::-1, ::-1]
    lmask = np.stack([np.tile(lm_f, (1, GROUP_HEADS, 1)), np.tile(lm_b, (1, GROUP_HEADS, 1))]).astype(np.float32)

    g = same.astype(np.float32)
    g2 = np.concatenate([g, g], axis=0)
    e = np.zeros((LANES, 4 * GROUP_W), np.float32)
    for j in range(4 * GROUP_HEADS):
        e[j, j * HEAD_DIM:(j + 1) * HEAD_DIM] = 1.0
    e3 = np.concatenate([e, e, e], axis=0)
    return dict(tri3=jnp.asarray(tri3, BF16), dmask=jnp.asarray(dmask), lvmask=jnp.asarray(lvmask),
                w3=jnp.asarray(w3, BF16), lmask=jnp.asarray(lmask), same=jnp.asarray(g),
                g2=jnp.asarray(g2, BF16), e3=jnp.asarray(e3, BF16))


def _na_bias_table(rpb):
    cols = np.arange(GRID_W)
    c_start = np.clip(cols - NA_KW // 2, 0, GRID_W - NA_KW)
    kc = np.arange(GRID_W)
    valid = (kc[None, :] >= c_start[:, None]) & (kc[None, :] < c_start[:, None] + NA_KW)
    dc = np.clip(kc[None, :] - cols[:, None] + NA_KW - 1, 0, 2 * NA_KW - 2)
    dr = np.arange(NA_KH)[:, None] + np.arange(NA_KH)[None, :]
    n_dc = 2 * NA_KW - 1
    idx = dr[:, None, :, None] * n_dc + dc[None, :, None, :]
    ok = np.broadcast_to(valid[None, :, None, :], idx.shape)
    flat = rpb.reshape(GROUP_HEADS, -1).astype(F32)
    t = jnp.where(ok[None], jnp.take(flat, jnp.asarray(idx.reshape(-1)), axis=1).reshape((GROUP_HEADS,) + idx.shape),
                  -jnp.inf)
    return t.transpose(1, 0, 2, 3, 4).reshape(NA_KH, GROUP_HEADS * GRID_W, NA_KH * GRID_W)


def _rope_tables(n_lat, n_ctx):
    pos = jnp.arange(n_lat)
    row = (pos // GRID_W).astype(F32)
    col = (pos % GRID_W).astype(F32)
    half = DF_HEAD_DIM // 2
    inv = 1.0 / (ROPE_BASE ** (jnp.arange(0, half, 2, dtype=F32) / half))
    lane = np.arange(GROUP_W)
    use_row = jnp.asarray((lane % DF_HEAD_DIM) < half)
    ang = jnp.where(use_row[None, :], row[:, None], col[:, None]) * inv[lane % (half // 2)][None, :]
    sign = jnp.asarray(np.where((lane % half) < half // 2, -1.0, 1.0), F32)
    cos = jnp.concatenate([jnp.cos(ang), jnp.ones((n_ctx, GROUP_W), F32)], axis=0)
    sin = jnp.concatenate([jnp.sin(ang) * sign[None, :], jnp.zeros((n_ctx, GROUP_W), F32)], axis=0)
    return cos, sin


def _reorder_w_in(w_in_l):
    d = w_in_l.shape[0]
    a = 4 * GROUP_W
    ba = w_in_l[:, a:a + 4 * GROUP_HEADS]
    rest = w_in_l[:, a + 4 * GROUP_HEADS:]
    pad = jnp.zeros((d, LANES - 4 * GROUP_HEADS), w_in_l.dtype)
    return jnp.concatenate([w_in_l[:, :a], rest, ba, pad], axis=1)


def kernel(x, c, ctx, c_ctx, w_ada, b_ada, norm1_w, norm2_w, w_in, dn_conv_w, dn_a_log, dn_dt_bias, dn_norm_w,
           na_rpb, df_lambda, df_norm_w, hg_lb_raw, hg_norm_w, w_out, w_up, ffn_conv_w, w_down, final_norm_w):
    b, n_lat, d = x.shape
    n_ctx = ctx.shape[1]
    depth = w_ada.shape[0]
    t = n_lat + n_ctx
    assert n_ctx == ROW_TILE == DF_KC and n_lat % (NA_ROWS * GRID_W) == 0 and n_lat % DF_QB == 0
    assert b <= SUBLANES - 1
    nt_lat = n_lat // ROW_TILE
    nt_all = t // ROW_TILE

    consts = _scan_constants()
    cos, sin = _rope_tables(n_lat, n_ctx)

    c_all = jnp.zeros((SUBLANES, d), F32).at[:b].set(c).at[b].set(c_ctx)
    ada = _ada_call(c_all, w_ada.astype(BF16), b_ada).reshape(depth, SUBLANES, 6, d)

    lb_all = jnp.cumsum(jax.nn.softmax(hg_lb_raw.astype(F32), axis=0), axis=0)
    lb_all = (lb_all - lb_all[0]).reshape(depth, 2, 1, GROUP_W)

    xall = jnp.concatenate([x, ctx], axis=1)
    out = None
    for l in range(depth):
        need_ctx = l < depth - 1
        n_tiles = nt_all if need_ctx else nt_lat
        lam_init = 0.8 - 0.6 * math.exp(-0.3 * l)
        mod = jnp.stack([ada[l, :b], jnp.broadcast_to(ada[l, b], (b, 6, d))], axis=1).reshape(2 * b, 6, d)

        p = _inproj_call(xall, mod, norm1_w[l].reshape(1, d), _reorder_w_in(w_in[l]).astype(BF16), nt_lat)

        lane = np.arange(LANES)
        gate_idx = np.clip(lane - 2 * GROUP_HEADS, 0, 2 * GROUP_HEADS - 1)
        is_decay = jnp.asarray((lane >= 2 * GROUP_HEADS) & (lane < 4 * GROUP_HEADS))
        aneg = jnp.where(is_decay, -jnp.exp(dn_a_log[l].astype(F32).reshape(-1)[gate_idx]), 0.0).reshape(1, LANES)
        dtb = jnp.where(is_decay, dn_dt_bias[l].astype(F32).reshape(-1)[gate_idx], 0.0).reshape(1, LANES)
        dn_qkv, dn_aux = _dn_prep_call(p, dn_conv_w[l], aneg, dtb, consts["e3"], consts["g2"], n_lat)
        dn_o = _dn_scan_call(dn_qkv, dn_aux, consts["dmask"], consts["lvmask"], consts["tri3"], nt_lat)

        hg_o = _hg_scan_call(p, lb_all[l], consts["w3"], consts["lmask"], consts["same"], nt_lat)

        y_na = _na_call(p, _na_bias_table(na_rpb[l]), consts["same"], n_lat, n_ctx)
        y_na_ctx = _na_ctx_call(p, n_lat, n_ctx) if need_ctx else y_na

        qt, kr, vt = _df_prep_call(p, cos, sin)
        nw_col = df_norm_w[l].reshape(HEAD_DIM, 1)
        y_dft = _df_attn_call(qt, kr, vt, df_lambda[l], nw_col, lam_init, 0, n_lat, DF_QB, 0, t // DF_KC)
        y_dft_ctx = (_df_attn_call(qt, kr, vt, df_lambda[l], nw_col, lam_init, n_lat, n_ctx, n_ctx,
                                   n_lat // DF_KC, 1) if need_ctx else y_dft)

        xmid = _outproj_call(xall, mod, dn_o, p, y_na, y_na_ctx, y_dft, y_dft_ctx, hg_o,
                             jnp.tile(dn_norm_w[l], GROUP_HEADS).reshape(1, GROUP_W),
                             jnp.tile(hg_norm_w[l], GROUP_HEADS).reshape(1, GROUP_W),
                             consts["g2"], w_out[l].astype(BF16), nt_lat, n_tiles)
        out = _ffn_call(xmid, mod, norm2_w[l].reshape(1, d), w_up[l].astype(BF16), ffn_conv_w[l],
                        w_down[l].astype(BF16), final_norm_w.reshape(1, d), nt_lat, n_tiles, n_lat,
                        final=not need_ctx)
        xall = out
    return out
```

```python
import functools
import math

import numpy as np
import jax
import jax.numpy as jnp
from jax import lax
from jax.experimental import pallas as pl
from jax.experimental.pallas import tpu as pltpu

F32 = jnp.float32
BF16 = jnp.bfloat16

GRID_W = 64
HEAD_DIM = 64
GROUP_HEADS = 4
GROUP_W = GROUP_HEADS * HEAD_DIM
CHUNK = 64
NA_KH = 8
NA_KW = 16
DF_HEAD_DIM = HEAD_DIM // 2
ROPE_BASE = 10000.0
EPS = 1e-6
LOG2E = 1.4426950408889634

LANES = 128
SUBLANES = 8
VMEM_LIMIT_BYTES = 56 * 1024 * 1024

ROW_TILE = 256
NA_ROWS = 8
DF_QB = 512
DF_KC = 256
DF_UNROLL = 8
DF_BOUND_SLACK = 1.01
DF_MAX_SHIFT = 60.0
FF_SPLIT = 2

C_DN_Q, C_DN_K, C_DN_V, C_DN_GATE = 0, 1, 2, 3
C_NA_Q, C_NA_K, C_NA_V = 4, 5, 6
C_DF_Q, C_DF_K, C_DF_V = 7, 8, 9
C_HG_Q, C_HG_I, C_HG_FF, C_HG_FB, C_HG_GATE = 10, 11, 12, 13, 14
N_COL_BLOCKS = 15
P_WIDTH = N_COL_BLOCKS * GROUP_W + LANES
C_DN_BA_128 = N_COL_BLOCKS * GROUP_W // LANES


def _cparams(sem):
    return pltpu.CompilerParams(dimension_semantics=sem, vmem_limit_bytes=VMEM_LIMIT_BYTES)


def _mm(a, b):
    return jnp.dot(a, b, preferred_element_type=F32)


def _nt(a, b):
    return lax.dot_general(a, b, (((1,), (1,)), ((), ())), preferred_element_type=F32)


def _silu(x):
    return x * jax.nn.sigmoid(x)


def _split3(x):
    hi = x.astype(BF16)
    r = x - hi.astype(F32)
    mid = r.astype(BF16)
    lo = (r - mid.astype(F32)).astype(BF16)
    return hi, mid, lo


def _split2(x):
    hi = x.astype(BF16)
    lo = (x - hi.astype(F32)).astype(BF16)
    return hi, lo


def _head_masks(n_lanes=GROUP_W):
    lane = lax.broadcasted_iota(jnp.int32, (1, n_lanes), 1)
    return [(lane // HEAD_DIM == h).astype(F32) for h in range(GROUP_HEADS)]


def _mrs(x, hms):
    return jnp.concatenate([x * hm for hm in hms], axis=0)


def _rbs(z):
    n = z.shape[0] // GROUP_HEADS
    return z[0:n] + z[n:2 * n] + z[2 * n:3 * n] + z[3 * n:4 * n]


def _tile4(x):
    return jnp.concatenate([x, x, x, x], axis=0)


def _group_sum(x, g2_ref):
    hi, lo = _split2(x)
    return _mm(jnp.concatenate([hi, lo], axis=1), g2_ref[...])


def _rms_mod(x, nw, shift, scale):
    ms = jnp.mean(x * x, axis=-1, keepdims=True)
    return (x * lax.rsqrt(ms + EPS) * nw) * (1.0 + scale) + shift


def _ada_kernel(c_ref, w_ref, b_ref, o_ref):
    s = _silu(c_ref[...])
    o_ref[0] = _mm(s.astype(BF16), w_ref[0]) + b_ref[0]


def _ada_call(c_all, w_ada_bf, b_ada):
    depth, d, d6 = w_ada_bf.shape
    n_chunks = d6 // d
    return pl.pallas_call(
        _ada_kernel,
        out_shape=jax.ShapeDtypeStruct((depth, SUBLANES, d6), F32),
        grid=(depth, n_chunks),
        in_specs=[
            pl.BlockSpec((SUBLANES, d), lambda l, j: (0, 0)),
            pl.BlockSpec((1, d, d), lambda l, j: (l, 0, j)),
            pl.BlockSpec((1, 1, d), lambda l, j: (l, 0, j)),
        ],
        out_specs=pl.BlockSpec((1, SUBLANES, d), lambda l, j: (l, 0, j)),
        compiler_params=_cparams(("arbitrary", "arbitrary")),
        name="ada_ln",
    )(c_all, w_ada_bf, b_ada.reshape(depth, 1, d6))


def _inproj_kernel(x_ref, mod_ref, nw_ref, w_ref, o_ref):
    h = _rms_mod(x_ref[0], nw_ref[...], mod_ref[0, 0:1, :], mod_ref[0, 1:2, :])
    o_ref[0] = _mm(h.astype(BF16), w_ref[...])


def _inproj_call(xall, mod, nw, w_bf, nt_lat):
    b, t, d = xall.shape
    nt = t // ROW_TILE
    pw = w_bf.shape[1]
    return pl.pallas_call(
        _inproj_kernel,
        out_shape=jax.ShapeDtypeStruct((b, t, pw), F32),
        grid=(b, nt),
        in_specs=[
            pl.BlockSpec((1, ROW_TILE, d), lambda i, j: (i, j, 0)),
            pl.BlockSpec((1, 6, d), lambda i, j: (i * 2 + j // nt_lat, 0, 0)),
            pl.BlockSpec((1, d), lambda i, j: (0, 0)),
            pl.BlockSpec((d, pw), lambda i, j: (0, 0), pipeline_mode=pl.Buffered(1)),
        ],
        out_specs=pl.BlockSpec((1, ROW_TILE, pw), lambda i, j: (i, j, 0)),
        compiler_params=_cparams(("arbitrary", "arbitrary")),
        name="in_proj",
    )(xall, mod, nw, w_bf)


def _shift_rows(x, prev_row, next_row, gpos, seg_starts, seg_ends):
    n = x.shape[0]
    row = lax.broadcasted_iota(jnp.int32, (n, 1), 0)
    xm = jnp.where(row == 0, prev_row, pltpu.roll(x, 1, axis=0))
    xp = jnp.where(row == n - 1, next_row, pltpu.roll(x, n - 1, axis=0))
    at_start = functools.reduce(jnp.logical_or, [gpos == s for s in seg_starts])
    at_end = functools.reduce(jnp.logical_or, [gpos == e - 1 for e in seg_ends])
    return jnp.where(at_start, 0.0, xm), jnp.where(at_end, 0.0, xp)


def _dn_prep_kernel(x_ref, xp_ref, xn_ref, ba_ref, cw_ref, aneg_ref, dtb_ref, e3_ref, g2_ref,
                    qkv_ref, aux_ref, *, n_lat, n_all):
    tm = x_ref.shape[1]
    x = x_ref[0]
    gpos = pl.program_id(1) * tm + lax.broadcasted_iota(jnp.int32, (tm, 1), 0)
    xm, xp = _shift_rows(x, xp_ref[0, SUBLANES - 1:SUBLANES, :], xn_ref[0, 0:1, :], gpos,
                         (0, n_lat), (n_lat, n_all))
    cw = cw_ref[...]
    y = _silu(xm * cw[0:1] + x * cw[1:2] + xp * cw[2:3])
    q, k, v = y[:, 0:GROUP_W], y[:, GROUP_W:2 * GROUP_W], y[:, 2 * GROUP_W:3 * GROUP_W]
    qn = q * lax.rsqrt(_group_sum(q * q, g2_ref) + EPS) * (HEAD_DIM ** -0.5)
    kn = k * lax.rsqrt(_group_sum(k * k, g2_ref) + EPS)
    qkv_ref[0] = jnp.concatenate([qn, kn, v], axis=1)
    ba = ba_ref[0]
    lane = lax.broadcasted_iota(jnp.int32, ba.shape, 1)
    z = ba + dtb_ref[...]
    softplus = jnp.maximum(z, 0.0) + jnp.log(1.0 + jnp.exp(-jnp.abs(z)))
    gates = jnp.where(lane < 2 * GROUP_HEADS, jax.nn.sigmoid(ba), aneg_ref[...] * softplus)
    gates = jnp.where(lane < 4 * GROUP_HEADS, gates, 0.0)
    hi, mid, lo = _split3(gates)
    aux_ref[0] = _mm(jnp.concatenate([hi, mid, lo], axis=1), e3_ref[...])


def _halo_specs(width, col, n_rows_total, tm):
    per = tm // SUBLANES
    last = n_rows_total // SUBLANES - 1
    prev = pl.BlockSpec((1, SUBLANES, width), lambda i, j: (i, jnp.maximum(j * per - 1, 0), col))
    nxt = pl.BlockSpec((1, SUBLANES, width), lambda i, j: (i, jnp.minimum((j + 1) * per, last), col))
    return prev, nxt


def _dn_prep_call(p, conv_w, aneg, dtb, e3, g2, n_lat):
    b, t, _ = p.shape
    nt = t // ROW_TILE
    w3 = 3 * GROUP_W
    prev, nxt = _halo_specs(w3, 0, t, ROW_TILE)
    kern = functools.partial(_dn_prep_kernel, n_lat=n_lat, n_all=t)
    return pl.pallas_call(
        kern,
        out_shape=(jax.ShapeDtypeStruct((b, t, w3), F32), jax.ShapeDtypeStruct((b, t, 4 * GROUP_W), F32)),
        grid=(b, nt),
        in_specs=[
            pl.BlockSpec((1, ROW_TILE, w3), lambda i, j: (i, j, 0)),
            prev, nxt,
            pl.BlockSpec((1, ROW_TILE, LANES), lambda i, j: (i, j, C_DN_BA_128)),
            pl.BlockSpec((3, w3), lambda i, j: (0, 0)),
            pl.BlockSpec((1, LANES), lambda i, j: (0, 0)),
            pl.BlockSpec((1, LANES), lambda i, j: (0, 0)),
            pl.BlockSpec(e3.shape, lambda i, j: (0, 0)),
            pl.BlockSpec(g2.shape, lambda i, j: (0, 0)),
        ],
        out_specs=(pl.BlockSpec((1, ROW_TILE, w3), lambda i, j: (i, j, 0)),
                   pl.BlockSpec((1, ROW_TILE, 4 * GROUP_W), lambda i, j: (i, j, 0))),
        compiler_params=_cparams(("arbitrary", "arbitrary")),
        name="dn_prep",
    )(p, p, p, p, conv_w, aneg, dtb, e3, g2)


def _scan_tile(d, s, nt_lat):
    return jnp.where(s == 0, nt_lat, jnp.where(d == 0, s - 1, nt_lat - s))


def _unit_tri_inverse(mats, lv_ref):
    n = mats[0].shape[0]
    eye = (lax.broadcasted_iota(jnp.int32, (n, n), 0) == lax.broadcasted_iota(jnp.int32, (n, n), 1)).astype(F32)
    n1 = [a * lv_ref[0] for a in mats]
    n1b = [x.astype(BF16) for x in n1]
    n2 = [_mm(x, x) for x in n1b]
    n2b = [x.astype(BF16) for x in n2]
    n4 = [_mm(x, x) for x in n2b]
    inner = [_mm((eye + x2).astype(BF16), (eye + x4).astype(BF16)) for x2, x4 in zip(n2, n4)]
    ts = [_mm((eye - x1).astype(BF16), y.astype(BF16)) for x1, y in zip(n1, inner)]
    for lvl in (1, 2, 3):
        tb = [t.astype(BF16) for t in ts]
        xs = [_mm((a * lv_ref[lvl]).astype(BF16), t) for a, t in zip(mats, tb)]
        ts = [t - _mm(t16, x.astype(BF16)) for t, t16, x in zip(ts, tb, xs)]
    return ts


def _dn_scan_kernel(qkv_ref, beta_ref, la_ref, dm_ref, lv_ref, tri_ref, o_ref, s_ref):
    d = pl.program_id(1)

    @pl.when(pl.program_id(2) == 0)
    def _():
        s_ref[...] = jnp.zeros_like(s_ref)

    hms = _head_masks()
    incl = dm_ref[0, 0]
    strict = dm_ref[0, 1]
    n_chunks = qkv_ref.shape[1] // CHUNK
    rows, a_mats, attn, w_in, u_in, qg, kdec_t, glast = [], [], [], [], [], [], [], []
    for i in range(n_chunks):
        r0 = pl.multiple_of(jnp.where(d == 0, i, n_chunks - 1 - i) * CHUNK, CHUNK)
        rows.append(pl.ds(r0, CHUNK))
        q = qkv_ref[0, rows[i], 0:GROUP_W]
        k = qkv_ref[0, rows[i], GROUP_W:2 * GROUP_W]
        v = qkv_ref[0, rows[i], 2 * GROUP_W:3 * GROUP_W]
        beta = beta_ref[0, rows[i], :]
        hi, mid, lo = _split3(la_ref[0, rows[i], :])
        g = _mm(tri_ref[0], jnp.concatenate([hi, mid, lo], axis=0))
        gtot = jnp.where(d == 0, g[CHUNK - 1:CHUNK, :], g[0:1, :])
        eg = jnp.exp(g)
        rg = _tile4(g)
        decay = jnp.exp((rg - rg.T) * incl) * incl
        mk = _mrs(k, hms).astype(BF16)
        a_mats.append(strict * _tile4(beta) * _nt(mk, mk) * decay)
        attn.append((_nt(_mrs(q, hms).astype(BF16), mk) * decay).astype(BF16))
        w_in.append(_mrs(k * beta * eg, hms).astype(BF16))
        u_in.append(_mrs(v * beta, hms).astype(BF16))
        qg.append(_mrs(q * eg, hms).astype(BF16))
        kdec_t.append(_mrs(k * jnp.exp(gtot - g), hms).T.astype(BF16))
        glast.append(jnp.exp(gtot))
    tinv = [t.astype(BF16) for t in _unit_tri_inverse(a_mats, lv_ref)]
    w_bd = [_mm(t, x).astype(BF16) for t, x in zip(tinv, w_in)]
    u_bd = [_mm(t, x) for t, x in zip(tinv, u_in)]
    for i in range(n_chunks):
        sv = s_ref[...]
        svb = sv.astype(BF16)
        v_newb = (u_bd[i] - _mm(w_bd[i], svb)).astype(BF16)
        o_ref[0, 0, rows[i], :] = _rbs(_mm(qg[i], svb) + _mm(attn[i], v_newb))
        s_ref[...] = sv * glast[i] + _mm(kdec_t[i], v_newb)


def _dn_scan_call(qkv, aux, dmask, lvmask, tri3, nt_lat):
    b, t, w3 = qkv.shape
    ns = t // ROW_TILE
    return pl.pallas_call(
        _dn_scan_kernel,
        out_shape=jax.ShapeDtypeStruct((b, 2, t, GROUP_W), F32),
        grid=(b, 2, ns),
        in_specs=[
            pl.BlockSpec((1, ROW_TILE, w3), lambda i, d, s: (i, _scan_tile(d, s, nt_lat), 0)),
            pl.BlockSpec((1, ROW_TILE, GROUP_W), lambda i, d, s: (i, _scan_tile(d, s, nt_lat), d)),
            pl.BlockSpec((1, ROW_TILE, GROUP_W), lambda i, d, s: (i, _scan_tile(d, s, nt_lat), 2 + d)),
            pl.BlockSpec((1,) + dmask.shape[1:], lambda i, d, s: (d, 0, 0, 0)),
            pl.BlockSpec(lvmask.shape, lambda i, d, s: (0, 0, 0)),
            pl.BlockSpec((1,) + tri3.shape[1:], lambda i, d, s: (d, 0, 0)),
        ],
        out_specs=pl.BlockSpec((1, 1, ROW_TILE, GROUP_W), lambda i, d, s: (i, d, _scan_tile(d, s, nt_lat), 0)),
        scratch_shapes=[pltpu.VMEM((GROUP_W, GROUP_W), F32)],
        compiler_params=_cparams(("arbitrary", "arbitrary", "arbitrary")),
        name="dn_scan",
    )(qkv, aux, aux, dmask, lvmask, tri3)


HG_LEVELS = (1, 2, 4, 8, 16, 32)


def _hg_scan_kernel(q_ref, i_ref, f_ref, lb_ref, w3_ref, lm_ref, same_ref, o_ref, st_ref, qb_ref, ke_ref, oi_ref):
    d = pl.program_id(1)

    @pl.when(pl.program_id(2) == 0)
    def _():
        st_ref[...] = jnp.zeros_like(st_ref)

    hms = _head_masks()
    same = same_ref[...]
    lb = lb_ref[0]
    tm = q_ref.shape[1]
    n_chunks = tm // CHUNK
    nl = len(HG_LEVELS)
    chunks = range(n_chunks)
    rows = [slice(c * CHUNK, (c + 1) * CHUNK) for c in chunks]
    qs = [q_ref[0, r, :] * (HEAD_DIM ** -0.5) for r in rows]
    fs = [lb + (1.0 - lb) * jax.nn.sigmoid(f_ref[0, r, :]) for r in rows]
    kxs = [1.0 - f for f in fs]
    es = [_mm(w3_ref[0], jnp.concatenate(_split3(jnp.log(f)), axis=0)) for f in fs]
    atts = [lm_ref[0, 0] * _nt(_mrs(q, hms).astype(BF16), kx.astype(BF16)) for q, kx in zip(qs, kxs)]
    for li in range(nl):
        ea = slice((2 + li) * CHUNK, (3 + li) * CHUNK)
        eb = slice((2 + nl + li) * CHUNK, (3 + nl + li) * CHUNK)
        terms = [_nt(_mrs(q * jnp.exp(e[ea]), hms).astype(BF16), (kx * jnp.exp(e[eb])).astype(BF16))
                 for q, kx, e in zip(qs, kxs, es)]
        atts = [a + lm_ref[0, 1 + li] * t for a, t in zip(atts, terms)]
    tots = []
    for c in chunks:
        cum = es[c][0:CHUNK]
        tot = es[c][CHUNK:2 * CHUNK]
        oi_ref[rows[c], :] = _rbs(_mm(atts[c].astype(BF16), i_ref[0, rows[c], :].astype(BF16)) * same)
        qb_ref[rows[c], :] = qs[c] * jnp.exp(cum)
        ke_ref[rows[c], :] = kxs[c] * jnp.exp(tot - cum)
        tots.append(tot[0:1, :])
    vt = i_ref[0].T
    lane = lax.broadcasted_iota(jnp.int32, (1, tm), 1)
    keb = ke_ref[...].astype(BF16)
    for i in range(n_chunks):
        c = jnp.where(d == 0, i, n_chunks - 1 - i)
        rows = pl.ds(pl.multiple_of(c * CHUNK, CHUNK), CHUNK)
        st = st_ref[...]
        o_ref[0, 0, rows, :] = oi_ref[rows, :] + _nt(qb_ref[rows, :].astype(BF16), st.astype(BF16))
        tot = functools.reduce(lambda acc, j: jnp.where(c == j, tots[j], acc), range(1, n_chunks), tots[0])
        vt_c = (vt * (lane // CHUNK == c).astype(F32)).astype(BF16)
        st_ref[...] = st * jnp.exp(tot) + same * _mm(vt_c, keb)


def _hg_scan_call(p, lb, w3, lmask, same, nt_lat):
    b, t, _ = p.shape
    ns = t // ROW_TILE
    blk = lambda col: pl.BlockSpec((1, ROW_TILE, GROUP_W), lambda i, d, s: (i, _scan_tile(d, s, nt_lat), col))
    return pl.pallas_call(
        _hg_scan_kernel,
        out_shape=jax.ShapeDtypeStruct((b, 2, t, GROUP_W), F32),
        grid=(b, 2, ns),
        in_specs=[
            blk(C_HG_Q), blk(C_HG_I),
            pl.BlockSpec((1, ROW_TILE, GROUP_W), lambda i, d, s: (i, _scan_tile(d, s, nt_lat), C_HG_FF + d)),
            pl.BlockSpec((1, 1, GROUP_W), lambda i, d, s: (d, 0, 0)),
            pl.BlockSpec((1,) + w3.shape[1:], lambda i, d, s: (d, 0, 0)),
            pl.BlockSpec((1,) + lmask.shape[1:], lambda i, d, s: (d, 0, 0, 0)),
            pl.BlockSpec(same.shape, lambda i, d, s: (0, 0)),
        ],
        out_specs=pl.BlockSpec((1, 1, ROW_TILE, GROUP_W), lambda i, d, s: (i, d, _scan_tile(d, s, nt_lat), 0)),
        scratch_shapes=[pltpu.VMEM((GROUP_W, GROUP_W), F32), pltpu.VMEM((ROW_TILE, GROUP_W), F32),
                        pltpu.VMEM((ROW_TILE, GROUP_W), F32), pltpu.VMEM((ROW_TILE, GROUP_W), F32)],
        compiler_params=_cparams(("arbitrary", "arbitrary", "arbitrary")),
        name="hg_scan",
    )(p, p, p, lb, w3, lmask, same)


def _na_kernel(q_ref, kp_ref, kc_ref, kn_ref, vp_ref, vc_ref, vn_ref, kx_ref, vx_ref, bias_ref, same_ref,
               o_ref, kw_ref, vw_ref, *, n_rows):
    i = pl.program_id(1)
    blk = q_ref.shape[1]
    for j, (kr, vr) in enumerate(((kp_ref, vp_ref), (kc_ref, vc_ref), (kn_ref, vn_ref))):
        kw_ref[j * blk:(j + 1) * blk, :] = kr[0].astype(BF16)
        vw_ref[j * blk:(j + 1) * blk, :] = vr[0].astype(BF16)
    kx = kx_ref[0].astype(BF16)
    vx = vx_ref[0].astype(BF16)
    hms = _head_masks()
    same = same_ref[...]
    win = NA_KH * GRID_W

    def row_body(rr, carry):
        r = i * NA_ROWS + rr
        rs = jnp.clip(r - NA_KH // 2, 0, n_rows - NA_KH)
        start = pl.multiple_of((rs - i * NA_ROWS + NA_ROWS) * GRID_W, GRID_W)
        rows = pl.ds(pl.multiple_of(rr * GRID_W, GRID_W), GRID_W)
        mq = _mrs(q_ref[0, rows, :] * (HEAD_DIM ** -0.5), hms).astype(BF16)
        s_loc = _nt(mq, kw_ref[pl.ds(start, win), :]) + bias_ref[rs - r + NA_KH - 1]
        s_ctx = _nt(mq, kx)
        m = jnp.maximum(jnp.max(s_loc, axis=-1, keepdims=True), jnp.max(s_ctx, axis=-1, keepdims=True))
        p_loc = jnp.exp(s_loc - m)
        p_ctx = jnp.exp(s_ctx - m)
        den = jnp.sum(p_loc, axis=-1, keepdims=True) + jnp.sum(p_ctx, axis=-1, keepdims=True)
        o = (_mm(p_loc.astype(BF16), vw_ref[pl.ds(start, win), :]) + _mm(p_ctx.astype(BF16), vx)) / den
        o_ref[0, rows, :] = _rbs(o * same)
        return carry

    lax.fori_loop(0, NA_ROWS, row_body, 0, unroll=2)


def _na_call(p, bias, same, n_lat, n_ctx):
    b, t, _ = p.shape
    blk = NA_ROWS * GRID_W
    nb = n_lat // blk
    spec = lambda col, f: pl.BlockSpec((1, blk, GROUP_W), lambda i, j: (i, f(j), col))
    prev = lambda j: jnp.maximum(j - 1, 0)
    cur = lambda j: j
    nxt = lambda j: jnp.minimum(j + 1, nb - 1)
    ctx_spec = lambda col: pl.BlockSpec((1, n_ctx, GROUP_W), lambda i, j: (i, n_lat // n_ctx, col))
    kern = functools.partial(_na_kernel, n_rows=n_lat // GRID_W)
    return pl.pallas_call(
        kern,
        out_shape=jax.ShapeDtypeStruct((b, n_lat, GROUP_W), F32),
        grid=(b, nb),
        in_specs=[spec(C_NA_Q, cur),
                  spec(C_NA_K, prev), spec(C_NA_K, cur), spec(C_NA_K, nxt),
                  spec(C_NA_V, prev), spec(C_NA_V, cur), spec(C_NA_V, nxt),
                  ctx_spec(C_NA_K), ctx_spec(C_NA_V),
                  pl.BlockSpec(bias.shape, lambda i, j: (0, 0, 0), pipeline_mode=pl.Buffered(1)),
                  pl.BlockSpec(same.shape, lambda i, j: (0, 0))],
        out_specs=pl.BlockSpec((1, blk, GROUP_W), lambda i, j: (i, j, 0)),
        scratch_shapes=[pltpu.VMEM((3 * blk, GROUP_W), BF16), pltpu.VMEM((3 * blk, GROUP_W), BF16)],
        compiler_params=_cparams(("arbitrary", "arbitrary")),
        name="na_attn",
    )(p, p, p, p, p, p, p, p, p, bias, same)


def _na_ctx_kernel(q_ref, k_ref, v_ref, o_ref):
    hms = _head_masks()
    n = q_ref.shape[1]
    mq = _mrs(q_ref[0] * (HEAD_DIM ** -0.5), hms).astype(BF16)
    s = _nt(mq, k_ref[0].astype(BF16))
    p = jnp.exp(s - jnp.max(s, axis=-1, keepdims=True))
    o = _mm(p.astype(BF16), v_ref[0].astype(BF16)) / jnp.sum(p, axis=-1, keepdims=True)
    o_ref[0] = _rbs(o * jnp.concatenate([jnp.broadcast_to(hm, (n, GROUP_W)) for hm in hms], axis=0))


def _na_ctx_call(p, n_lat, n_ctx):
    b = p.shape[0]
    spec = lambda col: pl.BlockSpec((1, n_ctx, GROUP_W), lambda i: (i, n_lat // n_ctx, col))
    return pl.pallas_call(
        _na_ctx_kernel,
        out_shape=jax.ShapeDtypeStruct((b, n_ctx, GROUP_W), F32),
        grid=(b,),
        in_specs=[spec(C_NA_Q), spec(C_NA_K), spec(C_NA_V)],
        out_specs=pl.BlockSpec((1, n_ctx, GROUP_W), lambda i: (i, 0, 0)),
        compiler_params=_cparams(("arbitrary",)),
        name="na_ctx",
    )(p, p, p)


def _df_prep_kernel(q_ref, k_ref, v_ref, cos_ref, sin_ref, qt_ref, kr_ref, vt_ref, qn_ref, kn_ref):
    cos = cos_ref[...]
    sin = sin_ref[...]
    lane = lax.broadcasted_iota(jnp.int32, cos.shape, 1)
    first = (lane % 16) < 8

    def rope(x):
        w = x.shape[1]
        partner = jnp.where(first, pltpu.roll(x, w - 8, axis=1), pltpu.roll(x, 8, axis=1))
        return x * cos + partner * sin

    tm = q_ref.shape[1]
    qt = (rope(q_ref[0]) * (DF_HEAD_DIM ** -0.5 * LOG2E)).T
    row = lax.broadcasted_iota(jnp.int32, (GROUP_W, 1), 0)
    kr = rope(k_ref[0])
    krt = kr.T
    for s in range(2 * GROUP_HEADS):
        keep = (row // DF_HEAD_DIM == s).astype(F32)
        qt_ref[0, s] = (qt * keep).astype(BF16)
        rows = slice(s * DF_HEAD_DIM, (s + 1) * DF_HEAD_DIM)
        qn_ref[0, s] = jnp.sqrt(jnp.sum(qt[rows] * qt[rows], axis=0, keepdims=True))
        kn2 = jnp.max(jnp.sum(krt[rows] * krt[rows], axis=0, keepdims=True), axis=1, keepdims=True)
        kn_ref[0, 0, s:s + 1, :] = jnp.broadcast_to(kn2, (1, LANES))
    kr_ref[0, 0] = kr.astype(BF16)
    vt = v_ref[0].T
    ext = (lax.broadcasted_iota(jnp.int32, (2 * SUBLANES, tm), 0) == 0).astype(F32)
    for h in range(GROUP_HEADS):
        vt_ref[0, h, 0] = jnp.concatenate([vt[h * HEAD_DIM:(h + 1) * HEAD_DIM], ext], axis=0).astype(BF16)


def _df_prep_call(p, cos, sin):
    b, t, _ = p.shape
    nt = t // DF_KC
    vrows = HEAD_DIM + 2 * SUBLANES
    blk = lambda col: pl.BlockSpec((1, DF_KC, GROUP_W), lambda i, j: (i, j, col))
    return pl.pallas_call(
        _df_prep_kernel,
        out_shape=(jax.ShapeDtypeStruct((b, 2 * GROUP_HEADS, GROUP_W, t), BF16),
                   jax.ShapeDtypeStruct((b, nt, DF_KC, GROUP_W), BF16),
                   jax.ShapeDtypeStruct((b, GROUP_HEADS, nt, vrows, DF_KC), BF16),
                   jax.ShapeDtypeStruct((b, 2 * GROUP_HEADS, 1, t), F32),
                   jax.ShapeDtypeStruct((b, nt, 2 * GROUP_HEADS, LANES), F32)),
        grid=(b, nt),
        in_specs=[blk(C_DF_Q), blk(C_DF_K), blk(C_DF_V),
                  pl.BlockSpec((DF_KC, GROUP_W), lambda i, j: (j, 0)),
                  pl.BlockSpec((DF_KC, GROUP_W), lambda i, j: (j, 0))],
        out_specs=(pl.BlockSpec((1, 2 * GROUP_HEADS, GROUP_W, DF_KC), lambda i, j: (i, 0, 0, j)),
                   pl.BlockSpec((1, 1, DF_KC, GROUP_W), lambda i, j: (i, j, 0, 0)),
                   pl.BlockSpec((1, GROUP_HEADS, 1, vrows, DF_KC), lambda i, j: (i, 0, j, 0, 0)),
                   pl.BlockSpec((1, 2 * GROUP_HEADS, 1, DF_KC), lambda i, j: (i, 0, 0, j)),
                   pl.BlockSpec((1, 1, 2 * GROUP_HEADS, LANES), lambda i, j: (i, j, 0, 0))),
        compiler_params=_cparams(("arbitrary", "arbitrary")),
        name="df_prep",
    )(p, p, p, cos, sin)


def _df_attn_kernel(qt_ref, k_ref, vt_ref, lam_ref, nw_ref, o_ref,
                    m_ref, acc_ref, s_ref, cm_ref, p_ref, al_ref, *, lam_init):
    nk = k_ref.shape[1]
    m_ref[...] = jnp.full(m_ref.shape, -jnp.inf, F32)
    acc_ref[...] = jnp.zeros_like(acc_ref)

    def scores(c, slot):
        kc = k_ref[0, c]
        for j in range(2):
            s = _mm(kc, qt_ref[0, j])
            s_ref[slot, j] = s
            cm_ref[slot, j] = jnp.max(s, axis=0, keepdims=True)

    def probs(slot):
        for j in range(2):
            m_old = m_ref[j]
            m_new = jnp.maximum(m_old, cm_ref[slot, j])
            p_ref[slot, j] = jnp.exp2(s_ref[slot, j] - m_new).astype(BF16)
            al_ref[slot, j] = jnp.exp2(m_old - m_new)
            m_ref[j] = m_new

    def accum(c, slot):
        vt = vt_ref[0, 0, c]
        for j in range(2):
            acc_ref[j] = acc_ref[j] * al_ref[slot, j] + _mm(vt, p_ref[slot, j])

    def iteration(i, slot, do_scores=True, do_probs=True, do_accum=True):
        if do_accum:
            accum(i - 2, slot)
        if do_probs:
            probs(1 - slot)
        if do_scores:
            scores(i, slot)

    for i in range(min(2, nk + 2)):
        iteration(i, i % 2, i < nk, 1 <= i <= nk, False)
    n_steady = max(nk - 2, 0)

    def group(k, carry):
        for u in range(DF_UNROLL):
            iteration(2 + DF_UNROLL * k + u, u % 2)
        return carry

    n_groups = n_steady // DF_UNROLL
    lax.fori_loop(0, n_groups, group, 0)
    for i in range(2 + DF_UNROLL * n_groups, nk + 2):
        iteration(i, i % 2, i < nk, i <= nk, True)
    _df_finish(acc_ref, lam_ref, nw_ref, o_ref, lam_init)


def _df_finish(acc_ref, lam_ref, nw_ref, o_ref, lam_init):
    lp = lam_ref[...]
    lam = (jnp.exp(jnp.sum(lp[0:1] * lp[1:2], keepdims=True)) - jnp.exp(jnp.sum(lp[2:3] * lp[3:4], keepdims=True))
           + lam_init)
    a0 = acc_ref[0]
    a1 = acc_ref[1]
    o = a0[0:HEAD_DIM] / a0[HEAD_DIM:HEAD_DIM + 1] - lam * (a1[0:HEAD_DIM] / a1[HEAD_DIM:HEAD_DIM + 1])
    ms = jnp.mean(o * o, axis=0, keepdims=True)
    o_ref[0] = o * lax.rsqrt(ms + EPS) * nw_ref[...] * (1.0 - lam_init)


def _df_attn_bounded_kernel(qt_ref, k_ref, vt_ref, qn_ref, km_ref, lam_ref, nw_ref, o_ref, acc_ref, p_ref,
                            *, lam_init):
    nk = k_ref.shape[1]
    shift = [qn_ref[0, j] * km_ref[0, j, :, 0:1] for j in range(2)]
    acc_ref[...] = jnp.zeros_like(acc_ref)

    def probs(c, slot):
        kc = k_ref[0, c]
        for j in range(2):
            p_ref[slot, j] = jnp.exp2(_mm(kc, qt_ref[0, j]) - shift[j]).astype(BF16)

    def accum(c, slot):
        vt = vt_ref[0, 0, c]
        for j in range(2):
            acc_ref[j] += _mm(vt, p_ref[slot, j])

    def iteration(i, slot, do_probs=True, do_accum=True):
        if do_accum:
            accum(i - 1, 1 - slot)
        if do_probs:
            probs(i, slot)

    iteration(0, 0, True, False)

    def group(g, carry):
        for u in range(DF_UNROLL):
            iteration(1 + DF_UNROLL * g + u, (1 + u) % 2)
        return carry

    n_groups = max(nk - 1, 0) // DF_UNROLL
    lax.fori_loop(0, n_groups, group, 0)
    for i in range(1 + DF_UNROLL * n_groups, nk + 1):
        iteration(i, i % 2, i < nk, True)
    _df_finish(acc_ref, lam_ref, nw_ref, o_ref, lam_init)


def _df_attn_call(bounded, qt, kr, vt, qn, km, lam_p, nw_col, lam_init, q_off, n_q, qb, k_off, nk):
    b = qt.shape[0]
    vrows = vt.shape[3]
    grid = (b, GROUP_HEADS, n_q // qb)
    qt_spec = pl.BlockSpec((1, 2, GROUP_W, qb), lambda i, h, q: (i, h, 0, q_off // qb + q))
    k_spec = pl.BlockSpec((1, nk, DF_KC, GROUP_W), lambda i, h, q: (i, k_off // nk, 0, 0),
                          pipeline_mode=pl.Buffered(1))
    vt_spec = pl.BlockSpec((1, 1, nk, vrows, DF_KC), lambda i, h, q: (i, h, k_off // nk, 0, 0))
    small = [pl.BlockSpec(lam_p.shape, lambda i, h, q: (0, 0)), pl.BlockSpec(nw_col.shape, lambda i, h, q: (0, 0))]
    out_shape = jax.ShapeDtypeStruct((b, GROUP_W, n_q), F32)
    out_spec = pl.BlockSpec((1, HEAD_DIM, qb), lambda i, h, q: (i, h, q))
    sem = ("arbitrary", "arbitrary", "arbitrary")

    def online():
        return pl.pallas_call(
            functools.partial(_df_attn_kernel, lam_init=lam_init),
            out_shape=out_shape, grid=grid,
            in_specs=[qt_spec, k_spec, vt_spec] + small,
            out_specs=out_spec,
            scratch_shapes=[pltpu.VMEM((2, 1, qb), F32), pltpu.VMEM((2, vrows, qb), F32),
                            pltpu.VMEM((2, 2, DF_KC, qb), F32), pltpu.VMEM((2, 2, 1, qb), F32),
                            pltpu.VMEM((2, 2, DF_KC, qb), BF16), pltpu.VMEM((2, 2, 1, qb), F32)],
            compiler_params=_cparams(sem),
            name="df_attn",
        )(qt, kr, vt, lam_p, nw_col)

    def max_free():
        return pl.pallas_call(
            functools.partial(_df_attn_bounded_kernel, lam_init=lam_init),
            out_shape=out_shape, grid=grid,
            in_specs=[qt_spec, k_spec, vt_spec,
                      pl.BlockSpec((1, 2, 1, qb), lambda i, h, q: (i, h, 0, q_off // qb + q)),
                      pl.BlockSpec((1, 2, 1, LANES), lambda i, h, q: (i, h, 0, 0))] + small,
            out_specs=out_spec,
            scratch_shapes=[pltpu.VMEM((2, vrows, qb), F32), pltpu.VMEM((2, 2, DF_KC, qb), BF16)],
            compiler_params=_cparams(sem),
            name="df_attn_bounded",
        )(qt, kr, vt, qn, km, lam_p, nw_col)

    return lax.cond(bounded, max_free, online)


def _outproj_kernel(x_ref, mod_ref, dno_f, dno_b, dng_ref, na_ref, nac_ref, dft_ref, dftc_ref, hgo_f, hgo_b, hgg_ref,
                    dnw_ref, hgw_ref, g2_ref, w_ref, o_ref, *, nt_lat):
    inv = 1.0 / HEAD_DIM
    is_ctx = pl.program_id(1) == nt_lat

    def gated_norm(o, nw, gate):
        ms = _group_sum(o * o, g2_ref) * inv
        return o * lax.rsqrt(ms + EPS) * nw * _silu(gate)

    y_dn = gated_norm(dno_f[0, 0] + dno_b[0, 0], dnw_ref[...], dng_ref[0])
    y_hg = gated_norm(hgo_f[0, 0] + hgo_b[0, 0], hgw_ref[...], hgg_ref[0])
    y_df = jnp.where(is_ctx, dftc_ref[0], dft_ref[0]).T
    y_na = jnp.where(is_ctx, nac_ref[0], na_ref[0])
    acc = _mm(y_dn.astype(BF16), w_ref[0:GROUP_W, :])
    acc += _mm(y_na.astype(BF16), w_ref[GROUP_W:2 * GROUP_W, :])
    acc += _mm(y_df.astype(BF16), w_ref[2 * GROUP_W:3 * GROUP_W, :])
    acc += _mm(y_hg.astype(BF16), w_ref[3 * GROUP_W:4 * GROUP_W, :])
    o_ref[0] = x_ref[0] + mod_ref[0, 2:3, :] * acc


def _outproj_call(xall, mod, dn_o, p, y_na, y_na_ctx, y_dft, y_dft_ctx, hg_o, dnw, hgw, g2, w_bf, nt_lat, n_tiles):
    b, t, d = xall.shape
    row = lambda w, col=0: pl.BlockSpec((1, ROW_TILE, w), lambda i, j: (i, j, col))
    dirspec = lambda dd: pl.BlockSpec((1, 1, ROW_TILE, GROUP_W), lambda i, j: (i, dd, j, 0))
    const = lambda a: pl.BlockSpec(a.shape, lambda i, j: (0,) * a.ndim)
    lat = lambda j: jnp.minimum(j, nt_lat - 1)
    return pl.pallas_call(
        functools.partial(_outproj_kernel, nt_lat=nt_lat),
        out_shape=jax.ShapeDtypeStruct((b, n_tiles * ROW_TILE, d), F32),
        grid=(b, n_tiles),
        in_specs=[row(d),
                  pl.BlockSpec((1, 6, d), lambda i, j: (i * 2 + j // nt_lat, 0, 0)),
                  dirspec(0), dirspec(1), row(GROUP_W, C_DN_GATE),
                  pl.BlockSpec((1, ROW_TILE, GROUP_W), lambda i, j: (i, lat(j), 0)),
                  pl.BlockSpec((1, ROW_TILE, GROUP_W), lambda i, j: (i, 0, 0)),
                  pl.BlockSpec((1, GROUP_W, ROW_TILE), lambda i, j: (i, 0, lat(j))),
                  pl.BlockSpec((1, GROUP_W, ROW_TILE), lambda i, j: (i, 0, 0)),
                  dirspec(0), dirspec(1), row(GROUP_W, C_HG_GATE),
                  const(dnw), const(hgw), const(g2),
                  pl.BlockSpec(w_bf.shape, lambda i, j: (0, 0), pipeline_mode=pl.Buffered(1))],
        out_specs=row(d),
        compiler_params=_cparams(("arbitrary", "arbitrary")),
        name="out_proj",
    )(xall, mod, dn_o, dn_o, p, y_na, y_na_ctx, y_dft, y_dft_ctx, hg_o, hg_o, p, dnw, hgw, g2, w_bf)


def _ffn_kernel(x_ref, xp_ref, xn_ref, mod_ref, nw_ref, wup_ref, cw_ref, wdn_ref, fw_ref, o_ref,
                *, n_lat, n_all, final):
    tm = x_ref.shape[1]
    d_ff = wdn_ref.shape[0]
    fc = d_ff // FF_SPLIT
    x = x_ref[0]
    xe = jnp.concatenate([x, xp_ref[0], xn_ref[0]], axis=0)
    hb = _rms_mod(xe, nw_ref[...], mod_ref[0, 3:4, :], mod_ref[0, 4:5, :]).astype(BF16)
    gpos = pl.program_id(1) * tm + lax.broadcasted_iota(jnp.int32, (tm, 1), 0)
    acc = jnp.zeros((tm, x.shape[1]), F32)
    for f in range(FF_SPLIT):
        g_all = _mm(hb, wup_ref[:, f * fc:(f + 1) * fc])
        val = _mm(hb[0:tm], wup_ref[:, d_ff + f * fc:d_ff + (f + 1) * fc])
        g = g_all[0:tm]
        gm, gp = _shift_rows(g, g_all[tm + SUBLANES - 1:tm + SUBLANES], g_all[tm + SUBLANES:tm + SUBLANES + 1],
                             gpos, (0, n_lat), (n_lat, n_all))
        cw = cw_ref[:, f * fc:(f + 1) * fc]
        a = _silu(gm * cw[0:1] + g * cw[1:2] + gp * cw[2:3]) * val
        acc += _mm(a.astype(BF16), wdn_ref[f * fc:(f + 1) * fc, :])
    y = x + mod_ref[0, 5:6, :] * acc
    if final:
        ms = jnp.mean(y * y, axis=-1, keepdims=True)
        y = y * lax.rsqrt(ms + EPS) * fw_ref[...]
    o_ref[0] = y


def _ffn_call(xmid, mod, nw, wup_bf, cw, wdn_bf, fw, nt_lat, n_tiles, n_lat, final):
    b, t, d = xmid.shape
    out_rows = n_tiles * ROW_TILE
    prev, nxt = _halo_specs(d, 0, out_rows, ROW_TILE)
    kern = functools.partial(_ffn_kernel, n_lat=n_lat, n_all=t, final=final)
    const = lambda a, **kw: pl.BlockSpec(a.shape, lambda i, j: (0,) * a.ndim, **kw)
    return pl.pallas_call(
        kern,
        out_shape=jax.ShapeDtypeStruct((b, out_rows, d), F32),
        grid=(b, n_tiles),
        in_specs=[pl.BlockSpec((1, ROW_TILE, d), lambda i, j: (i, j, 0)), prev, nxt,
                  pl.BlockSpec((1, 6, d), lambda i, j: (i * 2 + j // nt_lat, 0, 0)),
                  const(nw), const(wup_bf, pipeline_mode=pl.Buffered(1)), const(cw),
                  const(wdn_bf, pipeline_mode=pl.Buffered(1)), const(fw)],
        out_specs=pl.BlockSpec((1, ROW_TILE, d), lambda i, j: (i, j, 0)),
        compiler_params=_cparams(("arbitrary", "arbitrary")),
        name="conv_ffn",
    )(xmid, xmid, xmid, mod, nw, wup_bf, cw, wdn_bf, fw)


def _scan_constants():
    c = CHUNK
    n = GROUP_W
    t = np.arange(c)
    tri_f = (t[None, :] <= t[:, None]).astype(np.float32)
    tri = np.stack([tri_f, tri_f[::-1, ::-1]])
    tri3 = np.concatenate([tri] * 3, axis=2)

    idx = np.arange(n)
    hh, tt = idx // c, idx % c
    same = hh[:, None] == hh[None, :]
    ti, si = tt[:, None], tt[None, :]
    dmask = np.stack([np.stack([same & (si <= ti), same & (si < ti)]),
                      np.stack([same & (si >= ti), same & (si > ti)])]).astype(np.float32)
    lv = [same & (ti // 8 == si // 8)]
    for m in (8, 16, 32):
        lv.append(same & (ti // (2 * m) == si // (2 * m)) & (ti // m != si // m))
    lvmask = np.stack(lv).astype(np.float32)

    u = t[None, :]
    ws = [tri_f, np.ones((c, c), np.float32)]
    for m in HG_LEVELS:
        r = (t // m) * m
        ws.append(((u > r[:, None]) & (u <= t[:, None])).astype(np.float32))
    for m in HG_LEVELS:
        r2 = np.minimum((t // m + 1) * m, c - 1)
        ws.append(((u > t[:, None]) & (u <= r2[:, None])).astype(np.float32))
    w_f = np.concatenate(ws, axis=0)
    w_b = np.concatenate([w[::-1, ::-1] for w in ws], axis=0)
    w3 = np.stack([np.concatenate([w_f] * 3, axis=1), np.concatenate([w_b] * 3, axis=1)])

    lms = [np.eye(c, dtype=bool)]
    for m in HG_LEVELS:
        lms.append(((t[:, None] // m) % 2 == 1) & (t[None, :] // m == t[:, None] // m - 1))
    lm_f = np.stack(lms)
    lm_b = lm_f[:, ::-1, ::-1]
    lmask = np.stack([np.tile(lm_f, (1, GROUP_HEADS, 1)), np.tile(lm_b, (1, GROUP_HEADS, 1))]).astype(np.float32)

    g = same.astype(np.float32)
    g2 = np.concatenate([g, g], axis=0)
    e = np.zeros((LANES, 4 * GROUP_W), np.float32)
    for j in range(4 * GROUP_HEADS):
        e[j, j * HEAD_DIM:(j + 1) * HEAD_DIM] = 1.0
    e3 = np.concatenate([e, e, e], axis=0)
    return dict(tri3=jnp.asarray(tri3, BF16), dmask=jnp.asarray(dmask), lvmask=jnp.asarray(lvmask),
                w3=jnp.asarray(w3, BF16), lmask=jnp.asarray(lmask), same=jnp.asarray(g),
                g2=jnp.asarray(g2, BF16), e3=jnp.asarray(e3, BF16))


def _na_bias_table(rpb):
    cols = np.arange(GRID_W)
    c_start = np.clip(cols - NA_KW // 2, 0, GRID_W - NA_KW)
    kc = np.arange(GRID_W)
    valid = (kc[None, :] >= c_start[:, None]) & (kc[None, :] < c_start[:, None] + NA_KW)
    pad = GRID_W - NA_KW
    padded = jnp.pad(rpb.astype(F32), ((0, 0), (0, 0), (pad, pad)))
    toe = jnp.stack([padded[:, :, GRID_W - 1 - q:2 * GRID_W - 1 - q] for q in range(GRID_W)], axis=1)
    toe = jnp.where(valid[None, :, None, :], toe, -jnp.inf)
    variants = [toe[:, :, var:var + NA_KH, :].reshape(GROUP_HEADS * GRID_W, NA_KH * GRID_W) for var in range(NA_KH)]
    return jnp.stack(variants)


def _rope_tables(n_lat, n_ctx):
    pos = jnp.arange(n_lat)
    row = (pos // GRID_W).astype(F32)
    col = (pos % GRID_W).astype(F32)
    half = DF_HEAD_DIM // 2
    inv = 1.0 / (ROPE_BASE ** (jnp.arange(0, half, 2, dtype=F32) / half))
    lane = np.arange(GROUP_W)
    use_row = jnp.asarray((lane % DF_HEAD_DIM) < half)
    ang = jnp.where(use_row[None, :], row[:, None], col[:, None]) * inv[lane % (half // 2)][None, :]
    sign = jnp.asarray(np.where((lane % half) < half // 2, -1.0, 1.0), F32)
    cos = jnp.concatenate([jnp.cos(ang), jnp.ones((n_ctx, GROUP_W), F32)], axis=0)
    sin = jnp.concatenate([jnp.sin(ang) * sign[None, :], jnp.zeros((n_ctx, GROUP_W), F32)], axis=0)
    return cos, sin


def _reorder_w_in(w_in_l):
    d = w_in_l.shape[0]
    a = 4 * GROUP_W
    ba = w_in_l[:, a:a + 4 * GROUP_HEADS]
    rest = w_in_l[:, a + 4 * GROUP_HEADS:]
    pad = jnp.zeros((d, LANES - 4 * GROUP_HEADS), w_in_l.dtype)
    return jnp.concatenate([w_in_l[:, :a], rest, ba, pad], axis=1)


def kernel(x, c, ctx, c_ctx, w_ada, b_ada, norm1_w, norm2_w, w_in, dn_conv_w, dn_a_log, dn_dt_bias, dn_norm_w,
           na_rpb, df_lambda, df_norm_w, hg_lb_raw, hg_norm_w, w_out, w_up, ffn_conv_w, w_down, final_norm_w):
    b, n_lat, d = x.shape
    n_ctx = ctx.shape[1]
    depth = w_ada.shape[0]
    t = n_lat + n_ctx
    assert n_ctx == ROW_TILE == DF_KC and n_lat % (NA_ROWS * GRID_W) == 0 and n_lat % DF_QB == 0
    assert b <= SUBLANES - 1
    nt_lat = n_lat // ROW_TILE
    nt_all = t // ROW_TILE

    consts = _scan_constants()
    cos, sin = _rope_tables(n_lat, n_ctx)

    c_all = jnp.zeros((SUBLANES, d), F32).at[:b].set(c).at[b].set(c_ctx)
    ada = _ada_call(c_all, w_ada.astype(BF16), b_ada).reshape(depth, SUBLANES, 6, d)

    lb_all = jnp.cumsum(jax.nn.softmax(hg_lb_raw.astype(F32), axis=0), axis=0)
    lb_all = (lb_all - lb_all[0]).reshape(depth, 2, 1, GROUP_W)

    xall = jnp.concatenate([x, ctx], axis=1)
    out = None
    for l in range(depth):
        need_ctx = l < depth - 1
        n_tiles = nt_all if need_ctx else nt_lat
        lam_init = 0.8 - 0.6 * math.exp(-0.3 * l)
        mod = jnp.stack([ada[l, :b], jnp.broadcast_to(ada[l, b], (b, 6, d))], axis=1).reshape(2 * b, 6, d)

        p = _inproj_call(xall, mod, norm1_w[l].reshape(1, d), _reorder_w_in(w_in[l]).astype(BF16), nt_lat)

        lane = np.arange(LANES)
        gate_idx = np.clip(lane - 2 * GROUP_HEADS, 0, 2 * GROUP_HEADS - 1)
        is_decay = jnp.asarray((lane >= 2 * GROUP_HEADS) & (lane < 4 * GROUP_HEADS))
        aneg = jnp.where(is_decay, -jnp.exp(dn_a_log[l].astype(F32).reshape(-1)[gate_idx]), 0.0).reshape(1, LANES)
        dtb = jnp.where(is_decay, dn_dt_bias[l].astype(F32).reshape(-1)[gate_idx], 0.0).reshape(1, LANES)
        dn_qkv, dn_aux = _dn_prep_call(p, dn_conv_w[l], aneg, dtb, consts["e3"], consts["g2"], n_lat)
        dn_o = _dn_scan_call(dn_qkv, dn_aux, consts["dmask"], consts["lvmask"], consts["tri3"], nt_lat)

        hg_o = _hg_scan_call(p, lb_all[l], consts["w3"], consts["lmask"], consts["same"], nt_lat)

        y_na = _na_call(p, _na_bias_table(na_rpb[l]), consts["same"], n_lat, n_ctx)
        y_na_ctx = _na_ctx_call(p, n_lat, n_ctx) if need_ctx else y_na

        qt, kr, vt, qn, kn2 = _df_prep_call(p, cos, sin)
        nw_col = df_norm_w[l].reshape(HEAD_DIM, 1)
        lam_p = jnp.pad(df_lambda[l].astype(F32), ((0, 4 * SUBLANES - 4), (0, LANES - DF_HEAD_DIM)))
        kmax = jnp.sqrt(jnp.max(kn2[..., 0], axis=1)) * DF_BOUND_SLACK
        bounded = jnp.all(jnp.max(qn[:, :, 0, :], axis=-1) * kmax <= DF_MAX_SHIFT)
        km = jnp.broadcast_to(kmax[:, :, None, None], (b, 2 * GROUP_HEADS, 1, LANES))
        y_dft = _df_attn_call(bounded, qt, kr, vt, qn, km, lam_p, nw_col, lam_init, 0, n_lat, DF_QB, 0, t // DF_KC)
        y_dft_ctx = (_df_attn_call(bounded, qt, kr, vt, qn, km, lam_p, nw_col, lam_init, n_lat, n_ctx, n_ctx,
                                   n_lat // DF_KC, 1) if need_ctx else y_dft)

        xmid = _outproj_call(xall, mod, dn_o, p, y_na, y_na_ctx, y_dft, y_dft_ctx, hg_o,
                             jnp.tile(dn_norm_w[l], GROUP_HEADS).reshape(1, GROUP_W),
                             jnp.tile(hg_norm_w[l], GROUP_HEADS).reshape(1, GROUP_W),
                             consts["g2"], w_out[l].astype(BF16), nt_lat, n_tiles)
        out = _ffn_call(xmid, mod, norm2_w[l].reshape(1, d), w_up[l].astype(BF16), ffn_conv_w[l],
                        w_down[l].astype(BF16), final_norm_w.reshape(1, d), nt_lat, n_tiles, n_lat,
                        final=not need_ctx)
        xall = out
    return out
```

```python
import functools
import math

import numpy as np
import jax
import jax.numpy as jnp
from jax import lax
from jax.experimental import pallas as pl
from jax.experimental.pallas import tpu as pltpu

F32 = jnp.float32
BF16 = jnp.bfloat16

GRID_W = 64
HEAD_DIM = 64
GROUP_HEADS = 4
GROUP_W = GROUP_HEADS * HEAD_DIM
CHUNK = 64
NA_KH = 8
NA_KW = 16
DF_HEAD_DIM = HEAD_DIM // 2
ROPE_BASE = 10000.0
EPS = 1e-6
LOG2E = 1.4426950408889634

LANES = 128
SUBLANES = 8
VMEM_LIMIT_BYTES = 56 * 1024 * 1024

ROW_TILE = 256
NA_ROWS = 8
DF_QB = 512
DF_KC = 256
DF_UNROLL = 8
FF_SPLIT = 2

C_DN_Q, C_DN_K, C_DN_V, C_DN_GATE = 0, 1, 2, 3
C_NA_Q, C_NA_K, C_NA_V = 4, 5, 6
C_DF_Q, C_DF_K, C_DF_V = 7, 8, 9
C_HG_Q, C_HG_I, C_HG_FF, C_HG_FB, C_HG_GATE = 10, 11, 12, 13, 14
N_COL_BLOCKS = 15
P_WIDTH = N_COL_BLOCKS * GROUP_W + LANES
C_DN_BA_128 = N_COL_BLOCKS * GROUP_W // LANES


def _cparams(sem):
    return pltpu.CompilerParams(dimension_semantics=sem, vmem_limit_bytes=VMEM_LIMIT_BYTES)


def _mm(a, b):
    return jnp.dot(a, b, preferred_element_type=F32)


def _nt(a, b):
    return lax.dot_general(a, b, (((1,), (1,)), ((), ())), preferred_element_type=F32)


def _silu(x):
    return x * jax.nn.sigmoid(x)


def _split3(x):
    hi = x.astype(BF16)
    r = x - hi.astype(F32)
    mid = r.astype(BF16)
    lo = (r - mid.astype(F32)).astype(BF16)
    return hi, mid, lo


def _split2(x):
    hi = x.astype(BF16)
    lo = (x - hi.astype(F32)).astype(BF16)
    return hi, lo


def _head_masks(n_lanes=GROUP_W):
    lane = lax.broadcasted_iota(jnp.int32, (1, n_lanes), 1)
    return [(lane // HEAD_DIM == h).astype(F32) for h in range(GROUP_HEADS)]


def _mrs(x, hms):
    return jnp.concatenate([x * hm for hm in hms], axis=0)


def _rbs(z):
    n = z.shape[0] // GROUP_HEADS
    return z[0:n] + z[n:2 * n] + z[2 * n:3 * n] + z[3 * n:4 * n]


def _tile4(x):
    return jnp.concatenate([x, x, x, x], axis=0)


def _group_sum(x, g2_ref):
    hi, lo = _split2(x)
    return _mm(jnp.concatenate([hi, lo], axis=1), g2_ref[...])


def _rms_mod(x, nw, shift, scale):
    ms = jnp.mean(x * x, axis=-1, keepdims=True)
    return (x * lax.rsqrt(ms + EPS) * nw) * (1.0 + scale) + shift


def _ada_kernel(c_ref, w_ref, b_ref, o_ref):
    s = _silu(c_ref[...])
    o_ref[0] = _mm(s.astype(BF16), w_ref[0]) + b_ref[0]


def _ada_call(c_all, w_ada_bf, b_ada):
    depth, d, d6 = w_ada_bf.shape
    n_chunks = d6 // d
    return pl.pallas_call(
        _ada_kernel,
        out_shape=jax.ShapeDtypeStruct((depth, SUBLANES, d6), F32),
        grid=(depth, n_chunks),
        in_specs=[
            pl.BlockSpec((SUBLANES, d), lambda l, j: (0, 0)),
            pl.BlockSpec((1, d, d), lambda l, j: (l, 0, j)),
            pl.BlockSpec((1, 1, d), lambda l, j: (l, 0, j)),
        ],
        out_specs=pl.BlockSpec((1, SUBLANES, d), lambda l, j: (l, 0, j)),
        compiler_params=_cparams(("arbitrary", "arbitrary")),
        name="ada_ln",
    )(c_all, w_ada_bf, b_ada.reshape(depth, 1, d6))


def _inproj_kernel(x_ref, mod_ref, nw_ref, w_ref, o_ref):
    h = _rms_mod(x_ref[0], nw_ref[...], mod_ref[0, 0:1, :], mod_ref[0, 1:2, :])
    o_ref[0] = _mm(h.astype(BF16), w_ref[...])


def _inproj_call(xall, mod, nw, w_bf, nt_lat):
    b, t, d = xall.shape
    nt = t // ROW_TILE
    pw = w_bf.shape[1]
    return pl.pallas_call(
        _inproj_kernel,
        out_shape=jax.ShapeDtypeStruct((b, t, pw), F32),
        grid=(b, nt),
        in_specs=[
            pl.BlockSpec((1, ROW_TILE, d), lambda i, j: (i, j, 0)),
            pl.BlockSpec((1, 6, d), lambda i, j: (i * 2 + j // nt_lat, 0, 0)),
            pl.BlockSpec((1, d), lambda i, j: (0, 0)),
            pl.BlockSpec((d, pw), lambda i, j: (0, 0), pipeline_mode=pl.Buffered(1)),
        ],
        out_specs=pl.BlockSpec((1, ROW_TILE, pw), lambda i, j: (i, j, 0)),
        compiler_params=_cparams(("arbitrary", "arbitrary")),
        name="in_proj",
    )(xall, mod, nw, w_bf)


def _shift_rows(x, prev_row, next_row, gpos, seg_starts, seg_ends):
    n = x.shape[0]
    row = lax.broadcasted_iota(jnp.int32, (n, 1), 0)
    xm = jnp.where(row == 0, prev_row, pltpu.roll(x, 1, axis=0))
    xp = jnp.where(row == n - 1, next_row, pltpu.roll(x, n - 1, axis=0))
    at_start = functools.reduce(jnp.logical_or, [gpos == s for s in seg_starts])
    at_end = functools.reduce(jnp.logical_or, [gpos == e - 1 for e in seg_ends])
    return jnp.where(at_start, 0.0, xm), jnp.where(at_end, 0.0, xp)


def _dn_prep_kernel(x_ref, xp_ref, xn_ref, ba_ref, cw_ref, aneg_ref, dtb_ref, e3_ref, g2_ref,
                    qkv_ref, aux_ref, *, n_lat, n_all):
    tm = x_ref.shape[1]
    x = x_ref[0]
    gpos = pl.program_id(1) * tm + lax.broadcasted_iota(jnp.int32, (tm, 1), 0)
    xm, xp = _shift_rows(x, xp_ref[0, SUBLANES - 1:SUBLANES, :], xn_ref[0, 0:1, :], gpos,
                         (0, n_lat), (n_lat, n_all))
    cw = cw_ref[...]
    y = _silu(xm * cw[0:1] + x * cw[1:2] + xp * cw[2:3])
    q, k, v = y[:, 0:GROUP_W], y[:, GROUP_W:2 * GROUP_W], y[:, 2 * GROUP_W:3 * GROUP_W]
    qn = q * lax.rsqrt(_group_sum(q * q, g2_ref) + EPS) * (HEAD_DIM ** -0.5)
    kn = k * lax.rsqrt(_group_sum(k * k, g2_ref) + EPS)
    qkv_ref[0] = jnp.concatenate([qn, kn, v], axis=1)
    ba = ba_ref[0]
    lane = lax.broadcasted_iota(jnp.int32, ba.shape, 1)
    z = ba + dtb_ref[...]
    softplus = jnp.maximum(z, 0.0) + jnp.log(1.0 + jnp.exp(-jnp.abs(z)))
    gates = jnp.where(lane < 2 * GROUP_HEADS, jax.nn.sigmoid(ba), aneg_ref[...] * softplus)
    gates = jnp.where(lane < 4 * GROUP_HEADS, gates, 0.0)
    hi, mid, lo = _split3(gates)
    aux_ref[0] = _mm(jnp.concatenate([hi, mid, lo], axis=1), e3_ref[...])


def _halo_specs(width, col, n_rows_total, tm):
    per = tm // SUBLANES
    last = n_rows_total // SUBLANES - 1
    prev = pl.BlockSpec((1, SUBLANES, width), lambda i, j: (i, jnp.maximum(j * per - 1, 0), col))
    nxt = pl.BlockSpec((1, SUBLANES, width), lambda i, j: (i, jnp.minimum((j + 1) * per, last), col))
    return prev, nxt


def _dn_prep_call(p, conv_w, aneg, dtb, e3, g2, n_lat):
    b, t, _ = p.shape
    nt = t // ROW_TILE
    w3 = 3 * GROUP_W
    prev, nxt = _halo_specs(w3, 0, t, ROW_TILE)
    kern = functools.partial(_dn_prep_kernel, n_lat=n_lat, n_all=t)
    return pl.pallas_call(
        kern,
        out_shape=(jax.ShapeDtypeStruct((b, t, w3), F32), jax.ShapeDtypeStruct((b, t, 4 * GROUP_W), F32)),
        grid=(b, nt),
        in_specs=[
            pl.BlockSpec((1, ROW_TILE, w3), lambda i, j: (i, j, 0)),
            prev, nxt,
            pl.BlockSpec((1, ROW_TILE, LANES), lambda i, j: (i, j, C_DN_BA_128)),
            pl.BlockSpec((3, w3), lambda i, j: (0, 0)),
            pl.BlockSpec((1, LANES), lambda i, j: (0, 0)),
            pl.BlockSpec((1, LANES), lambda i, j: (0, 0)),
            pl.BlockSpec(e3.shape, lambda i, j: (0, 0)),
            pl.BlockSpec(g2.shape, lambda i, j: (0, 0)),
        ],
        out_specs=(pl.BlockSpec((1, ROW_TILE, w3), lambda i, j: (i, j, 0)),
                   pl.BlockSpec((1, ROW_TILE, 4 * GROUP_W), lambda i, j: (i, j, 0))),
        compiler_params=_cparams(("arbitrary", "arbitrary")),
        name="dn_prep",
    )(p, p, p, p, conv_w, aneg, dtb, e3, g2)


def _scan_tile(d, s, nt_lat):
    return jnp.where(s == 0, nt_lat, jnp.where(d == 0, s - 1, nt_lat - s))


def _unit_tri_inverse(mats, lv_ref):
    n = mats[0].shape[0]
    eye = (lax.broadcasted_iota(jnp.int32, (n, n), 0) == lax.broadcasted_iota(jnp.int32, (n, n), 1)).astype(F32)
    n1 = [a * lv_ref[0] for a in mats]
    n1b = [x.astype(BF16) for x in n1]
    n2 = [_mm(x, x) for x in n1b]
    n2b = [x.astype(BF16) for x in n2]
    n4 = [_mm(x, x) for x in n2b]
    inner = [_mm((eye + x2).astype(BF16), (eye + x4).astype(BF16)) for x2, x4 in zip(n2, n4)]
    ts = [_mm((eye - x1).astype(BF16), y.astype(BF16)) for x1, y in zip(n1, inner)]
    for lvl in (1, 2, 3):
        tb = [t.astype(BF16) for t in ts]
        xs = [_mm((a * lv_ref[lvl]).astype(BF16), t) for a, t in zip(mats, tb)]
        ts = [t - _mm(t16, x.astype(BF16)) for t, t16, x in zip(ts, tb, xs)]
    return ts


def _dn_scan_kernel(qkv_f, qkv_b, beta_f, beta_b, la_f, la_b, dm_ref, lv_ref, tri_ref, of_ref, ob_ref, sf_ref, sb_ref):
    @pl.when(pl.program_id(1) == 0)
    def _():
        sf_ref[...] = jnp.zeros_like(sf_ref)
        sb_ref[...] = jnp.zeros_like(sb_ref)

    hms = _head_masks()
    n_chunks = qkv_f.shape[1] // CHUNK
    rows, a_mats, attn, w_in, u_in, qg, kdec_t, glast = [], [], [], [], [], [], [], []
    for d, (qkv_ref, beta_ref, la_ref) in enumerate(((qkv_f, beta_f, la_f), (qkv_b, beta_b, la_b))):
        incl = dm_ref[d, 0]
        strict = dm_ref[d, 1]
        for i in range(n_chunks):
            c = i if d == 0 else n_chunks - 1 - i
            r = slice(c * CHUNK, (c + 1) * CHUNK)
            rows.append(r)
            q = qkv_ref[0, r, 0:GROUP_W]
            k = qkv_ref[0, r, GROUP_W:2 * GROUP_W]
            v = qkv_ref[0, r, 2 * GROUP_W:3 * GROUP_W]
            beta = beta_ref[0, r, :]
            hi, mid, lo = _split3(la_ref[0, r, :])
            g = _mm(tri_ref[d], jnp.concatenate([hi, mid, lo], axis=0))
            gtot = g[CHUNK - 1:CHUNK, :] if d == 0 else g[0:1, :]
            eg = jnp.exp(g)
            rg = _tile4(g)
            decay = jnp.exp((rg - rg.T) * incl) * incl
            mk = _mrs(k, hms).astype(BF16)
            a_mats.append(strict * _tile4(beta) * _nt(mk, mk) * decay)
            attn.append((_nt(_mrs(q, hms).astype(BF16), mk) * decay).astype(BF16))
            w_in.append(_mrs(k * beta * eg, hms).astype(BF16))
            u_in.append(_mrs(v * beta, hms).astype(BF16))
            qg.append(_mrs(q * eg, hms).astype(BF16))
            kdec_t.append(_mrs(k * jnp.exp(gtot - g), hms).T.astype(BF16))
            glast.append(jnp.exp(gtot))
    tinv = [t.astype(BF16) for t in _unit_tri_inverse(a_mats, lv_ref)]
    w_bd = [_mm(t, x).astype(BF16) for t, x in zip(tinv, w_in)]
    u_bd = [_mm(t, x) for t, x in zip(tinv, u_in)]
    for i in range(n_chunks):
        for d, (s_ref, o_ref) in enumerate(((sf_ref, of_ref), (sb_ref, ob_ref))):
            n = d * n_chunks + i
            sv = s_ref[...]
            svb = sv.astype(BF16)
            v_newb = (u_bd[n] - _mm(w_bd[n], svb)).astype(BF16)
            o_ref[0, rows[n], :] = _rbs(_mm(qg[n], svb) + _mm(attn[n], v_newb))
            s_ref[...] = sv * glast[n] + _mm(kdec_t[n], v_newb)


def _dn_scan_call(qkv, aux, dmask, lvmask, tri3, nt_lat):
    b, t, w3 = qkv.shape
    ns = t // ROW_TILE
    tile = lambda d: (lambda s: _scan_tile(d, s, nt_lat))
    row = lambda w, col, d: pl.BlockSpec((1, ROW_TILE, w), lambda i, s: (i, tile(d)(s), col))
    const = lambda a: pl.BlockSpec(a.shape, lambda i, s: (0,) * a.ndim)
    out = jax.ShapeDtypeStruct((b, t, GROUP_W), F32)
    return pl.pallas_call(
        _dn_scan_kernel,
        out_shape=(out, out),
        grid=(b, ns),
        in_specs=[row(w3, 0, 0), row(w3, 0, 1), row(GROUP_W, 0, 0), row(GROUP_W, 1, 1),
                  row(GROUP_W, 2, 0), row(GROUP_W, 3, 1), const(dmask), const(lvmask), const(tri3)],
        out_specs=(row(GROUP_W, 0, 0), row(GROUP_W, 0, 1)),
        scratch_shapes=[pltpu.VMEM((GROUP_W, GROUP_W), F32), pltpu.VMEM((GROUP_W, GROUP_W), F32)],
        compiler_params=_cparams(("arbitrary", "arbitrary")),
        name="dn_scan",
    )(qkv, qkv, aux, aux, aux, aux, dmask, lvmask, tri3)


HG_LEVELS = (1, 2, 4, 8, 16, 32)


def _hg_scan_kernel(q_f, q_b, i_f, i_b, f_f, f_b, lb_ref, w3_ref, lm_ref, same_ref, of_ref, ob_ref,
                    st_ref, qb_ref, ke_ref, oi_ref):
    @pl.when(pl.program_id(1) == 0)
    def _():
        st_ref[...] = jnp.zeros_like(st_ref)

    hms = _head_masks()
    same = same_ref[...]
    tm = q_f.shape[1]
    n_chunks = tm // CHUNK
    nl = len(HG_LEVELS)
    dirs = ((q_f, i_f, f_f, of_ref), (q_b, i_b, f_b, ob_ref))
    units = [(d, c) for d in range(2) for c in range(n_chunks)]
    rows = [slice(c * CHUNK, (c + 1) * CHUNK) for _, c in units]
    qs = [dirs[d][0][0, r, :] * (HEAD_DIM ** -0.5) for (d, _), r in zip(units, rows)]
    fs = [lb_ref[d] + (1.0 - lb_ref[d]) * jax.nn.sigmoid(dirs[d][2][0, r, :]) for (d, _), r in zip(units, rows)]
    kxs = [1.0 - f for f in fs]
    es = [_mm(w3_ref[d], jnp.concatenate(_split3(jnp.log(f)), axis=0)) for (d, _), f in zip(units, fs)]
    atts = [lm_ref[d, 0] * _nt(_mrs(q, hms).astype(BF16), kx.astype(BF16)) for (d, _), q, kx in zip(units, qs, kxs)]
    for li in range(nl):
        ea = slice((2 + li) * CHUNK, (3 + li) * CHUNK)
        eb = slice((2 + nl + li) * CHUNK, (3 + nl + li) * CHUNK)
        terms = [_nt(_mrs(q * jnp.exp(e[ea]), hms).astype(BF16), (kx * jnp.exp(e[eb])).astype(BF16))
                 for q, kx, e in zip(qs, kxs, es)]
        atts = [a + lm_ref[d, 1 + li] * t for (d, _), a, t in zip(units, atts, terms)]
    tots = []
    for n, (d, c) in enumerate(units):
        cum = es[n][0:CHUNK]
        tot = es[n][CHUNK:2 * CHUNK]
        oi_ref[d, rows[n], :] = _rbs(_mm(atts[n].astype(BF16), dirs[d][1][0, rows[n], :].astype(BF16)) * same)
        qb_ref[d, rows[n], :] = qs[n] * jnp.exp(cum)
        ke_ref[d, rows[n], :] = kxs[n] * jnp.exp(tot - cum)
        tots.append(tot[0:1, :])
    vts = [dirs[d][1][0].T for d in range(2)]
    kebs = [ke_ref[d].astype(BF16) for d in range(2)]
    lane = lax.broadcasted_iota(jnp.int32, (1, tm), 1)
    for i in range(n_chunks):
        for d in range(2):
            c = i if d == 0 else n_chunks - 1 - i
            r = slice(c * CHUNK, (c + 1) * CHUNK)
            st = st_ref[d]
            dirs[d][3][0, r, :] = oi_ref[d, r, :] + _nt(qb_ref[d, r, :].astype(BF16), st.astype(BF16))
            vt_c = (vts[d] * (lane // CHUNK == c).astype(F32)).astype(BF16)
            st_ref[d] = st * jnp.exp(tots[d * n_chunks + c]) + same * _mm(vt_c, kebs[d])


def _hg_scan_call(p, lb, w3, lmask, same, nt_lat):
    b, t, _ = p.shape
    ns = t // ROW_TILE
    row = lambda col, d: pl.BlockSpec((1, ROW_TILE, GROUP_W), lambda i, s: (i, _scan_tile(d, s, nt_lat), col))
    const = lambda a: pl.BlockSpec(a.shape, lambda i, s: (0,) * a.ndim)
    out = jax.ShapeDtypeStruct((b, t, GROUP_W), F32)
    return pl.pallas_call(
        _hg_scan_kernel,
        out_shape=(out, out),
        grid=(b, ns),
        in_specs=[row(C_HG_Q, 0), row(C_HG_Q, 1), row(C_HG_I, 0), row(C_HG_I, 1), row(C_HG_FF, 0), row(C_HG_FB, 1),
                  const(lb), const(w3), const(lmask), const(same)],
        out_specs=(row(0, 0), row(0, 1)),
        scratch_shapes=[pltpu.VMEM((2, GROUP_W, GROUP_W), F32), pltpu.VMEM((2, ROW_TILE, GROUP_W), F32),
                        pltpu.VMEM((2, ROW_TILE, GROUP_W), F32), pltpu.VMEM((2, ROW_TILE, GROUP_W), F32)],
        compiler_params=_cparams(("arbitrary", "arbitrary")),
        name="hg_scan",
    )(p, p, p, p, p, p, lb, w3, lmask, same)


def _na_kernel(q_ref, kp_ref, kc_ref, kn_ref, vp_ref, vc_ref, vn_ref, kx_ref, vx_ref, bias_ref, same_ref,
               o_ref, kw_ref, vw_ref, *, n_rows):
    i = pl.program_id(1)
    blk = q_ref.shape[1]
    for j, (kr, vr) in enumerate(((kp_ref, vp_ref), (kc_ref, vc_ref), (kn_ref, vn_ref))):
        kw_ref[j * blk:(j + 1) * blk, :] = kr[0].astype(BF16)
        vw_ref[j * blk:(j + 1) * blk, :] = vr[0].astype(BF16)
    kx = kx_ref[0].astype(BF16)
    vx = vx_ref[0].astype(BF16)
    hms = _head_masks()
    same = same_ref[...]
    win = NA_KH * GRID_W

    def row_body(rr, carry):
        r = i * NA_ROWS + rr
        rs = jnp.clip(r - NA_KH // 2, 0, n_rows - NA_KH)
        start = pl.multiple_of((rs - i * NA_ROWS + NA_ROWS) * GRID_W, GRID_W)
        rows = pl.ds(pl.multiple_of(rr * GRID_W, GRID_W), GRID_W)
        mq = _mrs(q_ref[0, rows, :] * (HEAD_DIM ** -0.5), hms).astype(BF16)
        s_loc = _nt(mq, kw_ref[pl.ds(start, win), :]) + bias_ref[rs - r + NA_KH - 1]
        s_ctx = _nt(mq, kx)
        m = jnp.maximum(jnp.max(s_loc, axis=-1, keepdims=True), jnp.max(s_ctx, axis=-1, keepdims=True))
        p_loc = jnp.exp(s_loc - m)
        p_ctx = jnp.exp(s_ctx - m)
        den = jnp.sum(p_loc, axis=-1, keepdims=True) + jnp.sum(p_ctx, axis=-1, keepdims=True)
        o = (_mm(p_loc.astype(BF16), vw_ref[pl.ds(start, win), :]) + _mm(p_ctx.astype(BF16), vx)) / den
        o_ref[0, rows, :] = _rbs(o * same)
        return carry

    lax.fori_loop(0, NA_ROWS, row_body, 0, unroll=2)


def _na_call(p, bias, same, n_lat, n_ctx):
    b, t, _ = p.shape
    blk = NA_ROWS * GRID_W
    nb = n_lat // blk
    spec = lambda col, f: pl.BlockSpec((1, blk, GROUP_W), lambda i, j: (i, f(j), col))
    prev = lambda j: jnp.maximum(j - 1, 0)
    cur = lambda j: j
    nxt = lambda j: jnp.minimum(j + 1, nb - 1)
    ctx_spec = lambda col: pl.BlockSpec((1, n_ctx, GROUP_W), lambda i, j: (i, n_lat // n_ctx, col))
    kern = functools.partial(_na_kernel, n_rows=n_lat // GRID_W)
    return pl.pallas_call(
        kern,
        out_shape=jax.ShapeDtypeStruct((b, n_lat, GROUP_W), F32),
        grid=(b, nb),
        in_specs=[spec(C_NA_Q, cur),
                  spec(C_NA_K, prev), spec(C_NA_K, cur), spec(C_NA_K, nxt),
                  spec(C_NA_V, prev), spec(C_NA_V, cur), spec(C_NA_V, nxt),
                  ctx_spec(C_NA_K), ctx_spec(C_NA_V),
                  pl.BlockSpec(bias.shape, lambda i, j: (0, 0, 0), pipeline_mode=pl.Buffered(1)),
                  pl.BlockSpec(same.shape, lambda i, j: (0, 0))],
        out_specs=pl.BlockSpec((1, blk, GROUP_W), lambda i, j: (i, j, 0)),
        scratch_shapes=[pltpu.VMEM((3 * blk, GROUP_W), BF16), pltpu.VMEM((3 * blk, GROUP_W), BF16)],
        compiler_params=_cparams(("arbitrary", "arbitrary")),
        name="na_attn",
    )(p, p, p, p, p, p, p, p, p, bias, same)


def _na_ctx_kernel(q_ref, k_ref, v_ref, o_ref):
    hms = _head_masks()
    n = q_ref.shape[1]
    mq = _mrs(q_ref[0] * (HEAD_DIM ** -0.5), hms).astype(BF16)
    s = _nt(mq, k_ref[0].astype(BF16))
    p = jnp.exp(s - jnp.max(s, axis=-1, keepdims=True))
    o = _mm(p.astype(BF16), v_ref[0].astype(BF16)) / jnp.sum(p, axis=-1, keepdims=True)
    o_ref[0] = _rbs(o * jnp.concatenate([jnp.broadcast_to(hm, (n, GROUP_W)) for hm in hms], axis=0))


def _na_ctx_call(p, n_lat, n_ctx):
    b = p.shape[0]
    spec = lambda col: pl.BlockSpec((1, n_ctx, GROUP_W), lambda i: (i, n_lat // n_ctx, col))
    return pl.pallas_call(
        _na_ctx_kernel,
        out_shape=jax.ShapeDtypeStruct((b, n_ctx, GROUP_W), F32),
        grid=(b,),
        in_specs=[spec(C_NA_Q), spec(C_NA_K), spec(C_NA_V)],
        out_specs=pl.BlockSpec((1, n_ctx, GROUP_W), lambda i: (i, 0, 0)),
        compiler_params=_cparams(("arbitrary",)),
        name="na_ctx",
    )(p, p, p)


def _df_prep_kernel(q_ref, k_ref, v_ref, cos_ref, sin_ref, qt_ref, kr_ref, vt_ref):
    cos = cos_ref[...]
    sin = sin_ref[...]
    lane = lax.broadcasted_iota(jnp.int32, cos.shape, 1)
    first = (lane % 16) < 8

    def rope(x):
        w = x.shape[1]
        partner = jnp.where(first, pltpu.roll(x, w - 8, axis=1), pltpu.roll(x, 8, axis=1))
        return x * cos + partner * sin

    tm = q_ref.shape[1]
    qt = (rope(q_ref[0]) * (DF_HEAD_DIM ** -0.5 * LOG2E)).T
    row = lax.broadcasted_iota(jnp.int32, (GROUP_W, 1), 0)
    for s in range(2 * GROUP_HEADS):
        keep = (row // DF_HEAD_DIM == s).astype(F32)
        qt_ref[0, s] = (qt * keep).astype(BF16)
    kr_ref[0, 0] = rope(k_ref[0]).astype(BF16)
    vt = v_ref[0].T
    ext = (lax.broadcasted_iota(jnp.int32, (2 * SUBLANES, tm), 0) == 0).astype(F32)
    for h in range(GROUP_HEADS):
        vt_ref[0, h, 0] = jnp.concatenate([vt[h * HEAD_DIM:(h + 1) * HEAD_DIM], ext], axis=0).astype(BF16)


def _df_prep_call(p, cos, sin):
    b, t, _ = p.shape
    nt = t // DF_KC
    vrows = HEAD_DIM + 2 * SUBLANES
    blk = lambda col: pl.BlockSpec((1, DF_KC, GROUP_W), lambda i, j: (i, j, col))
    return pl.pallas_call(
        _df_prep_kernel,
        out_shape=(jax.ShapeDtypeStruct((b, 2 * GROUP_HEADS, GROUP_W, t), BF16),
                   jax.ShapeDtypeStruct((b, nt, DF_KC, GROUP_W), BF16),
                   jax.ShapeDtypeStruct((b, GROUP_HEADS, nt, vrows, DF_KC), BF16)),
        grid=(b, nt),
        in_specs=[blk(C_DF_Q), blk(C_DF_K), blk(C_DF_V),
                  pl.BlockSpec((DF_KC, GROUP_W), lambda i, j: (j, 0)),
                  pl.BlockSpec((DF_KC, GROUP_W), lambda i, j: (j, 0))],
        out_specs=(pl.BlockSpec((1, 2 * GROUP_HEADS, GROUP_W, DF_KC), lambda i, j: (i, 0, 0, j)),
                   pl.BlockSpec((1, 1, DF_KC, GROUP_W), lambda i, j: (i, j, 0, 0)),
                   pl.BlockSpec((1, GROUP_HEADS, 1, vrows, DF_KC), lambda i, j: (i, 0, j, 0, 0))),
        compiler_params=_cparams(("arbitrary", "arbitrary")),
        name="df_prep",
    )(p, p, p, cos, sin)


def _df_attn_kernel(qt_ref, k_ref, vt_ref, lam_ref, nw_ref, o_ref,
                    m_ref, acc_ref, s_ref, cm_ref, p_ref, al_ref, *, lam_init):
    nk = k_ref.shape[1]
    m_ref[...] = jnp.full(m_ref.shape, -jnp.inf, F32)
    acc_ref[...] = jnp.zeros_like(acc_ref)

    def scores(c, slot):
        kc = k_ref[0, c]
        for j in range(2):
            s = _mm(kc, qt_ref[0, j])
            s_ref[slot, j] = s
            cm_ref[slot, j] = jnp.max(s, axis=0, keepdims=True)

    def probs(slot):
        for j in range(2):
            m_old = m_ref[j]
            m_new = jnp.maximum(m_old, cm_ref[slot, j])
            p_ref[slot, j] = jnp.exp2(s_ref[slot, j] - m_new).astype(BF16)
            al_ref[slot, j] = jnp.exp2(m_old - m_new)
            m_ref[j] = m_new

    def accum(c, slot):
        vt = vt_ref[0, 0, c]
        for j in range(2):
            acc_ref[j] = acc_ref[j] * al_ref[slot, j] + _mm(vt, p_ref[slot, j])

    def iteration(i, slot, do_scores=True, do_probs=True, do_accum=True):
        if do_accum:
            accum(i - 2, slot)
        if do_probs:
            probs(1 - slot)
        if do_scores:
            scores(i, slot)

    for i in range(min(2, nk + 2)):
        iteration(i, i % 2, i < nk, 1 <= i <= nk, False)
    n_steady = max(nk - 2, 0)

    def group(k, carry):
        for u in range(DF_UNROLL):
            iteration(2 + DF_UNROLL * k + u, u % 2)
        return carry

    n_groups = n_steady // DF_UNROLL
    lax.fori_loop(0, n_groups, group, 0)
    for i in range(2 + DF_UNROLL * n_groups, nk + 2):
        iteration(i, i % 2, i < nk, i <= nk, True)
    _df_finish(acc_ref, lam_ref, nw_ref, o_ref, lam_init)


def _df_finish(acc_ref, lam_ref, nw_ref, o_ref, lam_init):
    lp = lam_ref[...]
    lam = (jnp.exp(jnp.sum(lp[0:1] * lp[1:2], keepdims=True)) - jnp.exp(jnp.sum(lp[2:3] * lp[3:4], keepdims=True))
           + lam_init)
    a0 = acc_ref[0]
    a1 = acc_ref[1]
    o = a0[0:HEAD_DIM] / a0[HEAD_DIM:HEAD_DIM + 1] - lam * (a1[0:HEAD_DIM] / a1[HEAD_DIM:HEAD_DIM + 1])
    ms = jnp.mean(o * o, axis=0, keepdims=True)
    o_ref[0] = o * lax.rsqrt(ms + EPS) * nw_ref[...] * (1.0 - lam_init)


def _df_attn_call(qt, kr, vt, lam_p, nw_col, lam_init, q_off, n_q, qb, k_off, nk):
    b = qt.shape[0]
    vrows = vt.shape[3]
    return pl.pallas_call(
        functools.partial(_df_attn_kernel, lam_init=lam_init),
        out_shape=jax.ShapeDtypeStruct((b, GROUP_W, n_q), F32),
        grid=(b, GROUP_HEADS, n_q // qb),
        in_specs=[
            pl.BlockSpec((1, 2, GROUP_W, qb), lambda i, h, q: (i, h, 0, q_off // qb + q)),
            pl.BlockSpec((1, nk, DF_KC, GROUP_W), lambda i, h, q: (i, k_off // nk, 0, 0),
                         pipeline_mode=pl.Buffered(1)),
            pl.BlockSpec((1, 1, nk, vrows, DF_KC), lambda i, h, q: (i, h, k_off // nk, 0, 0)),
            pl.BlockSpec(lam_p.shape, lambda i, h, q: (0, 0)),
            pl.BlockSpec(nw_col.shape, lambda i, h, q: (0, 0)),
        ],
        out_specs=pl.BlockSpec((1, HEAD_DIM, qb), lambda i, h, q: (i, h, q)),
        scratch_shapes=[pltpu.VMEM((2, 1, qb), F32), pltpu.VMEM((2, vrows, qb), F32),
                        pltpu.VMEM((2, 2, DF_KC, qb), F32), pltpu.VMEM((2, 2, 1, qb), F32),
                        pltpu.VMEM((2, 2, DF_KC, qb), BF16), pltpu.VMEM((2, 2, 1, qb), F32)],
        compiler_params=_cparams(("arbitrary", "arbitrary", "arbitrary")),
        name="df_attn",
    )(qt, kr, vt, lam_p, nw_col)


def _outproj_kernel(x_ref, mod_ref, dno_f, dno_b, dng_ref, na_ref, nac_ref, dft_ref, dftc_ref, hgo_f, hgo_b, hgg_ref,
                    dnw_ref, hgw_ref, g2_ref, w_ref, o_ref, *, nt_lat):
    inv = 1.0 / HEAD_DIM
    is_ctx = pl.program_id(1) == nt_lat

    def gated_norm(o, nw, gate):
        ms = _group_sum(o * o, g2_ref) * inv
        return o * lax.rsqrt(ms + EPS) * nw * _silu(gate)

    y_dn = gated_norm(dno_f[0] + dno_b[0], dnw_ref[...], dng_ref[0])
    y_hg = gated_norm(hgo_f[0] + hgo_b[0], hgw_ref[...], hgg_ref[0])
    y_df = jnp.where(is_ctx, dftc_ref[0], dft_ref[0]).T
    y_na = jnp.where(is_ctx, nac_ref[0], na_ref[0])
    acc = _mm(y_dn.astype(BF16), w_ref[0:GROUP_W, :])
    acc += _mm(y_na.astype(BF16), w_ref[GROUP_W:2 * GROUP_W, :])
    acc += _mm(y_df.astype(BF16), w_ref[2 * GROUP_W:3 * GROUP_W, :])
    acc += _mm(y_hg.astype(BF16), w_ref[3 * GROUP_W:4 * GROUP_W, :])
    o_ref[0] = x_ref[0] + mod_ref[0, 2:3, :] * acc


def _outproj_call(xall, mod, dn_o, p, y_na, y_na_ctx, y_dft, y_dft_ctx, hg_o, dnw, hgw, g2, w_bf, nt_lat, n_tiles):
    b, t, d = xall.shape
    row = lambda w, col=0: pl.BlockSpec((1, ROW_TILE, w), lambda i, j: (i, j, col))
    dirspec = lambda dd: row(GROUP_W)
    const = lambda a: pl.BlockSpec(a.shape, lambda i, j: (0,) * a.ndim)
    lat = lambda j: jnp.minimum(j, nt_lat - 1)
    return pl.pallas_call(
        functools.partial(_outproj_kernel, nt_lat=nt_lat),
        out_shape=jax.ShapeDtypeStruct((b, n_tiles * ROW_TILE, d), F32),
        grid=(b, n_tiles),
        in_specs=[row(d),
                  pl.BlockSpec((1, 6, d), lambda i, j: (i * 2 + j // nt_lat, 0, 0)),
                  dirspec(0), dirspec(1), row(GROUP_W, C_DN_GATE),
                  pl.BlockSpec((1, ROW_TILE, GROUP_W), lambda i, j: (i, lat(j), 0)),
                  pl.BlockSpec((1, ROW_TILE, GROUP_W), lambda i, j: (i, 0, 0)),
                  pl.BlockSpec((1, GROUP_W, ROW_TILE), lambda i, j: (i, 0, lat(j))),
                  pl.BlockSpec((1, GROUP_W, ROW_TILE), lambda i, j: (i, 0, 0)),
                  dirspec(0), dirspec(1), row(GROUP_W, C_HG_GATE),
                  const(dnw), const(hgw), const(g2),
                  pl.BlockSpec(w_bf.shape, lambda i, j: (0, 0), pipeline_mode=pl.Buffered(1))],
        out_specs=row(d),
        compiler_params=_cparams(("arbitrary", "arbitrary")),
        name="out_proj",
    )(xall, mod, dn_o[0], dn_o[1], p, y_na, y_na_ctx, y_dft, y_dft_ctx, hg_o[0], hg_o[1], p, dnw, hgw, g2, w_bf)


def _ffn_kernel(x_ref, xp_ref, xn_ref, mod_ref, nw_ref, wup_ref, cw_ref, wdn_ref, fw_ref, o_ref,
                *, n_lat, n_all, final):
    tm = x_ref.shape[1]
    d_ff = wdn_ref.shape[0]
    fc = d_ff // FF_SPLIT
    x = x_ref[0]
    xe = jnp.concatenate([x, xp_ref[0], xn_ref[0]], axis=0)
    hb = _rms_mod(xe, nw_ref[...], mod_ref[0, 3:4, :], mod_ref[0, 4:5, :]).astype(BF16)
    gpos = pl.program_id(1) * tm + lax.broadcasted_iota(jnp.int32, (tm, 1), 0)
    acc = jnp.zeros((tm, x.shape[1]), F32)
    for f in range(FF_SPLIT):
        g_all = _mm(hb, wup_ref[:, f * fc:(f + 1) * fc])
        val = _mm(hb[0:tm], wup_ref[:, d_ff + f * fc:d_ff + (f + 1) * fc])
        g = g_all[0:tm]
        gm, gp = _shift_rows(g, g_all[tm + SUBLANES - 1:tm + SUBLANES], g_all[tm + SUBLANES:tm + SUBLANES + 1],
                             gpos, (0, n_lat), (n_lat, n_all))
        cw = cw_ref[:, f * fc:(f + 1) * fc]
        a = _silu(gm * cw[0:1] + g * cw[1:2] + gp * cw[2:3]) * val
        acc += _mm(a.astype(BF16), wdn_ref[f * fc:(f + 1) * fc, :])
    y = x + mod_ref[0, 5:6, :] * acc
    if final:
        ms = jnp.mean(y * y, axis=-1, keepdims=True)
        y = y * lax.rsqrt(ms + EPS) * fw_ref[...]
    o_ref[0] = y


def _ffn_call(xmid, mod, nw, wup_bf, cw, wdn_bf, fw, nt_lat, n_tiles, n_lat, final):
    b, t, d = xmid.shape
    out_rows = n_tiles * ROW_TILE
    prev, nxt = _halo_specs(d, 0, out_rows, ROW_TILE)
    kern = functools.partial(_ffn_kernel, n_lat=n_lat, n_all=t, final=final)
    const = lambda a, **kw: pl.BlockSpec(a.shape, lambda i, j: (0,) * a.ndim, **kw)
    return pl.pallas_call(
        kern,
        out_shape=jax.ShapeDtypeStruct((b, out_rows, d), F32),
        grid=(b, n_tiles),
        in_specs=[pl.BlockSpec((1, ROW_TILE, d), lambda i, j: (i, j, 0)), prev, nxt,
                  pl.BlockSpec((1, 6, d), lambda i, j: (i * 2 + j // nt_lat, 0, 0)),
                  const(nw), const(wup_bf, pipeline_mode=pl.Buffered(1)), const(cw),
                  const(wdn_bf, pipeline_mode=pl.Buffered(1)), const(fw)],
        out_specs=pl.BlockSpec((1, ROW_TILE, d), lambda i, j: (i, j, 0)),
        compiler_params=_cparams(("arbitrary", "arbitrary")),
        name="conv_ffn",
    )(xmid, xmid, xmid, mod, nw, wup_bf, cw, wdn_bf, fw)


def _scan_constants():
    c = CHUNK
    n = GROUP_W
    t = np.arange(c)
    tri_f = (t[None, :] <= t[:, None]).astype(np.float32)
    tri = np.stack([tri_f, tri_f[::-1, ::-1]])
    tri3 = np.concatenate([tri] * 3, axis=2)

    idx = np.arange(n)
    hh, tt = idx // c, idx % c
    same = hh[:, None] == hh[None, :]
    ti, si = tt[:, None], tt[None, :]
    dmask = np.stack([np.stack([same & (si <= ti), same & (si < ti)]),
                      np.stack([same & (si >= ti), same & (si > ti)])]).astype(np.float32)
    lv = [same & (ti // 8 == si // 8)]
    for m in (8, 16, 32):
        lv.append(same & (ti // (2 * m) == si // (2 * m)) & (ti // m != si // m))
    lvmask = np.stack(lv).astype(np.float32)

    u = t[None, :]
    ws = [tri_f, np.ones((c, c), np.float32)]
    for m in HG_LEVELS:
        r = (t // m) * m
        ws.append(((u > r[:, None]) & (u <= t[:, None])).astype(np.float32))
    for m in HG_LEVELS:
        r2 = np.minimum((t // m + 1) * m, c - 1)
        ws.append(((u > t[:, None]) & (u <= r2[:, None])).astype(np.float32))
    w_f = np.concatenate(ws, axis=0)
    w_b = np.concatenate([w[::-1, ::-1] for w in ws], axis=0)
    w3 = np.stack([np.concatenate([w_f] * 3, axis=1), np.concatenate([w_b] * 3, axis=1)])

    lms = [np.eye(c, dtype=bool)]
    for m in HG_LEVELS:
        lms.append(((t[:, None] // m) % 2 == 1) & (t[None, :] // m == t[:, None] // m - 1))
    lm_f = np.stack(lms)
    lm_b = lm_f[:, ::-1, ::-1]
    lmask = np.stack([np.tile(lm_f, (1, GROUP_HEADS, 1)), np.tile(lm_b, (1, GROUP_HEADS, 1))]).astype(np.float32)

    g = same.astype(np.float32)
    g2 = np.concatenate([g, g], axis=0)
    e = np.zeros((LANES, 4 * GROUP_W), np.float32)
    for j in range(4 * GROUP_HEADS):
        e[j, j * HEAD_DIM:(j + 1) * HEAD_DIM] = 1.0
    e3 = np.concatenate([e, e, e], axis=0)
    return dict(tri3=jnp.asarray(tri3, BF16), dmask=jnp.asarray(dmask), lvmask=jnp.asarray(lvmask),
                w3=jnp.asarray(w3, BF16), lmask=jnp.asarray(lmask), same=jnp.asarray(g),
                g2=jnp.asarray(g2, BF16), e3=jnp.asarray(e3, BF16))


def _na_bias_table(rpb):
    cols = np.arange(GRID_W)
    c_start = np.clip(cols - NA_KW // 2, 0, GRID_W - NA_KW)
    kc = np.arange(GRID_W)
    valid = (kc[None, :] >= c_start[:, None]) & (kc[None, :] < c_start[:, None] + NA_KW)
    pad = GRID_W - NA_KW
    padded = jnp.pad(rpb.astype(F32), ((0, 0), (0, 0), (pad, pad)))
    toe = jnp.stack([padded[:, :, GRID_W - 1 - q:2 * GRID_W - 1 - q] for q in range(GRID_W)], axis=1)
    toe = jnp.where(valid[None, :, None, :], toe, -jnp.inf)
    variants = [toe[:, :, var:var + NA_KH, :].reshape(GROUP_HEADS * GRID_W, NA_KH * GRID_W) for var in range(NA_KH)]
    return jnp.stack(variants)


def _rope_tables(n_lat, n_ctx):
    pos = jnp.arange(n_lat)
    row = (pos // GRID_W).astype(F32)
    col = (pos % GRID_W).astype(F32)
    half = DF_HEAD_DIM // 2
    inv = 1.0 / (ROPE_BASE ** (jnp.arange(0, half, 2, dtype=F32) / half))
    lane = np.arange(GROUP_W)
    use_row = jnp.asarray((lane % DF_HEAD_DIM) < half)
    ang = jnp.where(use_row[None, :], row[:, None], col[:, None]) * inv[lane % (half // 2)][None, :]
    sign = jnp.asarray(np.where((lane % half) < half // 2, -1.0, 1.0), F32)
    cos = jnp.concatenate([jnp.cos(ang), jnp.ones((n_ctx, GROUP_W), F32)], axis=0)
    sin = jnp.concatenate([jnp.sin(ang) * sign[None, :], jnp.zeros((n_ctx, GROUP_W), F32)], axis=0)
    return cos, sin


def _reorder_w_in(w_in_l):
    d = w_in_l.shape[0]
    a = 4 * GROUP_W
    ba = w_in_l[:, a:a + 4 * GROUP_HEADS]
    rest = w_in_l[:, a + 4 * GROUP_HEADS:]
    pad = jnp.zeros((d, LANES - 4 * GROUP_HEADS), w_in_l.dtype)
    return jnp.concatenate([w_in_l[:, :a], rest, ba, pad], axis=1)


def kernel(x, c, ctx, c_ctx, w_ada, b_ada, norm1_w, norm2_w, w_in, dn_conv_w, dn_a_log, dn_dt_bias, dn_norm_w,
           na_rpb, df_lambda, df_norm_w, hg_lb_raw, hg_norm_w, w_out, w_up, ffn_conv_w, w_down, final_norm_w):
    b, n_lat, d = x.shape
    n_ctx = ctx.shape[1]
    depth = w_ada.shape[0]
    t = n_lat + n_ctx
    assert n_ctx == ROW_TILE == DF_KC and n_lat % (NA_ROWS * GRID_W) == 0 and n_lat % DF_QB == 0
    assert b <= SUBLANES - 1
    nt_lat = n_lat // ROW_TILE
    nt_all = t // ROW_TILE

    consts = _scan_constants()
    cos, sin = _rope_tables(n_lat, n_ctx)

    c_all = jnp.zeros((SUBLANES, d), F32).at[:b].set(c).at[b].set(c_ctx)
    ada = _ada_call(c_all, w_ada.astype(BF16), b_ada).reshape(depth, SUBLANES, 6, d)

    lb_all = jnp.cumsum(jax.nn.softmax(hg_lb_raw.astype(F32), axis=0), axis=0)
    lb_all = (lb_all - lb_all[0]).reshape(depth, 2, 1, GROUP_W)

    xall = jnp.concatenate([x, ctx], axis=1)
    out = None
    for l in range(depth):
        need_ctx = l < depth - 1
        n_tiles = nt_all if need_ctx else nt_lat
        lam_init = 0.8 - 0.6 * math.exp(-0.3 * l)
        mod = jnp.stack([ada[l, :b], jnp.broadcast_to(ada[l, b], (b, 6, d))], axis=1).reshape(2 * b, 6, d)

        p = _inproj_call(xall, mod, norm1_w[l].reshape(1, d), _reorder_w_in(w_in[l]).astype(BF16), nt_lat)

        lane = np.arange(LANES)
        gate_idx = np.clip(lane - 2 * GROUP_HEADS, 0, 2 * GROUP_HEADS - 1)
        is_decay = jnp.asarray((lane >= 2 * GROUP_HEADS) & (lane < 4 * GROUP_HEADS))
        aneg = jnp.where(is_decay, -jnp.exp(dn_a_log[l].astype(F32).reshape(-1)[gate_idx]), 0.0).reshape(1, LANES)
        dtb = jnp.where(is_decay, dn_dt_bias[l].astype(F32).reshape(-1)[gate_idx], 0.0).reshape(1, LANES)
        dn_qkv, dn_aux = _dn_prep_call(p, dn_conv_w[l], aneg, dtb, consts["e3"], consts["g2"], n_lat)
        dn_o = _dn_scan_call(dn_qkv, dn_aux, consts["dmask"], consts["lvmask"], consts["tri3"], nt_lat)

        hg_o = _hg_scan_call(p, lb_all[l], consts["w3"], consts["lmask"], consts["same"], nt_lat)

        y_na = _na_call(p, _na_bias_table(na_rpb[l]), consts["same"], n_lat, n_ctx)
        y_na_ctx = _na_ctx_call(p, n_lat, n_ctx) if need_ctx else y_na

        qt, kr, vt = _df_prep_call(p, cos, sin)
        nw_col = df_norm_w[l].reshape(HEAD_DIM, 1)
        lam_p = jnp.pad(df_lambda[l].astype(F32), ((0, 4 * SUBLANES - 4), (0, LANES - DF_HEAD_DIM)))
        y_dft = _df_attn_call(qt, kr, vt, lam_p, nw_col, lam_init, 0, n_lat, DF_QB, 0, t // DF_KC)
        y_dft_ctx = (_df_attn_call(qt, kr, vt, lam_p, nw_col, lam_init, n_lat, n_ctx, n_ctx,
                                   n_lat // DF_KC, 1) if need_ctx else y_dft)

        xmid = _outproj_call(xall, mod, dn_o, p, y_na, y_na_ctx, y_dft, y_dft_ctx, hg_o,
                             jnp.tile(dn_norm_w[l], GROUP_HEADS).reshape(1, GROUP_W),
                             jnp.tile(hg_norm_w[l], GROUP_HEADS).reshape(1, GROUP_W),
                             consts["g2"], w_out[l].astype(BF16), nt_lat, n_tiles)
        out = _ffn_call(xmid, mod, norm2_w[l].reshape(1, d), w_up[l].astype(BF16), ffn_conv_w[l],
                        w_down[l].astype(BF16), final_norm_w.reshape(1, d), nt_lat, n_tiles, n_lat,
                        final=not need_ctx)
        xall = out
    return out
```

```python
import functools
import math

import numpy as np
import jax
import jax.numpy as jnp
from jax import lax
from jax.experimental import pallas as pl
from jax.experimental.pallas import tpu as pltpu

F32 = jnp.float32
BF16 = jnp.bfloat16

GRID_W = 64
HEAD_DIM = 64
GROUP_HEADS = 4
GROUP_W = GROUP_HEADS * HEAD_DIM
CHUNK = 64
NA_KH = 8
NA_KW = 16
DF_HEAD_DIM = HEAD_DIM // 2
ROPE_BASE = 10000.0
EPS = 1e-6
LOG2E = 1.4426950408889634

LANES = 128
SUBLANES = 8
VMEM_LIMIT_BYTES = 56 * 1024 * 1024

ROW_TILE = 256
NA_ROWS = 8
DF_QB = 512
DF_KC = 256
DF_UNROLL = 16
FF_SPLIT = 2

C_DN_Q, C_DN_K, C_DN_V, C_DN_GATE = 0, 1, 2, 3
C_NA_Q, C_NA_K, C_NA_V = 4, 5, 6
C_DF_Q, C_DF_K, C_DF_V = 7, 8, 9
C_HG_Q, C_HG_I, C_HG_FF, C_HG_FB, C_HG_GATE = 10, 11, 12, 13, 14
N_COL_BLOCKS = 15
P_WIDTH = N_COL_BLOCKS * GROUP_W + LANES
C_DN_BA_128 = N_COL_BLOCKS * GROUP_W // LANES


def _cparams(sem):
    return pltpu.CompilerParams(dimension_semantics=sem, vmem_limit_bytes=VMEM_LIMIT_BYTES)


def _mm(a, b):
    return jnp.dot(a, b, preferred_element_type=F32)


def _nt(a, b):
    return lax.dot_general(a, b, (((1,), (1,)), ((), ())), preferred_element_type=F32)


def _silu(x):
    return x * jax.nn.sigmoid(x)


def _split3(x):
    hi = x.astype(BF16)
    r = x - hi.astype(F32)
    mid = r.astype(BF16)
    lo = (r - mid.astype(F32)).astype(BF16)
    return hi, mid, lo


def _split2(x):
    hi = x.astype(BF16)
    lo = (x - hi.astype(F32)).astype(BF16)
    return hi, lo


def _head_masks(n_lanes=GROUP_W):
    lane = lax.broadcasted_iota(jnp.int32, (1, n_lanes), 1)
    return [(lane // HEAD_DIM == h).astype(F32) for h in range(GROUP_HEADS)]


def _mrs(x, hms):
    return jnp.concatenate([x * hm for hm in hms], axis=0)


def _rbs(z):
    n = z.shape[0] // GROUP_HEADS
    return z[0:n] + z[n:2 * n] + z[2 * n:3 * n] + z[3 * n:4 * n]


def _tile4(x):
    return jnp.concatenate([x, x, x, x], axis=0)


def _group_sum(x, g2_ref):
    hi, lo = _split2(x)
    return _mm(jnp.concatenate([hi, lo], axis=1), g2_ref[...])


def _rms_mod(x, nw, shift, scale):
    ms = jnp.mean(x * x, axis=-1, keepdims=True)
    return (x * lax.rsqrt(ms + EPS) * nw) * (1.0 + scale) + shift


def _ada_kernel(c_ref, w_ref, b_ref, o_ref):
    s = _silu(c_ref[...])
    o_ref[0] = _mm(s.astype(BF16), w_ref[0]) + b_ref[0]


def _ada_call(c_all, w_ada_bf, b_ada):
    depth, d, d6 = w_ada_bf.shape
    n_chunks = d6 // d
    return pl.pallas_call(
        _ada_kernel,
        out_shape=jax.ShapeDtypeStruct((depth, SUBLANES, d6), F32),
        grid=(depth, n_chunks),
        in_specs=[
            pl.BlockSpec((SUBLANES, d), lambda l, j: (0, 0)),
            pl.BlockSpec((1, d, d), lambda l, j: (l, 0, j)),
            pl.BlockSpec((1, 1, d), lambda l, j: (l, 0, j)),
        ],
        out_specs=pl.BlockSpec((1, SUBLANES, d), lambda l, j: (l, 0, j)),
        compiler_params=_cparams(("arbitrary", "arbitrary")),
        name="ada_ln",
    )(c_all, w_ada_bf, b_ada.reshape(depth, 1, d6))


def _inproj_kernel(x_ref, mod_ref, nw_ref, w_ref, o_ref):
    h = _rms_mod(x_ref[0], nw_ref[...], mod_ref[0, 0:1, :], mod_ref[0, 1:2, :])
    o_ref[0] = _mm(h.astype(BF16), w_ref[...])


def _inproj_call(xall, mod, nw, w_bf, nt_lat):
    b, t, d = xall.shape
    nt = t // ROW_TILE
    pw = w_bf.shape[1]
    return pl.pallas_call(
        _inproj_kernel,
        out_shape=jax.ShapeDtypeStruct((b, t, pw), F32),
        grid=(b, nt),
        in_specs=[
            pl.BlockSpec((1, ROW_TILE, d), lambda i, j: (i, j, 0)),
            pl.BlockSpec((1, 6, d), lambda i, j: (i * 2 + j // nt_lat, 0, 0)),
            pl.BlockSpec((1, d), lambda i, j: (0, 0)),
            pl.BlockSpec((d, pw), lambda i, j: (0, 0), pipeline_mode=pl.Buffered(1)),
        ],
        out_specs=pl.BlockSpec((1, ROW_TILE, pw), lambda i, j: (i, j, 0)),
        compiler_params=_cparams(("arbitrary", "arbitrary")),
        name="in_proj",
    )(xall, mod, nw, w_bf)


def _shift_rows(x, prev_row, next_row, gpos, seg_starts, seg_ends):
    n = x.shape[0]
    row = lax.broadcasted_iota(jnp.int32, (n, 1), 0)
    xm = jnp.where(row == 0, prev_row, pltpu.roll(x, 1, axis=0))
    xp = jnp.where(row == n - 1, next_row, pltpu.roll(x, n - 1, axis=0))
    at_start = functools.reduce(jnp.logical_or, [gpos == s for s in seg_starts])
    at_end = functools.reduce(jnp.logical_or, [gpos == e - 1 for e in seg_ends])
    return jnp.where(at_start, 0.0, xm), jnp.where(at_end, 0.0, xp)


def _dn_prep_kernel(x_ref, xp_ref, xn_ref, ba_ref, cw_ref, aneg_ref, dtb_ref, e3_ref, g2_ref,
                    qkv_ref, aux_ref, *, n_lat, n_all):
    tm = x_ref.shape[1]
    x = x_ref[0]
    gpos = pl.program_id(1) * tm + lax.broadcasted_iota(jnp.int32, (tm, 1), 0)
    xm, xp = _shift_rows(x, xp_ref[0, SUBLANES - 1:SUBLANES, :], xn_ref[0, 0:1, :], gpos,
                         (0, n_lat), (n_lat, n_all))
    cw = cw_ref[...]
    y = _silu(xm * cw[0:1] + x * cw[1:2] + xp * cw[2:3])
    q, k, v = y[:, 0:GROUP_W], y[:, GROUP_W:2 * GROUP_W], y[:, 2 * GROUP_W:3 * GROUP_W]
    qn = q * lax.rsqrt(_group_sum(q * q, g2_ref) + EPS) * (HEAD_DIM ** -0.5)
    kn = k * lax.rsqrt(_group_sum(k * k, g2_ref) + EPS)
    qkv_ref[0] = jnp.concatenate([qn, kn, v], axis=1)
    ba = ba_ref[0]
    lane = lax.broadcasted_iota(jnp.int32, ba.shape, 1)
    z = ba + dtb_ref[...]
    softplus = jnp.maximum(z, 0.0) + jnp.log(1.0 + jnp.exp(-jnp.abs(z)))
    gates = jnp.where(lane < 2 * GROUP_HEADS, jax.nn.sigmoid(ba), aneg_ref[...] * softplus)
    gates = jnp.where(lane < 4 * GROUP_HEADS, gates, 0.0)
    hi, mid, lo = _split3(gates)
    aux_ref[0] = _mm(jnp.concatenate([hi, mid, lo], axis=1), e3_ref[...])


def _halo_specs(width, col, n_rows_total, tm):
    per = tm // SUBLANES
    last = n_rows_total // SUBLANES - 1
    prev = pl.BlockSpec((1, SUBLANES, width), lambda i, j: (i, jnp.maximum(j * per - 1, 0), col))
    nxt = pl.BlockSpec((1, SUBLANES, width), lambda i, j: (i, jnp.minimum((j + 1) * per, last), col))
    return prev, nxt


def _dn_prep_call(p, conv_w, aneg, dtb, e3, g2, n_lat):
    b, t, _ = p.shape
    nt = t // ROW_TILE
    w3 = 3 * GROUP_W
    prev, nxt = _halo_specs(w3, 0, t, ROW_TILE)
    kern = functools.partial(_dn_prep_kernel, n_lat=n_lat, n_all=t)
    return pl.pallas_call(
        kern,
        out_shape=(jax.ShapeDtypeStruct((b, t, w3), F32), jax.ShapeDtypeStruct((b, t, 4 * GROUP_W), F32)),
        grid=(b, nt),
        in_specs=[
            pl.BlockSpec((1, ROW_TILE, w3), lambda i, j: (i, j, 0)),
            prev, nxt,
            pl.BlockSpec((1, ROW_TILE, LANES), lambda i, j: (i, j, C_DN_BA_128)),
            pl.BlockSpec((3, w3), lambda i, j: (0, 0)),
            pl.BlockSpec((1, LANES), lambda i, j: (0, 0)),
            pl.BlockSpec((1, LANES), lambda i, j: (0, 0)),
            pl.BlockSpec(e3.shape, lambda i, j: (0, 0)),
            pl.BlockSpec(g2.shape, lambda i, j: (0, 0)),
        ],
        out_specs=(pl.BlockSpec((1, ROW_TILE, w3), lambda i, j: (i, j, 0)),
                   pl.BlockSpec((1, ROW_TILE, 4 * GROUP_W), lambda i, j: (i, j, 0))),
        compiler_params=_cparams(("arbitrary", "arbitrary")),
        name="dn_prep",
    )(p, p, p, p, conv_w, aneg, dtb, e3, g2)


def _scan_tile(d, s, nt_lat):
    return jnp.where(s == 0, nt_lat, jnp.where(d == 0, s - 1, nt_lat - s))


def _unit_tri_inverse(mats, lv_ref):
    n = mats[0].shape[0]
    eye = (lax.broadcasted_iota(jnp.int32, (n, n), 0) == lax.broadcasted_iota(jnp.int32, (n, n), 1)).astype(F32)
    n1 = [a * lv_ref[0] for a in mats]
    n1b = [x.astype(BF16) for x in n1]
    n2 = [_mm(x, x) for x in n1b]
    n2b = [x.astype(BF16) for x in n2]
    n4 = [_mm(x, x) for x in n2b]
    inner = [_mm((eye + x2).astype(BF16), (eye + x4).astype(BF16)) for x2, x4 in zip(n2, n4)]
    ts = [_mm((eye - x1).astype(BF16), y.astype(BF16)) for x1, y in zip(n1, inner)]
    for lvl in (1, 2, 3):
        tb = [t.astype(BF16) for t in ts]
        xs = [_mm((a * lv_ref[lvl]).astype(BF16), t) for a, t in zip(mats, tb)]
        ts = [t - _mm(t16, x.astype(BF16)) for t, t16, x in zip(ts, tb, xs)]
    return ts


def _dn_scan_kernel(qkv_f, qkv_b, beta_f, beta_b, la_f, la_b, dm_ref, lv_ref, tri_ref, of_ref, ob_ref, sf_ref, sb_ref):
    @pl.when(pl.program_id(1) == 0)
    def _():
        sf_ref[...] = jnp.zeros_like(sf_ref)
        sb_ref[...] = jnp.zeros_like(sb_ref)

    hms = _head_masks()
    n_chunks = qkv_f.shape[1] // CHUNK
    rows, a_mats, attn, w_in, u_in, qg, kdec_t, glast = [], [], [], [], [], [], [], []
    for d, (qkv_ref, beta_ref, la_ref) in enumerate(((qkv_f, beta_f, la_f), (qkv_b, beta_b, la_b))):
        incl = dm_ref[d, 0]
        strict = dm_ref[d, 1]
        for i in range(n_chunks):
            c = i if d == 0 else n_chunks - 1 - i
            r = slice(c * CHUNK, (c + 1) * CHUNK)
            rows.append(r)
            q = qkv_ref[0, r, 0:GROUP_W]
            k = qkv_ref[0, r, GROUP_W:2 * GROUP_W]
            v = qkv_ref[0, r, 2 * GROUP_W:3 * GROUP_W]
            beta = beta_ref[0, r, :]
            hi, mid, lo = _split3(la_ref[0, r, :])
            g = _mm(tri_ref[d], jnp.concatenate([hi, mid, lo], axis=0))
            gtot = g[CHUNK - 1:CHUNK, :] if d == 0 else g[0:1, :]
            eg = jnp.exp(g)
            rg = _tile4(g)
            decay = jnp.exp((rg - rg.T) * incl) * incl
            mk = _mrs(k, hms).astype(BF16)
            a_mats.append(strict * _tile4(beta) * _nt(mk, mk) * decay)
            attn.append((_nt(_mrs(q, hms).astype(BF16), mk) * decay).astype(BF16))
            w_in.append(_mrs(k * beta * eg, hms).astype(BF16))
            u_in.append(_mrs(v * beta, hms).astype(BF16))
            qg.append(_mrs(q * eg, hms).astype(BF16))
            kdec_t.append(_mrs(k * jnp.exp(gtot - g), hms).T.astype(BF16))
            glast.append(jnp.exp(gtot))
    tinv = [t.astype(BF16) for t in _unit_tri_inverse(a_mats, lv_ref)]
    w_bd = [_mm(t, x).astype(BF16) for t, x in zip(tinv, w_in)]
    u_bd = [_mm(t, x) for t, x in zip(tinv, u_in)]
    for i in range(n_chunks):
        for d, (s_ref, o_ref) in enumerate(((sf_ref, of_ref), (sb_ref, ob_ref))):
            n = d * n_chunks + i
            sv = s_ref[...]
            svb = sv.astype(BF16)
            v_newb = (u_bd[n] - _mm(w_bd[n], svb)).astype(BF16)
            o_ref[0, rows[n], :] = _rbs(_mm(qg[n], svb) + _mm(attn[n], v_newb))
            s_ref[...] = sv * glast[n] + _mm(kdec_t[n], v_newb)


def _dn_scan_call(qkv, aux, dmask, lvmask, tri3, nt_lat):
    b, t, w3 = qkv.shape
    ns = t // ROW_TILE
    tile = lambda d: (lambda s: _scan_tile(d, s, nt_lat))
    row = lambda w, col, d: pl.BlockSpec((1, ROW_TILE, w), lambda i, s: (i, tile(d)(s), col))
    const = lambda a: pl.BlockSpec(a.shape, lambda i, s: (0,) * a.ndim)
    out = jax.ShapeDtypeStruct((b, t, GROUP_W), F32)
    return pl.pallas_call(
        _dn_scan_kernel,
        out_shape=(out, out),
        grid=(b, ns),
        in_specs=[row(w3, 0, 0), row(w3, 0, 1), row(GROUP_W, 0, 0), row(GROUP_W, 1, 1),
                  row(GROUP_W, 2, 0), row(GROUP_W, 3, 1), const(dmask), const(lvmask), const(tri3)],
        out_specs=(row(GROUP_W, 0, 0), row(GROUP_W, 0, 1)),
        scratch_shapes=[pltpu.VMEM((GROUP_W, GROUP_W), F32), pltpu.VMEM((GROUP_W, GROUP_W), F32)],
        compiler_params=_cparams(("arbitrary", "arbitrary")),
        name="dn_scan",
    )(qkv, qkv, aux, aux, aux, aux, dmask, lvmask, tri3)


HG_LEVELS = (1, 2, 4, 8, 16, 32)


def _hg_scan_kernel(q_f, q_b, i_f, i_b, f_f, f_b, lb_ref, w3_ref, lm_ref, same_ref, of_ref, ob_ref,
                    st_ref, qb_ref, ke_ref, oi_ref):
    @pl.when(pl.program_id(1) == 0)
    def _():
        st_ref[...] = jnp.zeros_like(st_ref)

    hms = _head_masks()
    same = same_ref[...]
    tm = q_f.shape[1]
    n_chunks = tm // CHUNK
    nl = len(HG_LEVELS)
    dirs = ((q_f, i_f, f_f, of_ref), (q_b, i_b, f_b, ob_ref))
    units = [(d, c) for d in range(2) for c in range(n_chunks)]
    rows = [slice(c * CHUNK, (c + 1) * CHUNK) for _, c in units]
    qs = [dirs[d][0][0, r, :] * (HEAD_DIM ** -0.5) for (d, _), r in zip(units, rows)]
    fs = [lb_ref[d] + (1.0 - lb_ref[d]) * jax.nn.sigmoid(dirs[d][2][0, r, :]) for (d, _), r in zip(units, rows)]
    kxs = [1.0 - f for f in fs]
    es = [_mm(w3_ref[d], jnp.concatenate(_split3(jnp.log(f)), axis=0)) for (d, _), f in zip(units, fs)]
    atts = [lm_ref[d, 0] * _nt(_mrs(q, hms).astype(BF16), kx.astype(BF16)) for (d, _), q, kx in zip(units, qs, kxs)]
    for li in range(nl):
        ea = slice((2 + li) * CHUNK, (3 + li) * CHUNK)
        eb = slice((2 + nl + li) * CHUNK, (3 + nl + li) * CHUNK)
        terms = [_nt(_mrs(q * jnp.exp(e[ea]), hms).astype(BF16), (kx * jnp.exp(e[eb])).astype(BF16))
                 for q, kx, e in zip(qs, kxs, es)]
        atts = [a + lm_ref[d, 1 + li] * t for (d, _), a, t in zip(units, atts, terms)]
    tots = []
    for n, (d, c) in enumerate(units):
        cum = es[n][0:CHUNK]
        tot = es[n][CHUNK:2 * CHUNK]
        oi_ref[d, rows[n], :] = _rbs(_mm(atts[n].astype(BF16), dirs[d][1][0, rows[n], :].astype(BF16)) * same)
        qb_ref[d, rows[n], :] = qs[n] * jnp.exp(cum)
        ke_ref[d, rows[n], :] = kxs[n] * jnp.exp(tot - cum)
        tots.append(tot[0:1, :])
    vts = [dirs[d][1][0].T for d in range(2)]
    kebs = [ke_ref[d].astype(BF16) for d in range(2)]
    lane = lax.broadcasted_iota(jnp.int32, (1, tm), 1)
    for i in range(n_chunks):
        for d in range(2):
            c = i if d == 0 else n_chunks - 1 - i
            r = slice(c * CHUNK, (c + 1) * CHUNK)
            st = st_ref[d]
            dirs[d][3][0, r, :] = oi_ref[d, r, :] + _nt(qb_ref[d, r, :].astype(BF16), st.astype(BF16))
            vt_c = (vts[d] * (lane // CHUNK == c).astype(F32)).astype(BF16)
            st_ref[d] = st * jnp.exp(tots[d * n_chunks + c]) + same * _mm(vt_c, kebs[d])


def _hg_scan_call(p, lb, w3, lmask, same, nt_lat):
    b, t, _ = p.shape
    ns = t // ROW_TILE
    row = lambda col, d: pl.BlockSpec((1, ROW_TILE, GROUP_W), lambda i, s: (i, _scan_tile(d, s, nt_lat), col))
    const = lambda a: pl.BlockSpec(a.shape, lambda i, s: (0,) * a.ndim)
    out = jax.ShapeDtypeStruct((b, t, GROUP_W), F32)
    return pl.pallas_call(
        _hg_scan_kernel,
        out_shape=(out, out),
        grid=(b, ns),
        in_specs=[row(C_HG_Q, 0), row(C_HG_Q, 1), row(C_HG_I, 0), row(C_HG_I, 1), row(C_HG_FF, 0), row(C_HG_FB, 1),
                  const(lb), const(w3), const(lmask), const(same)],
        out_specs=(row(0, 0), row(0, 1)),
        scratch_shapes=[pltpu.VMEM((2, GROUP_W, GROUP_W), F32), pltpu.VMEM((2, ROW_TILE, GROUP_W), F32),
                        pltpu.VMEM((2, ROW_TILE, GROUP_W), F32), pltpu.VMEM((2, ROW_TILE, GROUP_W), F32)],
        compiler_params=_cparams(("arbitrary", "arbitrary")),
        name="hg_scan",
    )(p, p, p, p, p, p, lb, w3, lmask, same)


def _na_kernel(q_ref, kp_ref, kc_ref, kn_ref, vp_ref, vc_ref, vn_ref, kx_ref, vx_ref, bias_ref, same_ref,
               o_ref, kw_ref, vw_ref, *, n_rows):
    i = pl.program_id(1)
    blk = q_ref.shape[1]
    for j, (kr, vr) in enumerate(((kp_ref, vp_ref), (kc_ref, vc_ref), (kn_ref, vn_ref))):
        kw_ref[j * blk:(j + 1) * blk, :] = kr[0].astype(BF16)
        vw_ref[j * blk:(j + 1) * blk, :] = vr[0].astype(BF16)
    kx = kx_ref[0].astype(BF16)
    vx = vx_ref[0].astype(BF16)
    hms = _head_masks()
    same = same_ref[...]
    win = NA_KH * GRID_W

    def row_body(rr, carry):
        r = i * NA_ROWS + rr
        rs = jnp.clip(r - NA_KH // 2, 0, n_rows - NA_KH)
        start = pl.multiple_of((rs - i * NA_ROWS + NA_ROWS) * GRID_W, GRID_W)
        rows = pl.ds(pl.multiple_of(rr * GRID_W, GRID_W), GRID_W)
        mq = _mrs(q_ref[0, rows, :] * (HEAD_DIM ** -0.5), hms).astype(BF16)
        s_loc = _nt(mq, kw_ref[pl.ds(start, win), :]) + bias_ref[rs - r + NA_KH - 1]
        s_ctx = _nt(mq, kx)
        m = jnp.maximum(jnp.max(s_loc, axis=-1, keepdims=True), jnp.max(s_ctx, axis=-1, keepdims=True))
        p_loc = jnp.exp(s_loc - m)
        p_ctx = jnp.exp(s_ctx - m)
        den = jnp.sum(p_loc, axis=-1, keepdims=True) + jnp.sum(p_ctx, axis=-1, keepdims=True)
        o = (_mm(p_loc.astype(BF16), vw_ref[pl.ds(start, win), :]) + _mm(p_ctx.astype(BF16), vx)) / den
        o_ref[0, rows, :] = _rbs(o * same)
        return carry

    lax.fori_loop(0, NA_ROWS, row_body, 0, unroll=4)


def _na_call(p, bias, same, n_lat, n_ctx):
    b, t, _ = p.shape
    blk = NA_ROWS * GRID_W
    nb = n_lat // blk
    spec = lambda col, f: pl.BlockSpec((1, blk, GROUP_W), lambda i, j: (i, f(j), col))
    prev = lambda j: jnp.maximum(j - 1, 0)
    cur = lambda j: j
    nxt = lambda j: jnp.minimum(j + 1, nb - 1)
    ctx_spec = lambda col: pl.BlockSpec((1, n_ctx, GROUP_W), lambda i, j: (i, n_lat // n_ctx, col))
    kern = functools.partial(_na_kernel, n_rows=n_lat // GRID_W)
    return pl.pallas_call(
        kern,
        out_shape=jax.ShapeDtypeStruct((b, n_lat, GROUP_W), F32),
        grid=(b, nb),
        in_specs=[spec(C_NA_Q, cur),
                  spec(C_NA_K, prev), spec(C_NA_K, cur), spec(C_NA_K, nxt),
                  spec(C_NA_V, prev), spec(C_NA_V, cur), spec(C_NA_V, nxt),
                  ctx_spec(C_NA_K), ctx_spec(C_NA_V),
                  pl.BlockSpec(bias.shape, lambda i, j: (0, 0, 0), pipeline_mode=pl.Buffered(1)),
                  pl.BlockSpec(same.shape, lambda i, j: (0, 0))],
        out_specs=pl.BlockSpec((1, blk, GROUP_W), lambda i, j: (i, j, 0)),
        scratch_shapes=[pltpu.VMEM((3 * blk, GROUP_W), BF16), pltpu.VMEM((3 * blk, GROUP_W), BF16)],
        compiler_params=_cparams(("arbitrary", "arbitrary")),
        name="na_attn",
    )(p, p, p, p, p, p, p, p, p, bias, same)


def _na_ctx_kernel(q_ref, k_ref, v_ref, o_ref):
    hms = _head_masks()
    n = q_ref.shape[1]
    mq = _mrs(q_ref[0] * (HEAD_DIM ** -0.5), hms).astype(BF16)
    s = _nt(mq, k_ref[0].astype(BF16))
    p = jnp.exp(s - jnp.max(s, axis=-1, keepdims=True))
    o = _mm(p.astype(BF16), v_ref[0].astype(BF16)) / jnp.sum(p, axis=-1, keepdims=True)
    o_ref[0] = _rbs(o * jnp.concatenate([jnp.broadcast_to(hm, (n, GROUP_W)) for hm in hms], axis=0))


def _na_ctx_call(p, n_lat, n_ctx):
    b = p.shape[0]
    spec = lambda col: pl.BlockSpec((1, n_ctx, GROUP_W), lambda i: (i, n_lat // n_ctx, col))
    return pl.pallas_call(
        _na_ctx_kernel,
        out_shape=jax.ShapeDtypeStruct((b, n_ctx, GROUP_W), F32),
        grid=(b,),
        in_specs=[spec(C_NA_Q), spec(C_NA_K), spec(C_NA_V)],
        out_specs=pl.BlockSpec((1, n_ctx, GROUP_W), lambda i: (i, 0, 0)),
        compiler_params=_cparams(("arbitrary",)),
        name="na_ctx",
    )(p, p, p)


def _df_prep_kernel(q_ref, k_ref, v_ref, cos_ref, sin_ref, qt_ref, kr_ref, vt_ref):
    cos = cos_ref[...]
    sin = sin_ref[...]
    lane = lax.broadcasted_iota(jnp.int32, cos.shape, 1)
    first = (lane % 16) < 8

    def rope(x):
        w = x.shape[1]
        partner = jnp.where(first, pltpu.roll(x, w - 8, axis=1), pltpu.roll(x, 8, axis=1))
        return x * cos + partner * sin

    tm = q_ref.shape[1]
    qt = (rope(q_ref[0]) * (DF_HEAD_DIM ** -0.5 * LOG2E)).T
    row = lax.broadcasted_iota(jnp.int32, (GROUP_W, 1), 0)
    for s in range(2 * GROUP_HEADS):
        keep = (row // DF_HEAD_DIM == s).astype(F32)
        qt_ref[0, s] = (qt * keep).astype(BF16)
    kr_ref[0, 0] = rope(k_ref[0]).astype(BF16)
    vt = v_ref[0].T
    ext = (lax.broadcasted_iota(jnp.int32, (2 * SUBLANES, tm), 0) == 0).astype(F32)
    for h in range(GROUP_HEADS):
        vt_ref[0, h, 0] = jnp.concatenate([vt[h * HEAD_DIM:(h + 1) * HEAD_DIM], ext], axis=0).astype(BF16)


def _df_prep_call(p, cos, sin):
    b, t, _ = p.shape
    nt = t // DF_KC
    vrows = HEAD_DIM + 2 * SUBLANES
    blk = lambda col: pl.BlockSpec((1, DF_KC, GROUP_W), lambda i, j: (i, j, col))
    return pl.pallas_call(
        _df_prep_kernel,
        out_shape=(jax.ShapeDtypeStruct((b, 2 * GROUP_HEADS, GROUP_W, t), BF16),
                   jax.ShapeDtypeStruct((b, nt, DF_KC, GROUP_W), BF16),
                   jax.ShapeDtypeStruct((b, GROUP_HEADS, nt, vrows, DF_KC), BF16)),
        grid=(b, nt),
        in_specs=[blk(C_DF_Q), blk(C_DF_K), blk(C_DF_V),
                  pl.BlockSpec((DF_KC, GROUP_W), lambda i, j: (j, 0)),
                  pl.BlockSpec((DF_KC, GROUP_W), lambda i, j: (j, 0))],
        out_specs=(pl.BlockSpec((1, 2 * GROUP_HEADS, GROUP_W, DF_KC), lambda i, j: (i, 0, 0, j)),
                   pl.BlockSpec((1, 1, DF_KC, GROUP_W), lambda i, j: (i, j, 0, 0)),
                   pl.BlockSpec((1, GROUP_HEADS, 1, vrows, DF_KC), lambda i, j: (i, 0, j, 0, 0))),
        compiler_params=_cparams(("arbitrary", "arbitrary")),
        name="df_prep",
    )(p, p, p, cos, sin)


def _df_attn_kernel(qt_ref, k_ref, vt_ref, lam_ref, nw_ref, o_ref,
                    m_ref, acc_ref, s_ref, p_ref, al_ref, *, lam_init):
    nk = k_ref.shape[1]
    m_ref[...] = jnp.full(m_ref.shape, -jnp.inf, F32)
    acc_ref[...] = jnp.zeros_like(acc_ref)

    def scores(c, slot):
        kc = k_ref[0, c]
        for j in range(2):
            s_ref[slot, j] = _mm(kc, qt_ref[0, j])

    def probs(slot):
        for j in range(2):
            m_old = m_ref[j]
            s = s_ref[slot, j]
            m_new = jnp.maximum(m_old, jnp.max(s, axis=0, keepdims=True))
            p_ref[slot, j] = jnp.exp2(s - m_new).astype(BF16)
            al_ref[slot, j] = jnp.exp2(m_old - m_new)
            m_ref[j] = m_new

    def accum(c, slot):
        vt = vt_ref[0, 0, c]
        for j in range(2):
            acc_ref[j] = acc_ref[j] * al_ref[slot, j] + _mm(vt, p_ref[slot, j])

    def iteration(i, slot, do_scores=True, do_probs=True, do_accum=True):
        if do_accum:
            accum(i - 2, slot)
        if do_probs:
            probs(1 - slot)
        if do_scores:
            scores(i, slot)

    for i in range(min(2, nk + 2)):
        iteration(i, i % 2, i < nk, 1 <= i <= nk, False)
    n_steady = max(nk - 2, 0)

    def group(k, carry):
        for u in range(DF_UNROLL):
            iteration(2 + DF_UNROLL * k + u, u % 2)
        return carry

    n_groups = n_steady // DF_UNROLL
    lax.fori_loop(0, n_groups, group, 0)
    for i in range(2 + DF_UNROLL * n_groups, nk + 2):
        iteration(i, i % 2, i < nk, i <= nk, True)
    _df_finish(acc_ref, lam_ref, nw_ref, o_ref, lam_init)


def _df_finish(acc_ref, lam_ref, nw_ref, o_ref, lam_init):
    lp = lam_ref[...]
    lam = (jnp.exp(jnp.sum(lp[0:1] * lp[1:2], keepdims=True)) - jnp.exp(jnp.sum(lp[2:3] * lp[3:4], keepdims=True))
           + lam_init)
    a0 = acc_ref[0]
    a1 = acc_ref[1]
    o = a0[0:HEAD_DIM] / a0[HEAD_DIM:HEAD_DIM + 1] - lam * (a1[0:HEAD_DIM] / a1[HEAD_DIM:HEAD_DIM + 1])
    ms = jnp.mean(o * o, axis=0, keepdims=True)
    o_ref[0] = o * lax.rsqrt(ms + EPS) * nw_ref[...] * (1.0 - lam_init)


def _df_attn_call(qt, kr, vt, lam_p, nw_col, lam_init, q_off, n_q, qb, k_off, nk):
    b = qt.shape[0]
    vrows = vt.shape[3]
    return pl.pallas_call(
        functools.partial(_df_attn_kernel, lam_init=lam_init),
        out_shape=jax.ShapeDtypeStruct((b, GROUP_W, n_q), F32),
        grid=(b, GROUP_HEADS, n_q // qb),
        in_specs=[
            pl.BlockSpec((1, 2, GROUP_W, qb), lambda i, h, q: (i, h, 0, q_off // qb + q)),
            pl.BlockSpec((1, nk, DF_KC, GROUP_W), lambda i, h, q: (i, k_off // nk, 0, 0),
                         pipeline_mode=pl.Buffered(1)),
            pl.BlockSpec((1, 1, nk, vrows, DF_KC), lambda i, h, q: (i, h, k_off // nk, 0, 0)),
            pl.BlockSpec(lam_p.shape, lambda i, h, q: (0, 0)),
            pl.BlockSpec(nw_col.shape, lambda i, h, q: (0, 0)),
        ],
        out_specs=pl.BlockSpec((1, HEAD_DIM, qb), lambda i, h, q: (i, h, q)),
        scratch_shapes=[pltpu.VMEM((2, 1, qb), F32), pltpu.VMEM((2, vrows, qb), F32),
                        pltpu.VMEM((2, 2, DF_KC, qb), F32),
                        pltpu.VMEM((2, 2, DF_KC, qb), BF16), pltpu.VMEM((2, 2, 1, qb), F32)],
        compiler_params=_cparams(("arbitrary", "arbitrary", "arbitrary")),
        name="df_attn",
    )(qt, kr, vt, lam_p, nw_col)


def _outproj_kernel(x_ref, mod_ref, dno_f, dno_b, dng_ref, na_ref, nac_ref, dft_ref, dftc_ref, hgo_f, hgo_b, hgg_ref,
                    dnw_ref, hgw_ref, g2_ref, w_ref, o_ref, *, nt_lat):
    inv = 1.0 / HEAD_DIM
    is_ctx = pl.program_id(1) == nt_lat

    def gated_norm(o, nw, gate):
        ms = _group_sum(o * o, g2_ref) * inv
        return o * lax.rsqrt(ms + EPS) * nw * _silu(gate)

    y_dn = gated_norm(dno_f[0] + dno_b[0], dnw_ref[...], dng_ref[0])
    y_hg = gated_norm(hgo_f[0] + hgo_b[0], hgw_ref[...], hgg_ref[0])
    y_df = jnp.where(is_ctx, dftc_ref[0], dft_ref[0]).T
    y_na = jnp.where(is_ctx, nac_ref[0], na_ref[0])
    acc = _mm(y_dn.astype(BF16), w_ref[0:GROUP_W, :])
    acc += _mm(y_na.astype(BF16), w_ref[GROUP_W:2 * GROUP_W, :])
    acc += _mm(y_df.astype(BF16), w_ref[2 * GROUP_W:3 * GROUP_W, :])
    acc += _mm(y_hg.astype(BF16), w_ref[3 * GROUP_W:4 * GROUP_W, :])
    o_ref[0] = x_ref[0] + mod_ref[0, 2:3, :] * acc


def _outproj_call(xall, mod, dn_o, p, y_na, y_na_ctx, y_dft, y_dft_ctx, hg_o, dnw, hgw, g2, w_bf, nt_lat, n_tiles):
    b, t, d = xall.shape
    row = lambda w, col=0: pl.BlockSpec((1, ROW_TILE, w), lambda i, j: (i, j, col))
    dirspec = lambda dd: row(GROUP_W)
    const = lambda a: pl.BlockSpec(a.shape, lambda i, j: (0,) * a.ndim)
    lat = lambda j: jnp.minimum(j, nt_lat - 1)
    return pl.pallas_call(
        functools.partial(_outproj_kernel, nt_lat=nt_lat),
        out_shape=jax.ShapeDtypeStruct((b, n_tiles * ROW_TILE, d), F32),
        grid=(b, n_tiles),
        in_specs=[row(d),
                  pl.BlockSpec((1, 6, d), lambda i, j: (i * 2 + j // nt_lat, 0, 0)),
                  dirspec(0), dirspec(1), row(GROUP_W, C_DN_GATE),
                  pl.BlockSpec((1, ROW_TILE, GROUP_W), lambda i, j: (i, lat(j), 0)),
                  pl.BlockSpec((1, ROW_TILE, GROUP_W), lambda i, j: (i, 0, 0)),
                  pl.BlockSpec((1, GROUP_W, ROW_TILE), lambda i, j: (i, 0, lat(j))),
                  pl.BlockSpec((1, GROUP_W, ROW_TILE), lambda i, j: (i, 0, 0)),
                  dirspec(0), dirspec(1), row(GROUP_W, C_HG_GATE),
                  const(dnw), const(hgw), const(g2),
                  pl.BlockSpec(w_bf.shape, lambda i, j: (0, 0), pipeline_mode=pl.Buffered(1))],
        out_specs=row(d),
        compiler_params=_cparams(("arbitrary", "arbitrary")),
        name="out_proj",
    )(xall, mod, dn_o[0], dn_o[1], p, y_na, y_na_ctx, y_dft, y_dft_ctx, hg_o[0], hg_o[1], p, dnw, hgw, g2, w_bf)


def _ffn_kernel(x_ref, xp_ref, xn_ref, mod_ref, nw_ref, wup_ref, cw_ref, wdn_ref, fw_ref, o_ref,
                *, n_lat, n_all, final):
    tm = x_ref.shape[1]
    d_ff = wdn_ref.shape[0]
    fc = d_ff // FF_SPLIT
    x = x_ref[0]
    xe = jnp.concatenate([x, xp_ref[0], xn_ref[0]], axis=0)
    hb = _rms_mod(xe, nw_ref[...], mod_ref[0, 3:4, :], mod_ref[0, 4:5, :]).astype(BF16)
    gpos = pl.program_id(1) * tm + lax.broadcasted_iota(jnp.int32, (tm, 1), 0)
    acc = jnp.zeros((tm, x.shape[1]), F32)
    for f in range(FF_SPLIT):
        g_all = _mm(hb, wup_ref[:, f * fc:(f + 1) * fc])
        val = _mm(hb[0:tm], wup_ref[:, d_ff + f * fc:d_ff + (f + 1) * fc])
        g = g_all[0:tm]
        gm, gp = _shift_rows(g, g_all[tm + SUBLANES - 1:tm + SUBLANES], g_all[tm + SUBLANES:tm + SUBLANES + 1],
                             gpos, (0, n_lat), (n_lat, n_all))
        cw = cw_ref[:, f * fc:(f + 1) * fc]
        a = _silu(gm * cw[0:1] + g * cw[1:2] + gp * cw[2:3]) * val
        acc += _mm(a.astype(BF16), wdn_ref[f * fc:(f + 1) * fc, :])
    y = x + mod_ref[0, 5:6, :] * acc
    if final:
        ms = jnp.mean(y * y, axis=-1, keepdims=True)
        y = y * lax.rsqrt(ms + EPS) * fw_ref[...]
    o_ref[0] = y


def _ffn_call(xmid, mod, nw, wup_bf, cw, wdn_bf, fw, nt_lat, n_tiles, n_lat, final):
    b, t, d = xmid.shape
    out_rows = n_tiles * ROW_TILE
    prev, nxt = _halo_specs(d, 0, out_rows, ROW_TILE)
    kern = functools.partial(_ffn_kernel, n_lat=n_lat, n_all=t, final=final)
    const = lambda a, **kw: pl.BlockSpec(a.shape, lambda i, j: (0,) * a.ndim, **kw)
    return pl.pallas_call(
        kern,
        out_shape=jax.ShapeDtypeStruct((b, out_rows, d), F32),
        grid=(b, n_tiles),
        in_specs=[pl.BlockSpec((1, ROW_TILE, d), lambda i, j: (i, j, 0)), prev, nxt,
                  pl.BlockSpec((1, 6, d), lambda i, j: (i * 2 + j // nt_lat, 0, 0)),
                  const(nw), const(wup_bf, pipeline_mode=pl.Buffered(1)), const(cw),
                  const(wdn_bf, pipeline_mode=pl.Buffered(1)), const(fw)],
        out_specs=pl.BlockSpec((1, ROW_TILE, d), lambda i, j: (i, j, 0)),
        compiler_params=_cparams(("arbitrary", "arbitrary")),
        name="conv_ffn",
    )(xmid, xmid, xmid, mod, nw, wup_bf, cw, wdn_bf, fw)


def _scan_constants():
    c = CHUNK
    n = GROUP_W
    t = np.arange(c)
    tri_f = (t[None, :] <= t[:, None]).astype(np.float32)
    tri = np.stack([tri_f, tri_f[::-1, ::-1]])
    tri3 = np.concatenate([tri] * 3, axis=2)

    idx = np.arange(n)
    hh, tt = idx // c, idx % c
    same = hh[:, None] == hh[None, :]
    ti, si = tt[:, None], tt[None, :]
    dmask = np.stack([np.stack([same & (si <= ti), same & (si < ti)]),
                      np.stack([same & (si >= ti), same & (si > ti)])]).astype(np.float32)
    lv = [same & (ti // 8 == si // 8)]
    for m in (8, 16, 32):
        lv.append(same & (ti // (2 * m) == si // (2 * m)) & (ti // m != si // m))
    lvmask = np.stack(lv).astype(np.float32)

    u = t[None, :]
    ws = [tri_f, np.ones((c, c), np.float32)]
    for m in HG_LEVELS:
        r = (t // m) * m
        ws.append(((u > r[:, None]) & (u <= t[:, None])).astype(np.float32))
    for m in HG_LEVELS:
        r2 = np.minimum((t // m + 1) * m, c - 1)
        ws.append(((u > t[:, None]) & (u <= r2[:, None])).astype(np.float32))
    w_f = np.concatenate(ws, axis=0)
    w_b = np.concatenate([w[::-1, ::-1] for w in ws], axis=0)
    w3 = np.stack([np.concatenate([w_f] * 3, axis=1), np.concatenate([w_b] * 3, axis=1)])

    lms = [np.eye(c, dtype=bool)]
    for m in HG_LEVELS:
        lms.append(((t[:, None] // m) % 2 == 1) & (t[None, :] // m == t[:, None] // m - 1))
    lm_f = np.stack(lms)
    lm_b = lm_f[:, ::-1, ::-1]
    lmask = np.stack([np.tile(lm_f, (1, GROUP_HEADS, 1)), np.tile(lm_b, (1, GROUP_HEADS, 1))]).astype(np.float32)

    g = same.astype(np.float32)
    g2 = np.concatenate([g, g], axis=0)
    e = np.zeros((LANES, 4 * GROUP_W), np.float32)
    for j in range(4 * GROUP_HEADS):
        e[j, j * HEAD_DIM:(j + 1) * HEAD_DIM] = 1.0
    e3 = np.concatenate([e, e, e], axis=0)
    return dict(tri3=jnp.asarray(tri3, BF16), dmask=jnp.asarray(dmask), lvmask=jnp.asarray(lvmask),
                w3=jnp.asarray(w3, BF16), lmask=jnp.asarray(lmask), same=jnp.asarray(g),
                g2=jnp.asarray(g2, BF16), e3=jnp.asarray(e3, BF16))


def _na_bias_table(rpb):
    cols = np.arange(GRID_W)
    c_start = np.clip(cols - NA_KW // 2, 0, GRID_W - NA_KW)
    kc = np.arange(GRID_W)
    valid = (kc[None, :] >= c_start[:, None]) & (kc[None, :] < c_start[:, None] + NA_KW)
    pad = GRID_W - NA_KW
    padded = jnp.pad(rpb.astype(F32), ((0, 0), (0, 0), (pad, pad)))
    toe = jnp.stack([padded[:, :, GRID_W - 1 - q:2 * GRID_W - 1 - q] for q in range(GRID_W)], axis=1)
    toe = jnp.where(valid[None, :, None, :], toe, -jnp.inf)
    variants = [toe[:, :, var:var + NA_KH, :].reshape(GROUP_HEADS * GRID_W, NA_KH * GRID_W) for var in range(NA_KH)]
    return jnp.stack(variants)


def _rope_tables(n_lat, n_ctx):
    pos = jnp.arange(n_lat)
    row = (pos // GRID_W).astype(F32)
    col = (pos % GRID_W).astype(F32)
    half = DF_HEAD_DIM // 2
    inv = 1.0 / (ROPE_BASE ** (jnp.arange(0, half, 2, dtype=F32) / half))
    lane = np.arange(GROUP_W)
    use_row = jnp.asarray((lane % DF_HEAD_DIM) < half)
    ang = jnp.where(use_row[None, :], row[:, None], col[:, None]) * inv[lane % (half // 2)][None, :]
    sign = jnp.asarray(np.where((lane % half) < half // 2, -1.0, 1.0), F32)
    cos = jnp.concatenate([jnp.cos(ang), jnp.ones((n_ctx, GROUP_W), F32)], axis=0)
    sin = jnp.concatenate([jnp.sin(ang) * sign[None, :], jnp.zeros((n_ctx, GROUP_W), F32)], axis=0)
    return cos, sin


def _reorder_w_in(w_in_l):
    d = w_in_l.shape[0]
    a = 4 * GROUP_W
    ba = w_in_l[:, a:a + 4 * GROUP_HEADS]
    rest = w_in_l[:, a + 4 * GROUP_HEADS:]
    pad = jnp.zeros((d, LANES - 4 * GROUP_HEADS), w_in_l.dtype)
    return jnp.concatenate([w_in_l[:, :a], rest, ba, pad], axis=1)


def kernel(x, c, ctx, c_ctx, w_ada, b_ada, norm1_w, norm2_w, w_in, dn_conv_w, dn_a_log, dn_dt_bias, dn_norm_w,
           na_rpb, df_lambda, df_norm_w, hg_lb_raw, hg_norm_w, w_out, w_up, ffn_conv_w, w_down, final_norm_w):
    b, n_lat, d = x.shape
    n_ctx = ctx.shape[1]
    depth = w_ada.shape[0]
    t = n_lat + n_ctx
    assert n_ctx == ROW_TILE == DF_KC and n_lat % (NA_ROWS * GRID_W) == 0 and n_lat % DF_QB == 0
    assert b <= SUBLANES - 1
    nt_lat = n_lat // ROW_TILE
    nt_all = t // ROW_TILE

    consts = _scan_constants()
    cos, sin = _rope_tables(n_lat, n_ctx)

    c_all = jnp.zeros((SUBLANES, d), F32).at[:b].set(c).at[b].set(c_ctx)
    ada = _ada_call(c_all, w_ada.astype(BF16), b_ada).reshape(depth, SUBLANES, 6, d)

    lb_all = jnp.cumsum(jax.nn.softmax(hg_lb_raw.astype(F32), axis=0), axis=0)
    lb_all = (lb_all - lb_all[0]).reshape(depth, 2, 1, GROUP_W)

    xall = jnp.concatenate([x, ctx], axis=1)
    out = None
    for l in range(depth):
        need_ctx = l < depth - 1
        n_tiles = nt_all if need_ctx else nt_lat
        lam_init = 0.8 - 0.6 * math.exp(-0.3 * l)
        mod = jnp.stack([ada[l, :b], jnp.broadcast_to(ada[l, b], (b, 6, d))], axis=1).reshape(2 * b, 6, d)

        p = _inproj_call(xall, mod, norm1_w[l].reshape(1, d), _reorder_w_in(w_in[l]).astype(BF16), nt_lat)

        lane = np.arange(LANES)
        gate_idx = np.clip(lane - 2 * GROUP_HEADS, 0, 2 * GROUP_HEADS - 1)
        is_decay = jnp.asarray((lane >= 2 * GROUP_HEADS) & (lane < 4 * GROUP_HEADS))
        aneg = jnp.where(is_decay, -jnp.exp(dn_a_log[l].astype(F32).reshape(-1)[gate_idx]), 0.0).reshape(1, LANES)
        dtb = jnp.where(is_decay, dn_dt_bias[l].astype(F32).reshape(-1)[gate_idx], 0.0).reshape(1, LANES)
        dn_qkv, dn_aux = _dn_prep_call(p, dn_conv_w[l], aneg, dtb, consts["e3"], consts["g2"], n_lat)
        dn_o = _dn_scan_call(dn_qkv, dn_aux, consts["dmask"], consts["lvmask"], consts["tri3"], nt_lat)

        hg_o = _hg_scan_call(p, lb_all[l], consts["w3"], consts["lmask"], consts["same"], nt_lat)

        y_na = _na_call(p, _na_bias_table(na_rpb[l]), consts["same"], n_lat, n_ctx)
        y_na_ctx = _na_ctx_call(p, n_lat, n_ctx) if need_ctx else y_na

        qt, kr, vt = _df_prep_call(p, cos, sin)
        nw_col = df_norm_w[l].reshape(HEAD_DIM, 1)
        lam_p = jnp.pad(df_lambda[l].astype(F32), ((0, 4 * SUBLANES - 4), (0, LANES - DF_HEAD_DIM)))
        y_dft = _df_attn_call(qt, kr, vt, lam_p, nw_col, lam_init, 0, n_lat, DF_QB, 0, t // DF_KC)
        y_dft_ctx = (_df_attn_call(qt, kr, vt, lam_p, nw_col, lam_init, n_lat, n_ctx, n_ctx,
                                   n_lat // DF_KC, 1) if need_ctx else y_dft)

        xmid = _outproj_call(xall, mod, dn_o, p, y_na, y_na_ctx, y_dft, y_dft_ctx, hg_o,
                             jnp.tile(dn_norm_w[l], GROUP_HEADS).reshape(1, GROUP_W),
                             jnp.tile(hg_norm_w[l], GROUP_HEADS).reshape(1, GROUP_W),
                             consts["g2"], w_out[l].astype(BF16), nt_lat, n_tiles)
        out = _ffn_call(xmid, mod, norm2_w[l].reshape(1, d), w_up[l].astype(BF16), ffn_conv_w[l],
                        w_down[l].astype(BF16), final_norm_w.reshape(1, d), nt_lat, n_tiles, n_lat,
                        final=not need_ctx)
        xall = out
    return out
```

```python
import functools
import math

import numpy as np
import jax
import jax.numpy as jnp
from jax import lax
from jax.experimental import pallas as pl
from jax.experimental.pallas import tpu as pltpu

F32 = jnp.float32
BF16 = jnp.bfloat16

GRID_W = 64
HEAD_DIM = 64
GROUP_HEADS = 4
GROUP_W = GROUP_HEADS * HEAD_DIM
CHUNK = 64
NA_KH = 8
NA_KW = 16
DF_HEAD_DIM = HEAD_DIM // 2
ROPE_BASE = 10000.0
EPS = 1e-6
LOG2E = 1.4426950408889634

LANES = 128
SUBLANES = 8
VMEM_LIMIT_BYTES = 56 * 1024 * 1024

ROW_TILE = 256
NA_ROWS = 8
DF_QB = 512
DF_KC = 256
DF_UNROLL = 16
FF_SPLIT = 1

C_DN_Q, C_DN_K, C_DN_V, C_DN_GATE = 0, 1, 2, 3
C_NA_Q, C_NA_K, C_NA_V = 4, 5, 6
C_DF_Q, C_DF_K, C_DF_V = 7, 8, 9
C_HG_Q, C_HG_I, C_HG_FF, C_HG_FB, C_HG_GATE = 10, 11, 12, 13, 14
N_COL_BLOCKS = 15
P_WIDTH = N_COL_BLOCKS * GROUP_W + LANES
C_DN_BA_128 = N_COL_BLOCKS * GROUP_W // LANES


def _cparams(sem):
    return pltpu.CompilerParams(dimension_semantics=sem, vmem_limit_bytes=VMEM_LIMIT_BYTES)


def _mm(a, b):
    return jnp.dot(a, b, preferred_element_type=F32)


def _nt(a, b):
    return lax.dot_general(a, b, (((1,), (1,)), ((), ())), preferred_element_type=F32)


def _silu(x):
    return x * jax.nn.sigmoid(x)


def _split3(x):
    hi = x.astype(BF16)
    r = x - hi.astype(F32)
    mid = r.astype(BF16)
    lo = (r - mid.astype(F32)).astype(BF16)
    return hi, mid, lo


def _split2(x):
    hi = x.astype(BF16)
    lo = (x - hi.astype(F32)).astype(BF16)
    return hi, lo


def _head_masks(n_lanes=GROUP_W):
    lane = lax.broadcasted_iota(jnp.int32, (1, n_lanes), 1)
    return [(lane // HEAD_DIM == h).astype(F32) for h in range(GROUP_HEADS)]


def _mrs(x, hms):
    return jnp.concatenate([x * hm for hm in hms], axis=0)


def _rbs(z):
    n = z.shape[0] // GROUP_HEADS
    return z[0:n] + z[n:2 * n] + z[2 * n:3 * n] + z[3 * n:4 * n]


def _tile4(x):
    return jnp.concatenate([x, x, x, x], axis=0)


def _group_sum(x, g2_ref):
    hi, lo = _split2(x)
    return _mm(jnp.concatenate([hi, lo], axis=1), g2_ref[...])


def _rms_mod(x, nw, shift, scale):
    ms = jnp.mean(x * x, axis=-1, keepdims=True)
    return (x * lax.rsqrt(ms + EPS) * nw) * (1.0 + scale) + shift


def _ada_kernel(c_ref, w_ref, b_ref, o_ref):
    s = _silu(c_ref[...])
    o_ref[0] = _mm(s.astype(BF16), w_ref[0]) + b_ref[0]


def _ada_call(c_all, w_ada_bf, b_ada):
    depth, d, d6 = w_ada_bf.shape
    n_chunks = d6 // d
    return pl.pallas_call(
        _ada_kernel,
        out_shape=jax.ShapeDtypeStruct((depth, SUBLANES, d6), F32),
        grid=(depth, n_chunks),
        in_specs=[
            pl.BlockSpec((SUBLANES, d), lambda l, j: (0, 0)),
            pl.BlockSpec((1, d, d), lambda l, j: (l, 0, j)),
            pl.BlockSpec((1, 1, d), lambda l, j: (l, 0, j)),
        ],
        out_specs=pl.BlockSpec((1, SUBLANES, d), lambda l, j: (l, 0, j)),
        compiler_params=_cparams(("arbitrary", "arbitrary")),
        name="ada_ln",
    )(c_all, w_ada_bf, b_ada.reshape(depth, 1, d6))


def _inproj_kernel(x_ref, mod_ref, nw_ref, w_ref, o_ref):
    h = _rms_mod(x_ref[0], nw_ref[...], mod_ref[0, 0:1, :], mod_ref[0, 1:2, :])
    o_ref[0] = _mm(h.astype(BF16), w_ref[...])


def _inproj_call(xall, mod, nw, w_bf, nt_lat):
    b, t, d = xall.shape
    nt = t // ROW_TILE
    pw = w_bf.shape[1]
    return pl.pallas_call(
        _inproj_kernel,
        out_shape=jax.ShapeDtypeStruct((b, t, pw), F32),
        grid=(b, nt),
        in_specs=[
            pl.BlockSpec((1, ROW_TILE, d), lambda i, j: (i, j, 0)),
            pl.BlockSpec((1, 6, d), lambda i, j: (i * 2 + j // nt_lat, 0, 0)),
            pl.BlockSpec((1, d), lambda i, j: (0, 0)),
            pl.BlockSpec((d, pw), lambda i, j: (0, 0), pipeline_mode=pl.Buffered(1)),
        ],
        out_specs=pl.BlockSpec((1, ROW_TILE, pw), lambda i, j: (i, j, 0)),
        compiler_params=_cparams(("arbitrary", "arbitrary")),
        name="in_proj",
    )(xall, mod, nw, w_bf)


def _shift_rows(x, prev_row, next_row, gpos, seg_starts, seg_ends):
    n = x.shape[0]
    row = lax.broadcasted_iota(jnp.int32, (n, 1), 0)
    xm = jnp.where(row == 0, prev_row, pltpu.roll(x, 1, axis=0))
    xp = jnp.where(row == n - 1, next_row, pltpu.roll(x, n - 1, axis=0))
    at_start = functools.reduce(jnp.logical_or, [gpos == s for s in seg_starts])
    at_end = functools.reduce(jnp.logical_or, [gpos == e - 1 for e in seg_ends])
    return jnp.where(at_start, 0.0, xm), jnp.where(at_end, 0.0, xp)


def _dn_prep_kernel(x_ref, xp_ref, xn_ref, ba_ref, cw_ref, aneg_ref, dtb_ref, e3_ref, g2_ref,
                    qkv_ref, aux_ref, *, n_lat, n_all):
    tm = x_ref.shape[1]
    x = x_ref[0]
    gpos = pl.program_id(1) * tm + lax.broadcasted_iota(jnp.int32, (tm, 1), 0)
    xm, xp = _shift_rows(x, xp_ref[0, SUBLANES - 1:SUBLANES, :], xn_ref[0, 0:1, :], gpos,
                         (0, n_lat), (n_lat, n_all))
    cw = cw_ref[...]
    y = _silu(xm * cw[0:1] + x * cw[1:2] + xp * cw[2:3])
    q, k, v = y[:, 0:GROUP_W], y[:, GROUP_W:2 * GROUP_W], y[:, 2 * GROUP_W:3 * GROUP_W]
    qn = q * lax.rsqrt(_group_sum(q * q, g2_ref) + EPS) * (HEAD_DIM ** -0.5)
    kn = k * lax.rsqrt(_group_sum(k * k, g2_ref) + EPS)
    qkv_ref[0] = jnp.concatenate([qn, kn, v], axis=1)
    ba = ba_ref[0]
    lane = lax.broadcasted_iota(jnp.int32, ba.shape, 1)
    z = ba + dtb_ref[...]
    softplus = jnp.maximum(z, 0.0) + jnp.log(1.0 + jnp.exp(-jnp.abs(z)))
    gates = jnp.where(lane < 2 * GROUP_HEADS, jax.nn.sigmoid(ba), aneg_ref[...] * softplus)
    gates = jnp.where(lane < 4 * GROUP_HEADS, gates, 0.0)
    hi, mid, lo = _split3(gates)
    aux_ref[0] = _mm(jnp.concatenate([hi, mid, lo], axis=1), e3_ref[...])


def _halo_specs(width, col, n_rows_total, tm):
    per = tm // SUBLANES
    last = n_rows_total // SUBLANES - 1
    prev = pl.BlockSpec((1, SUBLANES, width), lambda i, j: (i, jnp.maximum(j * per - 1, 0), col))
    nxt = pl.BlockSpec((1, SUBLANES, width), lambda i, j: (i, jnp.minimum((j + 1) * per, last), col))
    return prev, nxt


def _dn_prep_call(p, conv_w, aneg, dtb, e3, g2, n_lat):
    b, t, _ = p.shape
    nt = t // ROW_TILE
    w3 = 3 * GROUP_W
    prev, nxt = _halo_specs(w3, 0, t, ROW_TILE)
    kern = functools.partial(_dn_prep_kernel, n_lat=n_lat, n_all=t)
    return pl.pallas_call(
        kern,
        out_shape=(jax.ShapeDtypeStruct((b, t, w3), F32), jax.ShapeDtypeStruct((b, t, 4 * GROUP_W), F32)),
        grid=(b, nt),
        in_specs=[
            pl.BlockSpec((1, ROW_TILE, w3), lambda i, j: (i, j, 0)),
            prev, nxt,
            pl.BlockSpec((1, ROW_TILE, LANES), lambda i, j: (i, j, C_DN_BA_128)),
            pl.BlockSpec((3, w3), lambda i, j: (0, 0)),
            pl.BlockSpec((1, LANES), lambda i, j: (0, 0)),
            pl.BlockSpec((1, LANES), lambda i, j: (0, 0)),
            pl.BlockSpec(e3.shape, lambda i, j: (0, 0)),
            pl.BlockSpec(g2.shape, lambda i, j: (0, 0)),
        ],
        out_specs=(pl.BlockSpec((1, ROW_TILE, w3), lambda i, j: (i, j, 0)),
                   pl.BlockSpec((1, ROW_TILE, 4 * GROUP_W), lambda i, j: (i, j, 0))),
        compiler_params=_cparams(("arbitrary", "arbitrary")),
        name="dn_prep",
    )(p, p, p, p, conv_w, aneg, dtb, e3, g2)


def _scan_tile(d, s, nt_lat):
    return jnp.where(s == 0, nt_lat, jnp.where(d == 0, s - 1, nt_lat - s))


def _unit_tri_inverse(mats, lv_ref):
    n = mats[0].shape[0]
    eye = (lax.broadcasted_iota(jnp.int32, (n, n), 0) == lax.broadcasted_iota(jnp.int32, (n, n), 1)).astype(F32)
    n1 = [a * lv_ref[0] for a in mats]
    n1b = [x.astype(BF16) for x in n1]
    n2 = [_mm(x, x) for x in n1b]
    n2b = [x.astype(BF16) for x in n2]
    n4 = [_mm(x, x) for x in n2b]
    inner = [_mm((eye + x2).astype(BF16), (eye + x4).astype(BF16)) for x2, x4 in zip(n2, n4)]
    ts = [_mm((eye - x1).astype(BF16), y.astype(BF16)) for x1, y in zip(n1, inner)]
    for lvl in (1, 2, 3):
        tb = [t.astype(BF16) for t in ts]
        xs = [_mm((a * lv_ref[lvl]).astype(BF16), t) for a, t in zip(mats, tb)]
        ts = [t - _mm(t16, x.astype(BF16)) for t, t16, x in zip(ts, tb, xs)]
    return ts


def _dn_scan_kernel(qkv_f, qkv_b, beta_f, beta_b, la_f, la_b, dm_ref, lv_ref, tri_ref, of_ref, ob_ref, sf_ref, sb_ref):
    @pl.when(pl.program_id(1) == 0)
    def _():
        sf_ref[...] = jnp.zeros_like(sf_ref)
        sb_ref[...] = jnp.zeros_like(sb_ref)

    hms = _head_masks()
    n_chunks = qkv_f.shape[1] // CHUNK
    rows, a_mats, attn, w_in, u_in, qg, kdec_t, glast = [], [], [], [], [], [], [], []
    for d, (qkv_ref, beta_ref, la_ref) in enumerate(((qkv_f, beta_f, la_f), (qkv_b, beta_b, la_b))):
        incl = dm_ref[d, 0]
        strict = dm_ref[d, 1]
        for i in range(n_chunks):
            c = i if d == 0 else n_chunks - 1 - i
            r = slice(c * CHUNK, (c + 1) * CHUNK)
            rows.append(r)
            q = qkv_ref[0, r, 0:GROUP_W]
            k = qkv_ref[0, r, GROUP_W:2 * GROUP_W]
            v = qkv_ref[0, r, 2 * GROUP_W:3 * GROUP_W]
            beta = beta_ref[0, r, :]
            hi, mid, lo = _split3(la_ref[0, r, :])
            g = _mm(tri_ref[d], jnp.concatenate([hi, mid, lo], axis=0))
            gtot = g[CHUNK - 1:CHUNK, :] if d == 0 else g[0:1, :]
            eg = jnp.exp(g)
            rg = _tile4(g)
            decay = jnp.exp((rg - rg.T) * incl) * incl
            mk = _mrs(k, hms).astype(BF16)
            a_mats.append(strict * _tile4(beta) * _nt(mk, mk) * decay)
            attn.append((_nt(_mrs(q, hms).astype(BF16), mk) * decay).astype(BF16))
            w_in.append(_mrs(k * beta * eg, hms).astype(BF16))
            u_in.append(_mrs(v * beta, hms).astype(BF16))
            qg.append(_mrs(q * eg, hms).astype(BF16))
            kdec_t.append(_mrs(k * jnp.exp(gtot - g), hms).T.astype(BF16))
            glast.append(jnp.exp(gtot))
    tinv = [t.astype(BF16) for t in _unit_tri_inverse(a_mats, lv_ref)]
    w_bd = [_mm(t, x).astype(BF16) for t, x in zip(tinv, w_in)]
    u_bd = [_mm(t, x) for t, x in zip(tinv, u_in)]
    for i in range(n_chunks):
        for d, (s_ref, o_ref) in enumerate(((sf_ref, of_ref), (sb_ref, ob_ref))):
            n = d * n_chunks + i
            sv = s_ref[...]
            svb = sv.astype(BF16)
            v_newb = (u_bd[n] - _mm(w_bd[n], svb)).astype(BF16)
            o_ref[0, rows[n], :] = _rbs(_mm(qg[n], svb) + _mm(attn[n], v_newb))
            s_ref[...] = sv * glast[n] + _mm(kdec_t[n], v_newb)


def _dn_scan_call(qkv, aux, dmask, lvmask, tri3, nt_lat):
    b, t, w3 = qkv.shape
    ns = t // ROW_TILE
    tile = lambda d: (lambda s: _scan_tile(d, s, nt_lat))
    row = lambda w, col, d: pl.BlockSpec((1, ROW_TILE, w), lambda i, s: (i, tile(d)(s), col))
    const = lambda a: pl.BlockSpec(a.shape, lambda i, s: (0,) * a.ndim)
    out = jax.ShapeDtypeStruct((b, t, GROUP_W), F32)
    return pl.pallas_call(
        _dn_scan_kernel,
        out_shape=(out, out),
        grid=(b, ns),
        in_specs=[row(w3, 0, 0), row(w3, 0, 1), row(GROUP_W, 0, 0), row(GROUP_W, 1, 1),
                  row(GROUP_W, 2, 0), row(GROUP_W, 3, 1), const(dmask), const(lvmask), const(tri3)],
        out_specs=(row(GROUP_W, 0, 0), row(GROUP_W, 0, 1)),
        scratch_shapes=[pltpu.VMEM((GROUP_W, GROUP_W), F32), pltpu.VMEM((GROUP_W, GROUP_W), F32)],
        compiler_params=_cparams(("arbitrary", "arbitrary")),
        name="dn_scan",
    )(qkv, qkv, aux, aux, aux, aux, dmask, lvmask, tri3)


HG_LEVELS = (1, 2, 4, 8, 16, 32)


def _hg_scan_kernel(q_f, q_b, i_f, i_b, f_f, f_b, lb_ref, w3_ref, lm_ref, same_ref, of_ref, ob_ref,
                    st_ref, qb_ref, ke_ref, oi_ref):
    @pl.when(pl.program_id(1) == 0)
    def _():
        st_ref[...] = jnp.zeros_like(st_ref)

    hms = _head_masks()
    same = same_ref[...]
    tm = q_f.shape[1]
    n_chunks = tm // CHUNK
    nl = len(HG_LEVELS)
    dirs = ((q_f, i_f, f_f, of_ref), (q_b, i_b, f_b, ob_ref))
    units = [(d, c) for d in range(2) for c in range(n_chunks)]
    rows = [slice(c * CHUNK, (c + 1) * CHUNK) for _, c in units]
    qs = [dirs[d][0][0, r, :] * (HEAD_DIM ** -0.5) for (d, _), r in zip(units, rows)]
    fs = [lb_ref[d] + (1.0 - lb_ref[d]) * jax.nn.sigmoid(dirs[d][2][0, r, :]) for (d, _), r in zip(units, rows)]
    kxs = [1.0 - f for f in fs]
    es = [_mm(w3_ref[d], jnp.concatenate(_split3(jnp.log(f)), axis=0)) for (d, _), f in zip(units, fs)]
    atts = [lm_ref[d, 0] * _nt(_mrs(q, hms).astype(BF16), kx.astype(BF16)) for (d, _), q, kx in zip(units, qs, kxs)]
    for li in range(nl):
        ea = slice((2 + li) * CHUNK, (3 + li) * CHUNK)
        eb = slice((2 + nl + li) * CHUNK, (3 + nl + li) * CHUNK)
        terms = [_nt(_mrs(q * jnp.exp(e[ea]), hms).astype(BF16), (kx * jnp.exp(e[eb])).astype(BF16))
                 for q, kx, e in zip(qs, kxs, es)]
        atts = [a + lm_ref[d, 1 + li] * t for (d, _), a, t in zip(units, atts, terms)]
    tots = []
    for n, (d, c) in enumerate(units):
        cum = es[n][0:CHUNK]
        tot = es[n][CHUNK:2 * CHUNK]
        oi_ref[d, rows[n], :] = _rbs(_mm(atts[n].astype(BF16), dirs[d][1][0, rows[n], :].astype(BF16)) * same)
        qb_ref[d, rows[n], :] = qs[n] * jnp.exp(cum)
        ke_ref[d, rows[n], :] = kxs[n] * jnp.exp(tot - cum)
        tots.append(tot[0:1, :])
    vts = [dirs[d][1][0].T for d in range(2)]
    kebs = [ke_ref[d].astype(BF16) for d in range(2)]
    lane = lax.broadcasted_iota(jnp.int32, (1, tm), 1)
    for i in range(n_chunks):
        for d in range(2):
            c = i if d == 0 else n_chunks - 1 - i
            r = slice(c * CHUNK, (c + 1) * CHUNK)
            st = st_ref[d]
            dirs[d][3][0, r, :] = oi_ref[d, r, :] + _nt(qb_ref[d, r, :].astype(BF16), st.astype(BF16))
            vt_c = (vts[d] * (lane // CHUNK == c).astype(F32)).astype(BF16)
            st_ref[d] = st * jnp.exp(tots[d * n_chunks + c]) + same * _mm(vt_c, kebs[d])


def _hg_scan_call(p, lb, w3, lmask, same, nt_lat):
    b, t, _ = p.shape
    ns = t // ROW_TILE
    row = lambda col, d: pl.BlockSpec((1, ROW_TILE, GROUP_W), lambda i, s: (i, _scan_tile(d, s, nt_lat), col))
    const = lambda a: pl.BlockSpec(a.shape, lambda i, s: (0,) * a.ndim)
    out = jax.ShapeDtypeStruct((b, t, GROUP_W), F32)
    return pl.pallas_call(
        _hg_scan_kernel,
        out_shape=(out, out),
        grid=(b, ns),
        in_specs=[row(C_HG_Q, 0), row(C_HG_Q, 1), row(C_HG_I, 0), row(C_HG_I, 1), row(C_HG_FF, 0), row(C_HG_FB, 1),
                  const(lb), const(w3), const(lmask), const(same)],
        out_specs=(row(0, 0), row(0, 1)),
        scratch_shapes=[pltpu.VMEM((2, GROUP_W, GROUP_W), F32), pltpu.VMEM((2, ROW_TILE, GROUP_W), F32),
                        pltpu.VMEM((2, ROW_TILE, GROUP_W), F32), pltpu.VMEM((2, ROW_TILE, GROUP_W), F32)],
        compiler_params=_cparams(("arbitrary", "arbitrary")),
        name="hg_scan",
    )(p, p, p, p, p, p, lb, w3, lmask, same)


def _na_kernel(q_ref, kp_ref, kc_ref, kn_ref, vp_ref, vc_ref, vn_ref, kx_ref, vx_ref, bias_ref, same_ref,
               o_ref, kw_ref, vw_ref, *, n_rows):
    i = pl.program_id(1)
    blk = q_ref.shape[1]
    for j, (kr, vr) in enumerate(((kp_ref, vp_ref), (kc_ref, vc_ref), (kn_ref, vn_ref))):
        kw_ref[j * blk:(j + 1) * blk, :] = kr[0].astype(BF16)
        vw_ref[j * blk:(j + 1) * blk, :] = vr[0].astype(BF16)
    kx = kx_ref[0].astype(BF16)
    vx = vx_ref[0].astype(BF16)
    hms = _head_masks()
    same = same_ref[...]
    win = NA_KH * GRID_W

    def row_body(rr, carry):
        r = i * NA_ROWS + rr
        rs = jnp.clip(r - NA_KH // 2, 0, n_rows - NA_KH)
        start = pl.multiple_of((rs - i * NA_ROWS + NA_ROWS) * GRID_W, GRID_W)
        rows = pl.ds(pl.multiple_of(rr * GRID_W, GRID_W), GRID_W)
        mq = _mrs(q_ref[0, rows, :] * (HEAD_DIM ** -0.5), hms).astype(BF16)
        s_loc = _nt(mq, kw_ref[pl.ds(start, win), :]) + bias_ref[rs - r + NA_KH - 1]
        s_ctx = _nt(mq, kx)
        m = jnp.maximum(jnp.max(s_loc, axis=-1, keepdims=True), jnp.max(s_ctx, axis=-1, keepdims=True))
        p_loc = jnp.exp(s_loc - m)
        p_ctx = jnp.exp(s_ctx - m)
        den = jnp.sum(p_loc, axis=-1, keepdims=True) + jnp.sum(p_ctx, axis=-1, keepdims=True)
        o = (_mm(p_loc.astype(BF16), vw_ref[pl.ds(start, win), :]) + _mm(p_ctx.astype(BF16), vx)) / den
        o_ref[0, rows, :] = _rbs(o * same)
        return carry

    lax.fori_loop(0, NA_ROWS, row_body, 0, unroll=4)


def _na_call(p, bias, same, n_lat, n_ctx):
    b, t, _ = p.shape
    blk = NA_ROWS * GRID_W
    nb = n_lat // blk
    spec = lambda col, f: pl.BlockSpec((1, blk, GROUP_W), lambda i, j: (i, f(j), col))
    prev = lambda j: jnp.maximum(j - 1, 0)
    cur = lambda j: j
    nxt = lambda j: jnp.minimum(j + 1, nb - 1)
    ctx_spec = lambda col: pl.BlockSpec((1, n_ctx, GROUP_W), lambda i, j: (i, n_lat // n_ctx, col))
    kern = functools.partial(_na_kernel, n_rows=n_lat // GRID_W)
    return pl.pallas_call(
        kern,
        out_shape=jax.ShapeDtypeStruct((b, n_lat, GROUP_W), F32),
        grid=(b, nb),
        in_specs=[spec(C_NA_Q, cur),
                  spec(C_NA_K, prev), spec(C_NA_K, cur), spec(C_NA_K, nxt),
                  spec(C_NA_V, prev), spec(C_NA_V, cur), spec(C_NA_V, nxt),
                  ctx_spec(C_NA_K), ctx_spec(C_NA_V),
                  pl.BlockSpec(bias.shape, lambda i, j: (0, 0, 0), pipeline_mode=pl.Buffered(1)),
                  pl.BlockSpec(same.shape, lambda i, j: (0, 0))],
        out_specs=pl.BlockSpec((1, blk, GROUP_W), lambda i, j: (i, j, 0)),
        scratch_shapes=[pltpu.VMEM((3 * blk, GROUP_W), BF16), pltpu.VMEM((3 * blk, GROUP_W), BF16)],
        compiler_params=_cparams(("arbitrary", "arbitrary")),
        name="na_attn",
    )(p, p, p, p, p, p, p, p, p, bias, same)


def _na_ctx_kernel(q_ref, k_ref, v_ref, o_ref):
    hms = _head_masks()
    n = q_ref.shape[1]
    mq = _mrs(q_ref[0] * (HEAD_DIM ** -0.5), hms).astype(BF16)
    s = _nt(mq, k_ref[0].astype(BF16))
    p = jnp.exp(s - jnp.max(s, axis=-1, keepdims=True))
    o = _mm(p.astype(BF16), v_ref[0].astype(BF16)) / jnp.sum(p, axis=-1, keepdims=True)
    o_ref[0] = _rbs(o * jnp.concatenate([jnp.broadcast_to(hm, (n, GROUP_W)) for hm in hms], axis=0))


def _na_ctx_call(p, n_lat, n_ctx):
    b = p.shape[0]
    spec = lambda col: pl.BlockSpec((1, n_ctx, GROUP_W), lambda i: (i, n_lat // n_ctx, col))
    return pl.pallas_call(
        _na_ctx_kernel,
        out_shape=jax.ShapeDtypeStruct((b, n_ctx, GROUP_W), F32),
        grid=(b,),
        in_specs=[spec(C_NA_Q), spec(C_NA_K), spec(C_NA_V)],
        out_specs=pl.BlockSpec((1, n_ctx, GROUP_W), lambda i: (i, 0, 0)),
        compiler_params=_cparams(("arbitrary",)),
        name="na_ctx",
    )(p, p, p)


def _df_prep_kernel(q_ref, k_ref, v_ref, cos_ref, sin_ref, qt_ref, kr_ref, vt_ref):
    cos = cos_ref[...]
    sin = sin_ref[...]
    lane = lax.broadcasted_iota(jnp.int32, cos.shape, 1)
    first = (lane % 16) < 8

    def rope(x):
        w = x.shape[1]
        partner = jnp.where(first, pltpu.roll(x, w - 8, axis=1), pltpu.roll(x, 8, axis=1))
        return x * cos + partner * sin

    tm = q_ref.shape[1]
    qt = (rope(q_ref[0]) * (DF_HEAD_DIM ** -0.5 * LOG2E)).T
    row = lax.broadcasted_iota(jnp.int32, (GROUP_W, 1), 0)
    for s in range(2 * GROUP_HEADS):
        keep = (row // DF_HEAD_DIM == s).astype(F32)
        qt_ref[0, s] = (qt * keep).astype(BF16)
    kr_ref[0, 0] = rope(k_ref[0]).astype(BF16)
    vt = v_ref[0].T
    ext = (lax.broadcasted_iota(jnp.int32, (2 * SUBLANES, tm), 0) == 0).astype(F32)
    for h in range(GROUP_HEADS):
        vt_ref[0, h, 0] = jnp.concatenate([vt[h * HEAD_DIM:(h + 1) * HEAD_DIM], ext], axis=0).astype(BF16)


def _df_prep_call(p, cos, sin):
    b, t, _ = p.shape
    nt = t // DF_KC
    vrows = HEAD_DIM + 2 * SUBLANES
    blk = lambda col: pl.BlockSpec((1, DF_KC, GROUP_W), lambda i, j: (i, j, col))
    return pl.pallas_call(
        _df_prep_kernel,
        out_shape=(jax.ShapeDtypeStruct((b, 2 * GROUP_HEADS, GROUP_W, t), BF16),
                   jax.ShapeDtypeStruct((b, nt, DF_KC, GROUP_W), BF16),
                   jax.ShapeDtypeStruct((b, GROUP_HEADS, nt, vrows, DF_KC), BF16)),
        grid=(b, nt),
        in_specs=[blk(C_DF_Q), blk(C_DF_K), blk(C_DF_V),
                  pl.BlockSpec((DF_KC, GROUP_W), lambda i, j: (j, 0)),
                  pl.BlockSpec((DF_KC, GROUP_W), lambda i, j: (j, 0))],
        out_specs=(pl.BlockSpec((1, 2 * GROUP_HEADS, GROUP_W, DF_KC), lambda i, j: (i, 0, 0, j)),
                   pl.BlockSpec((1, 1, DF_KC, GROUP_W), lambda i, j: (i, j, 0, 0)),
                   pl.BlockSpec((1, GROUP_HEADS, 1, vrows, DF_KC), lambda i, j: (i, 0, j, 0, 0))),
        compiler_params=_cparams(("arbitrary", "arbitrary")),
        name="df_prep",
    )(p, p, p, cos, sin)


def _df_attn_kernel(qt_ref, k_ref, vt_ref, lam_ref, nw_ref, o_ref,
                    m_ref, acc_ref, s_ref, p_ref, al_ref, *, lam_init):
    nk = k_ref.shape[1]
    m_ref[...] = jnp.full(m_ref.shape, -jnp.inf, F32)
    acc_ref[...] = jnp.zeros_like(acc_ref)

    def scores(c, slot):
        kc = k_ref[0, c]
        for j in range(2):
            s_ref[slot, j] = _mm(kc, qt_ref[0, j])

    def probs(slot):
        for j in range(2):
            m_old = m_ref[j]
            s = s_ref[slot, j]
            m_new = jnp.maximum(m_old, jnp.max(s, axis=0, keepdims=True))
            p_ref[slot, j] = jnp.exp2(s - m_new).astype(BF16)
            al_ref[slot, j] = jnp.exp2(m_old - m_new)
            m_ref[j] = m_new

    def accum(c, slot):
        vt = vt_ref[0, 0, c]
        for j in range(2):
            acc_ref[j] = acc_ref[j] * al_ref[slot, j] + _mm(vt, p_ref[slot, j])

    def iteration(i, slot, do_scores=True, do_probs=True, do_accum=True):
        if do_accum:
            accum(i - 2, slot)
        if do_probs:
            probs(1 - slot)
        if do_scores:
            scores(i, slot)

    for i in range(min(2, nk + 2)):
        iteration(i, i % 2, i < nk, 1 <= i <= nk, False)
    n_steady = max(nk - 2, 0)

    def group(k, carry):
        for u in range(DF_UNROLL):
            iteration(2 + DF_UNROLL * k + u, u % 2)
        return carry

    n_groups = n_steady // DF_UNROLL
    lax.fori_loop(0, n_groups, group, 0)
    for i in range(2 + DF_UNROLL * n_groups, nk + 2):
        iteration(i, i % 2, i < nk, i <= nk, True)
    _df_finish(acc_ref, lam_ref, nw_ref, o_ref, lam_init)


def _df_finish(acc_ref, lam_ref, nw_ref, o_ref, lam_init):
    lp = lam_ref[...]
    lam = (jnp.exp(jnp.sum(lp[0:1] * lp[1:2], keepdims=True)) - jnp.exp(jnp.sum(lp[2:3] * lp[3:4], keepdims=True))
           + lam_init)
    a0 = acc_ref[0]
    a1 = acc_ref[1]
    o = a0[0:HEAD_DIM] / a0[HEAD_DIM:HEAD_DIM + 1] - lam * (a1[0:HEAD_DIM] / a1[HEAD_DIM:HEAD_DIM + 1])
    ms = jnp.mean(o * o, axis=0, keepdims=True)
    o_ref[0] = o * lax.rsqrt(ms + EPS) * nw_ref[...] * (1.0 - lam_init)


def _df_attn_call(qt, kr, vt, lam_p, nw_col, lam_init, q_off, n_q, qb, k_off, nk):
    b = qt.shape[0]
    vrows = vt.shape[3]
    return pl.pallas_call(
        functools.partial(_df_attn_kernel, lam_init=lam_init),
        out_shape=jax.ShapeDtypeStruct((b, GROUP_W, n_q), F32),
        grid=(b, GROUP_HEADS, n_q // qb),
        in_specs=[
            pl.BlockSpec((1, 2, GROUP_W, qb), lambda i, h, q: (i, h, 0, q_off // qb + q)),
            pl.BlockSpec((1, nk, DF_KC, GROUP_W), lambda i, h, q: (i, k_off // nk, 0, 0),
                         pipeline_mode=pl.Buffered(1)),
            pl.BlockSpec((1, 1, nk, vrows, DF_KC), lambda i, h, q: (i, h, k_off // nk, 0, 0)),
            pl.BlockSpec(lam_p.shape, lambda i, h, q: (0, 0)),
            pl.BlockSpec(nw_col.shape, lambda i, h, q: (0, 0)),
        ],
        out_specs=pl.BlockSpec((1, HEAD_DIM, qb), lambda i, h, q: (i, h, q)),
        scratch_shapes=[pltpu.VMEM((2, 1, qb), F32), pltpu.VMEM((2, vrows, qb), F32),
                        pltpu.VMEM((2, 2, DF_KC, qb), F32),
                        pltpu.VMEM((2, 2, DF_KC, qb), BF16), pltpu.VMEM((2, 2, 1, qb), F32)],
        compiler_params=_cparams(("arbitrary", "arbitrary", "arbitrary")),
        name="df_attn",
    )(qt, kr, vt, lam_p, nw_col)


def _outproj_kernel(x_ref, mod_ref, dno_f, dno_b, dng_ref, na_ref, nac_ref, dft_ref, dftc_ref, hgo_f, hgo_b, hgg_ref,
                    dnw_ref, hgw_ref, g2_ref, w_ref, o_ref, *, nt_lat):
    inv = 1.0 / HEAD_DIM
    is_ctx = pl.program_id(1) == nt_lat

    def gated_norm(o, nw, gate):
        ms = _group_sum(o * o, g2_ref) * inv
        return o * lax.rsqrt(ms + EPS) * nw * _silu(gate)

    y_dn = gated_norm(dno_f[0] + dno_b[0], dnw_ref[...], dng_ref[0])
    y_hg = gated_norm(hgo_f[0] + hgo_b[0], hgw_ref[...], hgg_ref[0])
    y_df = jnp.where(is_ctx, dftc_ref[0], dft_ref[0]).T
    y_na = jnp.where(is_ctx, nac_ref[0], na_ref[0])
    acc = _mm(y_dn.astype(BF16), w_ref[0:GROUP_W, :])
    acc += _mm(y_na.astype(BF16), w_ref[GROUP_W:2 * GROUP_W, :])
    acc += _mm(y_df.astype(BF16), w_ref[2 * GROUP_W:3 * GROUP_W, :])
    acc += _mm(y_hg.astype(BF16), w_ref[3 * GROUP_W:4 * GROUP_W, :])
    o_ref[0] = x_ref[0] + mod_ref[0, 2:3, :] * acc


def _outproj_call(xall, mod, dn_o, p, y_na, y_na_ctx, y_dft, y_dft_ctx, hg_o, dnw, hgw, g2, w_bf, nt_lat, n_tiles):
    b, t, d = xall.shape
    row = lambda w, col=0: pl.BlockSpec((1, ROW_TILE, w), lambda i, j: (i, j, col))
    dirspec = lambda dd: row(GROUP_W)
    const = lambda a: pl.BlockSpec(a.shape, lambda i, j: (0,) * a.ndim)
    lat = lambda j: jnp.minimum(j, nt_lat - 1)
    return pl.pallas_call(
        functools.partial(_outproj_kernel, nt_lat=nt_lat),
        out_shape=jax.ShapeDtypeStruct((b, n_tiles * ROW_TILE, d), F32),
        grid=(b, n_tiles),
        in_specs=[row(d),
                  pl.BlockSpec((1, 6, d), lambda i, j: (i * 2 + j // nt_lat, 0, 0)),
                  dirspec(0), dirspec(1), row(GROUP_W, C_DN_GATE),
                  pl.BlockSpec((1, ROW_TILE, GROUP_W), lambda i, j: (i, lat(j), 0)),
                  pl.BlockSpec((1, ROW_TILE, GROUP_W), lambda i, j: (i, 0, 0)),
                  pl.BlockSpec((1, GROUP_W, ROW_TILE), lambda i, j: (i, 0, lat(j))),
                  pl.BlockSpec((1, GROUP_W, ROW_TILE), lambda i, j: (i, 0, 0)),
                  dirspec(0), dirspec(1), row(GROUP_W, C_HG_GATE),
                  const(dnw), const(hgw), const(g2),
                  pl.BlockSpec(w_bf.shape, lambda i, j: (0, 0), pipeline_mode=pl.Buffered(1))],
        out_specs=row(d),
        compiler_params=_cparams(("arbitrary", "arbitrary")),
        name="out_proj",
    )(xall, mod, dn_o[0], dn_o[1], p, y_na, y_na_ctx, y_dft, y_dft_ctx, hg_o[0], hg_o[1], p, dnw, hgw, g2, w_bf)


def _ffn_kernel(x_ref, xp_ref, xn_ref, mod_ref, nw_ref, wup_ref, cw_ref, wdn_ref, fw_ref, o_ref,
                *, n_lat, n_all, final):
    tm = x_ref.shape[1]
    d_ff = wdn_ref.shape[0]
    fc = d_ff // FF_SPLIT
    x = x_ref[0]
    xe = jnp.concatenate([x, xp_ref[0], xn_ref[0]], axis=0)
    hb = _rms_mod(xe, nw_ref[...], mod_ref[0, 3:4, :], mod_ref[0, 4:5, :]).astype(BF16)
    gpos = pl.program_id(1) * tm + lax.broadcasted_iota(jnp.int32, (tm, 1), 0)
    acc = jnp.zeros((tm, x.shape[1]), F32)
    for f in range(FF_SPLIT):
        g_all = _mm(hb, wup_ref[:, f * fc:(f + 1) * fc])
        val = _mm(hb[0:tm], wup_ref[:, d_ff + f * fc:d_ff + (f + 1) * fc])
        g = g_all[0:tm]
        gm, gp = _shift_rows(g, g_all[tm + SUBLANES - 1:tm + SUBLANES], g_all[tm + SUBLANES:tm + SUBLANES + 1],
                             gpos, (0, n_lat), (n_lat, n_all))
        cw = cw_ref[:, f * fc:(f + 1) * fc]
        a = _silu(gm * cw[0:1] + g * cw[1:2] + gp * cw[2:3]) * val
        acc += _mm(a.astype(BF16), wdn_ref[f * fc:(f + 1) * fc, :])
    y = x + mod_ref[0, 5:6, :] * acc
    if final:
        ms = jnp.mean(y * y, axis=-1, keepdims=True)
        y = y * lax.rsqrt(ms + EPS) * fw_ref[...]
    o_ref[0] = y


def _ffn_call(xmid, mod, nw, wup_bf, cw, wdn_bf, fw, nt_lat, n_tiles, n_lat, final):
    b, t, d = xmid.shape
    out_rows = n_tiles * ROW_TILE
    prev, nxt = _halo_specs(d, 0, out_rows, ROW_TILE)
    kern = functools.partial(_ffn_kernel, n_lat=n_lat, n_all=t, final=final)
    const = lambda a, **kw: pl.BlockSpec(a.shape, lambda i, j: (0,) * a.ndim, **kw)
    return pl.pallas_call(
        kern,
        out_shape=jax.ShapeDtypeStruct((b, out_rows, d), F32),
        grid=(b, n_tiles),
        in_specs=[pl.BlockSpec((1, ROW_TILE, d), lambda i, j: (i, j, 0)), prev, nxt,
                  pl.BlockSpec((1, 6, d), lambda i, j: (i * 2 + j // nt_lat, 0, 0)),
                  const(nw), const(wup_bf, pipeline_mode=pl.Buffered(1)), const(cw),
                  const(wdn_bf, pipeline_mode=pl.Buffered(1)), const(fw)],
        out_specs=pl.BlockSpec((1, ROW_TILE, d), lambda i, j: (i, j, 0)),
        compiler_params=_cparams(("arbitrary", "arbitrary")),
        name="conv_ffn",
    )(xmid, xmid, xmid, mod, nw, wup_bf, cw, wdn_bf, fw)


def _scan_constants():
    c = CHUNK
    n = GROUP_W
    t = np.arange(c)
    tri_f = (t[None, :] <= t[:, None]).astype(np.float32)
    tri = np.stack([tri_f, tri_f[::-1, ::-1]])
    tri3 = np.concatenate([tri] * 3, axis=2)

    idx = np.arange(n)
    hh, tt = idx // c, idx % c
    same = hh[:, None] == hh[None, :]
    ti, si = tt[:, None], tt[None, :]
    dmask = np.stack([np.stack([same & (si <= ti), same & (si < ti)]),
                      np.stack([same & (si >= ti), same & (si > ti)])]).astype(np.float32)
    lv = [same & (ti // 8 == si // 8)]
    for m in (8, 16, 32):
        lv.append(same & (ti // (2 * m) == si // (2 * m)) & (ti // m != si // m))
    lvmask = np.stack(lv).astype(np.float32)

    u = t[None, :]
    ws = [tri_f, np.ones((c, c), np.float32)]
    for m in HG_LEVELS:
        r = (t // m) * m
        ws.append(((u > r[:, None]) & (u <= t[:, None])).astype(np.float32))
    for m in HG_LEVELS:
        r2 = np.minimum((t // m + 1) * m, c - 1)
        ws.append(((u > t[:, None]) & (u <= r2[:, None])).astype(np.float32))
    w_f = np.concatenate(ws, axis=0)
    w_b = np.concatenate([w[::-1, ::-1] for w in ws], axis=0)
    w3 = np.stack([np.concatenate([w_f] * 3, axis=1), np.concatenate([w_b] * 3, axis=1)])

    lms = [np.eye(c, dtype=bool)]
    for m in HG_LEVELS:
        lms.append(((t[:, None] // m) % 2 == 1) & (t[None, :] // m == t[:, None] // m - 1))
    lm_f = np.stack(lms)
    lm_b = lm_f[:, ::-1, ::-1]
    lmask = np.stack([np.tile(lm_f, (1, GROUP_HEADS, 1)), np.tile(lm_b, (1, GROUP_HEADS, 1))]).astype(np.float32)

    g = same.astype(np.float32)
    g2 = np.concatenate([g, g], axis=0)
    e = np.zeros((LANES, 4 * GROUP_W), np.float32)
    for j in range(4 * GROUP_HEADS):
        e[j, j * HEAD_DIM:(j + 1) * HEAD_DIM] = 1.0
    e3 = np.concatenate([e, e, e], axis=0)
    return dict(tri3=jnp.asarray(tri3, BF16), dmask=jnp.asarray(dmask), lvmask=jnp.asarray(lvmask),
                w3=jnp.asarray(w3, BF16), lmask=jnp.asarray(lmask), same=jnp.asarray(g),
                g2=jnp.asarray(g2, BF16), e3=jnp.asarray(e3, BF16))


def _na_bias_table(rpb):
    cols = np.arange(GRID_W)
    c_start = np.clip(cols - NA_KW // 2, 0, GRID_W - NA_KW)
    kc = np.arange(GRID_W)
    valid = (kc[None, :] >= c_start[:, None]) & (kc[None, :] < c_start[:, None] + NA_KW)
    pad = GRID_W - NA_KW
    padded = jnp.pad(rpb.astype(F32), ((0, 0), (0, 0), (pad, pad)))
    toe = jnp.stack([padded[:, :, GRID_W - 1 - q:2 * GRID_W - 1 - q] for q in range(GRID_W)], axis=1)
    toe = jnp.where(valid[None, :, None, :], toe, -jnp.inf)
    variants = [toe[:, :, var:var + NA_KH, :].reshape(GROUP_HEADS * GRID_W, NA_KH * GRID_W) for var in range(NA_KH)]
    return jnp.stack(variants)


def _rope_tables(n_lat, n_ctx):
    pos = jnp.arange(n_lat)
    row = (pos // GRID_W).astype(F32)
    col = (pos % GRID_W).astype(F32)
    half = DF_HEAD_DIM // 2
    inv = 1.0 / (ROPE_BASE ** (jnp.arange(0, half, 2, dtype=F32) / half))
    lane = np.arange(GROUP_W)
    use_row = jnp.asarray((lane % DF_HEAD_DIM) < half)
    ang = jnp.where(use_row[None, :], row[:, None], col[:, None]) * inv[lane % (half // 2)][None, :]
    sign = jnp.asarray(np.where((lane % half) < half // 2, -1.0, 1.0), F32)
    cos = jnp.concatenate([jnp.cos(ang), jnp.ones((n_ctx, GROUP_W), F32)], axis=0)
    sin = jnp.concatenate([jnp.sin(ang) * sign[None, :], jnp.zeros((n_ctx, GROUP_W), F32)], axis=0)
    return cos, sin


def _reorder_w_in(w_in_l):
    d = w_in_l.shape[0]
    a = 4 * GROUP_W
    ba = w_in_l[:, a:a + 4 * GROUP_HEADS]
    rest = w_in_l[:, a + 4 * GROUP_HEADS:]
    pad = jnp.zeros((d, LANES - 4 * GROUP_HEADS), w_in_l.dtype)
    return jnp.concatenate([w_in_l[:, :a], rest, ba, pad], axis=1)


def kernel(x, c, ctx, c_ctx, w_ada, b_ada, norm1_w, norm2_w, w_in, dn_conv_w, dn_a_log, dn_dt_bias, dn_norm_w,
           na_rpb, df_lambda, df_norm_w, hg_lb_raw, hg_norm_w, w_out, w_up, ffn_conv_w, w_down, final_norm_w):
    b, n_lat, d = x.shape
    n_ctx = ctx.shape[1]
    depth = w_ada.shape[0]
    t = n_lat + n_ctx
    assert n_ctx == ROW_TILE == DF_KC and n_lat % (NA_ROWS * GRID_W) == 0 and n_lat % DF_QB == 0
    assert b <= SUBLANES - 1
    nt_lat = n_lat // ROW_TILE
    nt_all = t // ROW_TILE

    consts = _scan_constants()
    cos, sin = _rope_tables(n_lat, n_ctx)

    c_all = jnp.zeros((SUBLANES, d), F32).at[:b].set(c).at[b].set(c_ctx)
    ada = _ada_call(c_all, w_ada.astype(BF16), b_ada).reshape(depth, SUBLANES, 6, d)

    lb_all = jnp.cumsum(jax.nn.softmax(hg_lb_raw.astype(F32), axis=0), axis=0)
    lb_all = (lb_all - lb_all[0]).reshape(depth, 2, 1, GROUP_W)

    xall = jnp.concatenate([x, ctx], axis=1)
    out = None
    for l in range(depth):
        need_ctx = l < depth - 1
        n_tiles = nt_all if need_ctx else nt_lat
        lam_init = 0.8 - 0.6 * math.exp(-0.3 * l)
        mod = jnp.stack([ada[l, :b], jnp.broadcast_to(ada[l, b], (b, 6, d))], axis=1).reshape(2 * b, 6, d)

        p = _inproj_call(xall, mod, norm1_w[l].reshape(1, d), _reorder_w_in(w_in[l]).astype(BF16), nt_lat)

        lane = np.arange(LANES)
        gate_idx = np.clip(lane - 2 * GROUP_HEADS, 0, 2 * GROUP_HEADS - 1)
        is_decay = jnp.asarray((lane >= 2 * GROUP_HEADS) & (lane < 4 * GROUP_HEADS))
        aneg = jnp.where(is_decay, -jnp.exp(dn_a_log[l].astype(F32).reshape(-1)[gate_idx]), 0.0).reshape(1, LANES)
        dtb = jnp.where(is_decay, dn_dt_bias[l].astype(F32).reshape(-1)[gate_idx], 0.0).reshape(1, LANES)
        dn_qkv, dn_aux = _dn_prep_call(p, dn_conv_w[l], aneg, dtb, consts["e3"], consts["g2"], n_lat)
        dn_o = _dn_scan_call(dn_qkv, dn_aux, consts["dmask"], consts["lvmask"], consts["tri3"], nt_lat)

        hg_o = _hg_scan_call(p, lb_all[l], consts["w3"], consts["lmask"], consts["same"], nt_lat)

        y_na = _na_call(p, _na_bias_table(na_rpb[l]), consts["same"], n_lat, n_ctx)
        y_na_ctx = _na_ctx_call(p, n_lat, n_ctx) if need_ctx else y_na

        qt, kr, vt = _df_prep_call(p, cos, sin)
        nw_col = df_norm_w[l].reshape(HEAD_DIM, 1)
        lam_p = jnp.pad(df_lambda[l].astype(F32), ((0, 4 * SUBLANES - 4), (0, LANES - DF_HEAD_DIM)))
        y_dft = _df_attn_call(qt, kr, vt, lam_p, nw_col, lam_init, 0, n_lat, DF_QB, 0, t // DF_KC)
        y_dft_ctx = (_df_attn_call(qt, kr, vt, lam_p, nw_col, lam_init, n_lat, n_ctx, n_ctx,
                                   n_lat // DF_KC, 1) if need_ctx else y_dft)

        xmid = _outproj_call(xall, mod, dn_o, p, y_na, y_na_ctx, y_dft, y_dft_ctx, hg_o,
                             jnp.tile(dn_norm_w[l], GROUP_HEADS).reshape(1, GROUP_W),
                             jnp.tile(hg_norm_w[l], GROUP_HEADS).reshape(1, GROUP_W),
                             consts["g2"], w_out[l].astype(BF16), nt_lat, n_tiles)
        out = _ffn_call(xmid, mod, norm2_w[l].reshape(1, d), w_up[l].astype(BF16), ffn_conv_w[l],
                        w_down[l].astype(BF16), final_norm_w.reshape(1, d), nt_lat, n_tiles, n_lat,
                        final=not need_ctx)
        xall = out
    return out
```

```python
import functools
import math

import numpy as np
import jax
import jax.numpy as jnp
from jax import lax
from jax.experimental import pallas as pl
from jax.experimental.pallas import tpu as pltpu

F32 = jnp.float32
BF16 = jnp.bfloat16

GRID_W = 64
HEAD_DIM = 64
GROUP_HEADS = 4
GROUP_W = GROUP_HEADS * HEAD_DIM
CHUNK = 64
NA_KH = 8
NA_KW = 16
DF_HEAD_DIM = HEAD_DIM // 2
ROPE_BASE = 10000.0
EPS = 1e-6
LOG2E = 1.4426950408889634

LANES = 128
SUBLANES = 8
VMEM_LIMIT_BYTES = 56 * 1024 * 1024

ROW_TILE = 256
NA_ROWS = 8
DF_QB = 1024
DF_KC = 256
DF_UNROLL = 16
FF_SPLIT = 1

C_DN_Q, C_DN_K, C_DN_V, C_DN_GATE = 0, 1, 2, 3
C_NA_Q, C_NA_K, C_NA_V = 4, 5, 6
C_DF_Q, C_DF_K, C_DF_V = 7, 8, 9
C_HG_Q, C_HG_I, C_HG_FF, C_HG_FB, C_HG_GATE = 10, 11, 12, 13, 14
N_COL_BLOCKS = 15
P_WIDTH = N_COL_BLOCKS * GROUP_W + LANES
C_DN_BA_128 = N_COL_BLOCKS * GROUP_W // LANES


def _cparams(sem):
    return pltpu.CompilerParams(dimension_semantics=sem, vmem_limit_bytes=VMEM_LIMIT_BYTES)


def _mm(a, b):
    return jnp.dot(a, b, preferred_element_type=F32)


def _nt(a, b):
    return lax.dot_general(a, b, (((1,), (1,)), ((), ())), preferred_element_type=F32)


def _silu(x):
    return x * jax.nn.sigmoid(x)


def _split3(x):
    hi = x.astype(BF16)
    r = x - hi.astype(F32)
    mid = r.astype(BF16)
    lo = (r - mid.astype(F32)).astype(BF16)
    return hi, mid, lo


def _split2(x):
    hi = x.astype(BF16)
    lo = (x - hi.astype(F32)).astype(BF16)
    return hi, lo


def _head_masks(n_lanes=GROUP_W):
    lane = lax.broadcasted_iota(jnp.int32, (1, n_lanes), 1)
    return [(lane // HEAD_DIM == h).astype(F32) for h in range(GROUP_HEADS)]


def _mrs(x, hms):
    return jnp.concatenate([x * hm for hm in hms], axis=0)


def _rbs(z):
    n = z.shape[0] // GROUP_HEADS
    return z[0:n] + z[n:2 * n] + z[2 * n:3 * n] + z[3 * n:4 * n]


def _tile4(x):
    return jnp.concatenate([x, x, x, x], axis=0)


def _group_sum(x, g2_ref):
    hi, lo = _split2(x)
    return _mm(jnp.concatenate([hi, lo], axis=1), g2_ref[...])


def _rms_mod(x, nw, shift, scale):
    ms = jnp.mean(x * x, axis=-1, keepdims=True)
    return (x * lax.rsqrt(ms + EPS) * nw) * (1.0 + scale) + shift


def _ada_kernel(c_ref, w_ref, b_ref, o_ref):
    s = _silu(c_ref[...])
    o_ref[0] = _mm(s.astype(BF16), w_ref[0]) + b_ref[0]


def _ada_call(c_all, w_ada_bf, b_ada):
    depth, d, d6 = w_ada_bf.shape
    n_chunks = d6 // d
    return pl.pallas_call(
        _ada_kernel,
        out_shape=jax.ShapeDtypeStruct((depth, SUBLANES, d6), F32),
        grid=(depth, n_chunks),
        in_specs=[
            pl.BlockSpec((SUBLANES, d), lambda l, j: (0, 0)),
            pl.BlockSpec((1, d, d), lambda l, j: (l, 0, j)),
            pl.BlockSpec((1, 1, d), lambda l, j: (l, 0, j)),
        ],
        out_specs=pl.BlockSpec((1, SUBLANES, d), lambda l, j: (l, 0, j)),
        compiler_params=_cparams(("arbitrary", "arbitrary")),
        name="ada_ln",
    )(c_all, w_ada_bf, b_ada.reshape(depth, 1, d6))


def _inproj_kernel(x_ref, mod_ref, nw_ref, w_ref, o_ref):
    h = _rms_mod(x_ref[0], nw_ref[...], mod_ref[0, 0:1, :], mod_ref[0, 1:2, :])
    o_ref[0] = _mm(h.astype(BF16), w_ref[...])


def _inproj_call(xall, mod, nw, w_bf, nt_lat):
    b, t, d = xall.shape
    nt = t // ROW_TILE
    pw = w_bf.shape[1]
    return pl.pallas_call(
        _inproj_kernel,
        out_shape=jax.ShapeDtypeStruct((b, t, pw), F32),
        grid=(b, nt),
        in_specs=[
            pl.BlockSpec((1, ROW_TILE, d), lambda i, j: (i, j, 0)),
            pl.BlockSpec((1, 6, d), lambda i, j: (i * 2 + j // nt_lat, 0, 0)),
            pl.BlockSpec((1, d), lambda i, j: (0, 0)),
            pl.BlockSpec((d, pw), lambda i, j: (0, 0), pipeline_mode=pl.Buffered(1)),
        ],
        out_specs=pl.BlockSpec((1, ROW_TILE, pw), lambda i, j: (i, j, 0)),
        compiler_params=_cparams(("arbitrary", "arbitrary")),
        name="in_proj",
    )(xall, mod, nw, w_bf)


def _shift_rows(x, prev_row, next_row, gpos, seg_starts, seg_ends):
    n = x.shape[0]
    row = lax.broadcasted_iota(jnp.int32, (n, 1), 0)
    xm = jnp.where(row == 0, prev_row, pltpu.roll(x, 1, axis=0))
    xp = jnp.where(row == n - 1, next_row, pltpu.roll(x, n - 1, axis=0))
    at_start = functools.reduce(jnp.logical_or, [gpos == s for s in seg_starts])
    at_end = functools.reduce(jnp.logical_or, [gpos == e - 1 for e in seg_ends])
    return jnp.where(at_start, 0.0, xm), jnp.where(at_end, 0.0, xp)


def _dn_prep_kernel(x_ref, xp_ref, xn_ref, ba_ref, cw_ref, aneg_ref, dtb_ref, e3_ref, g2_ref,
                    qkv_ref, aux_ref, *, n_lat, n_all):
    tm = x_ref.shape[1]
    x = x_ref[0]
    gpos = pl.program_id(1) * tm + lax.broadcasted_iota(jnp.int32, (tm, 1), 0)
    xm, xp = _shift_rows(x, xp_ref[0, SUBLANES - 1:SUBLANES, :], xn_ref[0, 0:1, :], gpos,
                         (0, n_lat), (n_lat, n_all))
    cw = cw_ref[...]
    y = _silu(xm * cw[0:1] + x * cw[1:2] + xp * cw[2:3])
    q, k, v = y[:, 0:GROUP_W], y[:, GROUP_W:2 * GROUP_W], y[:, 2 * GROUP_W:3 * GROUP_W]
    qn = q * lax.rsqrt(_group_sum(q * q, g2_ref) + EPS) * (HEAD_DIM ** -0.5)
    kn = k * lax.rsqrt(_group_sum(k * k, g2_ref) + EPS)
    qkv_ref[0] = jnp.concatenate([qn, kn, v], axis=1)
    ba = ba_ref[0]
    lane = lax.broadcasted_iota(jnp.int32, ba.shape, 1)
    z = ba + dtb_ref[...]
    softplus = jnp.maximum(z, 0.0) + jnp.log(1.0 + jnp.exp(-jnp.abs(z)))
    gates = jnp.where(lane < 2 * GROUP_HEADS, jax.nn.sigmoid(ba), aneg_ref[...] * softplus)
    gates = jnp.where(lane < 4 * GROUP_HEADS, gates, 0.0)
    hi, mid, lo = _split3(gates)
    aux_ref[0] = _mm(jnp.concatenate([hi, mid, lo], axis=1), e3_ref[...])


def _halo_specs(width, col, n_rows_total, tm):
    per = tm // SUBLANES
    last = n_rows_total // SUBLANES - 1
    prev = pl.BlockSpec((1, SUBLANES, width), lambda i, j: (i, jnp.maximum(j * per - 1, 0), col))
    nxt = pl.BlockSpec((1, SUBLANES, width), lambda i, j: (i, jnp.minimum((j + 1) * per, last), col))
    return prev, nxt


def _dn_prep_call(p, conv_w, aneg, dtb, e3, g2, n_lat):
    b, t, _ = p.shape
    nt = t // ROW_TILE
    w3 = 3 * GROUP_W
    prev, nxt = _halo_specs(w3, 0, t, ROW_TILE)
    kern = functools.partial(_dn_prep_kernel, n_lat=n_lat, n_all=t)
    return pl.pallas_call(
        kern,
        out_shape=(jax.ShapeDtypeStruct((b, t, w3), F32), jax.ShapeDtypeStruct((b, t, 4 * GROUP_W), F32)),
        grid=(b, nt),
        in_specs=[
            pl.BlockSpec((1, ROW_TILE, w3), lambda i, j: (i, j, 0)),
            prev, nxt,
            pl.BlockSpec((1, ROW_TILE, LANES), lambda i, j: (i, j, C_DN_BA_128)),
            pl.BlockSpec((3, w3), lambda i, j: (0, 0)),
            pl.BlockSpec((1, LANES), lambda i, j: (0, 0)),
            pl.BlockSpec((1, LANES), lambda i, j: (0, 0)),
            pl.BlockSpec(e3.shape, lambda i, j: (0, 0)),
            pl.BlockSpec(g2.shape, lambda i, j: (0, 0)),
        ],
        out_specs=(pl.BlockSpec((1, ROW_TILE, w3), lambda i, j: (i, j, 0)),
                   pl.BlockSpec((1, ROW_TILE, 4 * GROUP_W), lambda i, j: (i, j, 0))),
        compiler_params=_cparams(("arbitrary", "arbitrary")),
        name="dn_prep",
    )(p, p, p, p, conv_w, aneg, dtb, e3, g2)


def _scan_tile(d, s, nt_lat):
    return jnp.where(s == 0, nt_lat, jnp.where(d == 0, s - 1, nt_lat - s))


def _unit_tri_inverse(mats, lv_ref):
    n = mats[0].shape[0]
    eye = (lax.broadcasted_iota(jnp.int32, (n, n), 0) == lax.broadcasted_iota(jnp.int32, (n, n), 1)).astype(F32)
    n1 = [a * lv_ref[0] for a in mats]
    n1b = [x.astype(BF16) for x in n1]
    n2 = [_mm(x, x) for x in n1b]
    n2b = [x.astype(BF16) for x in n2]
    n4 = [_mm(x, x) for x in n2b]
    inner = [_mm((eye + x2).astype(BF16), (eye + x4).astype(BF16)) for x2, x4 in zip(n2, n4)]
    ts = [_mm((eye - x1).astype(BF16), y.astype(BF16)) for x1, y in zip(n1, inner)]
    for lvl in (1, 2, 3):
        tb = [t.astype(BF16) for t in ts]
        xs = [_mm((a * lv_ref[lvl]).astype(BF16), t) for a, t in zip(mats, tb)]
        ts = [t - _mm(t16, x.astype(BF16)) for t, t16, x in zip(ts, tb, xs)]
    return ts


def _dn_scan_kernel(qkv_f, qkv_b, beta_f, beta_b, la_f, la_b, dm_ref, lv_ref, tri_ref, of_ref, ob_ref, sf_ref, sb_ref):
    @pl.when(pl.program_id(1) == 0)
    def _():
        sf_ref[...] = jnp.zeros_like(sf_ref)
        sb_ref[...] = jnp.zeros_like(sb_ref)

    hms = _head_masks()
    n_chunks = qkv_f.shape[1] // CHUNK
    rows, a_mats, attn, w_in, u_in, qg, kdec_t, glast = [], [], [], [], [], [], [], []
    for d, (qkv_ref, beta_ref, la_ref) in enumerate(((qkv_f, beta_f, la_f), (qkv_b, beta_b, la_b))):
        incl = dm_ref[d, 0]
        strict = dm_ref[d, 1]
        for i in range(n_chunks):
            c = i if d == 0 else n_chunks - 1 - i
            r = slice(c * CHUNK, (c + 1) * CHUNK)
            rows.append(r)
            q = qkv_ref[0, r, 0:GROUP_W]
            k = qkv_ref[0, r, GROUP_W:2 * GROUP_W]
            v = qkv_ref[0, r, 2 * GROUP_W:3 * GROUP_W]
            beta = beta_ref[0, r, :]
            hi, mid, lo = _split3(la_ref[0, r, :])
            g = _mm(tri_ref[d], jnp.concatenate([hi, mid, lo], axis=0))
            gtot = g[CHUNK - 1:CHUNK, :] if d == 0 else g[0:1, :]
            eg = jnp.exp(g)
            rg = _tile4(g)
            decay = jnp.exp((rg - rg.T) * incl) * incl
            mk = _mrs(k, hms).astype(BF16)
            a_mats.append(strict * _tile4(beta) * _nt(mk, mk) * decay)
            attn.append((_nt(_mrs(q, hms).astype(BF16), mk) * decay).astype(BF16))
            w_in.append(_mrs(k * beta * eg, hms).astype(BF16))
            u_in.append(_mrs(v * beta, hms).astype(BF16))
            qg.append(_mrs(q * eg, hms).astype(BF16))
            kdec_t.append(_mrs(k * jnp.exp(gtot - g), hms).T.astype(BF16))
            glast.append(jnp.exp(gtot))
    tinv = [t.astype(BF16) for t in _unit_tri_inverse(a_mats, lv_ref)]
    w_bd = [_mm(t, x).astype(BF16) for t, x in zip(tinv, w_in)]
    u_bd = [_mm(t, x) for t, x in zip(tinv, u_in)]
    for i in range(n_chunks):
        for d, (s_ref, o_ref) in enumerate(((sf_ref, of_ref), (sb_ref, ob_ref))):
            n = d * n_chunks + i
            sv = s_ref[...]
            svb = sv.astype(BF16)
            v_newb = (u_bd[n] - _mm(w_bd[n], svb)).astype(BF16)
            o_ref[0, rows[n], :] = _rbs(_mm(qg[n], svb) + _mm(attn[n], v_newb))
            s_ref[...] = sv * glast[n] + _mm(kdec_t[n], v_newb)


def _dn_scan_call(qkv, aux, dmask, lvmask, tri3, nt_lat):
    b, t, w3 = qkv.shape
    ns = t // ROW_TILE
    tile = lambda d: (lambda s: _scan_tile(d, s, nt_lat))
    row = lambda w, col, d: pl.BlockSpec((1, ROW_TILE, w), lambda i, s: (i, tile(d)(s), col))
    const = lambda a: pl.BlockSpec(a.shape, lambda i, s: (0,) * a.ndim)
    out = jax.ShapeDtypeStruct((b, t, GROUP_W), F32)
    return pl.pallas_call(
        _dn_scan_kernel,
        out_shape=(out, out),
        grid=(b, ns),
        in_specs=[row(w3, 0, 0), row(w3, 0, 1), row(GROUP_W, 0, 0), row(GROUP_W, 1, 1),
                  row(GROUP_W, 2, 0), row(GROUP_W, 3, 1), const(dmask), const(lvmask), const(tri3)],
        out_specs=(row(GROUP_W, 0, 0), row(GROUP_W, 0, 1)),
        scratch_shapes=[pltpu.VMEM((GROUP_W, GROUP_W), F32), pltpu.VMEM((GROUP_W, GROUP_W), F32)],
        compiler_params=_cparams(("arbitrary", "arbitrary")),
        name="dn_scan",
    )(qkv, qkv, aux, aux, aux, aux, dmask, lvmask, tri3)


HG_LEVELS = (1, 2, 4, 8, 16, 32)


def _hg_scan_kernel(q_f, q_b, i_f, i_b, f_f, f_b, lb_ref, w3_ref, lm_ref, same_ref, of_ref, ob_ref,
                    st_ref, qb_ref, ke_ref, oi_ref):
    @pl.when(pl.program_id(1) == 0)
    def _():
        st_ref[...] = jnp.zeros_like(st_ref)

    hms = _head_masks()
    same = same_ref[...]
    tm = q_f.shape[1]
    n_chunks = tm // CHUNK
    nl = len(HG_LEVELS)
    dirs = ((q_f, i_f, f_f, of_ref), (q_b, i_b, f_b, ob_ref))
    units = [(d, c) for d in range(2) for c in range(n_chunks)]
    rows = [slice(c * CHUNK, (c + 1) * CHUNK) for _, c in units]
    qs = [dirs[d][0][0, r, :] * (HEAD_DIM ** -0.5) for (d, _), r in zip(units, rows)]
    fs = [lb_ref[d] + (1.0 - lb_ref[d]) * jax.nn.sigmoid(dirs[d][2][0, r, :]) for (d, _), r in zip(units, rows)]
    kxs = [1.0 - f for f in fs]
    es = [_mm(w3_ref[d], jnp.concatenate(_split3(jnp.log(f)), axis=0)) for (d, _), f in zip(units, fs)]
    atts = [lm_ref[d, 0] * _nt(_mrs(q, hms).astype(BF16), kx.astype(BF16)) for (d, _), q, kx in zip(units, qs, kxs)]
    for li in range(nl):
        ea = slice((2 + li) * CHUNK, (3 + li) * CHUNK)
        eb = slice((2 + nl + li) * CHUNK, (3 + nl + li) * CHUNK)
        terms = [_nt(_mrs(q * jnp.exp(e[ea]), hms).astype(BF16), (kx * jnp.exp(e[eb])).astype(BF16))
                 for q, kx, e in zip(qs, kxs, es)]
        atts = [a + lm_ref[d, 1 + li] * t for (d, _), a, t in zip(units, atts, terms)]
    tots = []
    for n, (d, c) in enumerate(units):
        cum = es[n][0:CHUNK]
        tot = es[n][CHUNK:2 * CHUNK]
        oi_ref[d, rows[n], :] = _rbs(_mm(atts[n].astype(BF16), dirs[d][1][0, rows[n], :].astype(BF16)) * same)
        qb_ref[d, rows[n], :] = qs[n] * jnp.exp(cum)
        ke_ref[d, rows[n], :] = kxs[n] * jnp.exp(tot - cum)
        tots.append(tot[0:1, :])
    vts = [dirs[d][1][0].T for d in range(2)]
    kebs = [ke_ref[d].astype(BF16) for d in range(2)]
    lane = lax.broadcasted_iota(jnp.int32, (1, tm), 1)
    for i in range(n_chunks):
        for d in range(2):
            c = i if d == 0 else n_chunks - 1 - i
            r = slice(c * CHUNK, (c + 1) * CHUNK)
            st = st_ref[d]
            dirs[d][3][0, r, :] = oi_ref[d, r, :] + _nt(qb_ref[d, r, :].astype(BF16), st.astype(BF16))
            vt_c = (vts[d] * (lane // CHUNK == c).astype(F32)).astype(BF16)
            st_ref[d] = st * jnp.exp(tots[d * n_chunks + c]) + same * _mm(vt_c, kebs[d])


def _hg_scan_call(p, lb, w3, lmask, same, nt_lat):
    b, t, _ = p.shape
    ns = t // ROW_TILE
    row = lambda col, d: pl.BlockSpec((1, ROW_TILE, GROUP_W), lambda i, s: (i, _scan_tile(d, s, nt_lat), col))
    const = lambda a: pl.BlockSpec(a.shape, lambda i, s: (0,) * a.ndim)
    out = jax.ShapeDtypeStruct((b, t, GROUP_W), F32)
    return pl.pallas_call(
        _hg_scan_kernel,
        out_shape=(out, out),
        grid=(b, ns),
        in_specs=[row(C_HG_Q, 0), row(C_HG_Q, 1), row(C_HG_I, 0), row(C_HG_I, 1), row(C_HG_FF, 0), row(C_HG_FB, 1),
                  const(lb), const(w3), const(lmask), const(same)],
        out_specs=(row(0, 0), row(0, 1)),
        scratch_shapes=[pltpu.VMEM((2, GROUP_W, GROUP_W), F32), pltpu.VMEM((2, ROW_TILE, GROUP_W), F32),
                        pltpu.VMEM((2, ROW_TILE, GROUP_W), F32), pltpu.VMEM((2, ROW_TILE, GROUP_W), F32)],
        compiler_params=_cparams(("arbitrary", "arbitrary")),
        name="hg_scan",
    )(p, p, p, p, p, p, lb, w3, lmask, same)


def _na_kernel(q_ref, kp_ref, kc_ref, kn_ref, vp_ref, vc_ref, vn_ref, kx_ref, vx_ref, bias_ref, same_ref,
               o_ref, kw_ref, vw_ref, *, n_rows):
    i = pl.program_id(1)
    blk = q_ref.shape[1]
    for j, (kr, vr) in enumerate(((kp_ref, vp_ref), (kc_ref, vc_ref), (kn_ref, vn_ref))):
        kw_ref[j * blk:(j + 1) * blk, :] = kr[0].astype(BF16)
        vw_ref[j * blk:(j + 1) * blk, :] = vr[0].astype(BF16)
    kx = kx_ref[0].astype(BF16)
    vx = vx_ref[0].astype(BF16)
    hms = _head_masks()
    same = same_ref[...]
    win = NA_KH * GRID_W

    def row_body(rr, carry):
        r = i * NA_ROWS + rr
        rs = jnp.clip(r - NA_KH // 2, 0, n_rows - NA_KH)
        start = pl.multiple_of((rs - i * NA_ROWS + NA_ROWS) * GRID_W, GRID_W)
        rows = pl.ds(pl.multiple_of(rr * GRID_W, GRID_W), GRID_W)
        mq = _mrs(q_ref[0, rows, :] * (HEAD_DIM ** -0.5), hms).astype(BF16)
        s_loc = _nt(mq, kw_ref[pl.ds(start, win), :]) + bias_ref[rs - r + NA_KH - 1]
        s_ctx = _nt(mq, kx)
        m = jnp.maximum(jnp.max(s_loc, axis=-1, keepdims=True), jnp.max(s_ctx, axis=-1, keepdims=True))
        p_loc = jnp.exp(s_loc - m)
        p_ctx = jnp.exp(s_ctx - m)
        den = jnp.sum(p_loc, axis=-1, keepdims=True) + jnp.sum(p_ctx, axis=-1, keepdims=True)
        o = (_mm(p_loc.astype(BF16), vw_ref[pl.ds(start, win), :]) + _mm(p_ctx.astype(BF16), vx)) / den
        o_ref[0, rows, :] = _rbs(o * same)
        return carry

    lax.fori_loop(0, NA_ROWS, row_body, 0, unroll=4)


def _na_call(p, bias, same, n_lat, n_ctx):
    b, t, _ = p.shape
    blk = NA_ROWS * GRID_W
    nb = n_lat // blk
    spec = lambda col, f: pl.BlockSpec((1, blk, GROUP_W), lambda i, j: (i, f(j), col))
    prev = lambda j: jnp.maximum(j - 1, 0)
    cur = lambda j: j
    nxt = lambda j: jnp.minimum(j + 1, nb - 1)
    ctx_spec = lambda col: pl.BlockSpec((1, n_ctx, GROUP_W), lambda i, j: (i, n_lat // n_ctx, col))
    kern = functools.partial(_na_kernel, n_rows=n_lat // GRID_W)
    return pl.pallas_call(
        kern,
        out_shape=jax.ShapeDtypeStruct((b, n_lat, GROUP_W), F32),
        grid=(b, nb),
        in_specs=[spec(C_NA_Q, cur),
                  spec(C_NA_K, prev), spec(C_NA_K, cur), spec(C_NA_K, nxt),
                  spec(C_NA_V, prev), spec(C_NA_V, cur), spec(C_NA_V, nxt),
                  ctx_spec(C_NA_K), ctx_spec(C_NA_V),
                  pl.BlockSpec(bias.shape, lambda i, j: (0, 0, 0), pipeline_mode=pl.Buffered(1)),
                  pl.BlockSpec(same.shape, lambda i, j: (0, 0))],
        out_specs=pl.BlockSpec((1, blk, GROUP_W), lambda i, j: (i, j, 0)),
        scratch_shapes=[pltpu.VMEM((3 * blk, GROUP_W), BF16), pltpu.VMEM((3 * blk, GROUP_W), BF16)],
        compiler_params=_cparams(("arbitrary", "arbitrary")),
        name="na_attn",
    )(p, p, p, p, p, p, p, p, p, bias, same)


def _na_ctx_kernel(q_ref, k_ref, v_ref, o_ref):
    hms = _head_masks()
    n = q_ref.shape[1]
    mq = _mrs(q_ref[0] * (HEAD_DIM ** -0.5), hms).astype(BF16)
    s = _nt(mq, k_ref[0].astype(BF16))
    p = jnp.exp(s - jnp.max(s, axis=-1, keepdims=True))
    o = _mm(p.astype(BF16), v_ref[0].astype(BF16)) / jnp.sum(p, axis=-1, keepdims=True)
    o_ref[0] = _rbs(o * jnp.concatenate([jnp.broadcast_to(hm, (n, GROUP_W)) for hm in hms], axis=0))


def _na_ctx_call(p, n_lat, n_ctx):
    b = p.shape[0]
    spec = lambda col: pl.BlockSpec((1, n_ctx, GROUP_W), lambda i: (i, n_lat // n_ctx, col))
    return pl.pallas_call(
        _na_ctx_kernel,
        out_shape=jax.ShapeDtypeStruct((b, n_ctx, GROUP_W), F32),
        grid=(b,),
        in_specs=[spec(C_NA_Q), spec(C_NA_K), spec(C_NA_V)],
        out_specs=pl.BlockSpec((1, n_ctx, GROUP_W), lambda i: (i, 0, 0)),
        compiler_params=_cparams(("arbitrary",)),
        name="na_ctx",
    )(p, p, p)


def _df_prep_kernel(q_ref, k_ref, v_ref, cos_ref, sin_ref, qt_ref, kr_ref, vt_ref):
    cos = cos_ref[...]
    sin = sin_ref[...]
    lane = lax.broadcasted_iota(jnp.int32, cos.shape, 1)
    first = (lane % 16) < 8

    def rope(x):
        w = x.shape[1]
        partner = jnp.where(first, pltpu.roll(x, w - 8, axis=1), pltpu.roll(x, 8, axis=1))
        return x * cos + partner * sin

    tm = q_ref.shape[1]
    qt = (rope(q_ref[0]) * (DF_HEAD_DIM ** -0.5 * LOG2E)).T
    row = lax.broadcasted_iota(jnp.int32, (GROUP_W, 1), 0)
    for s in range(2 * GROUP_HEADS):
        keep = (row // DF_HEAD_DIM == s).astype(F32)
        qt_ref[0, s] = (qt * keep).astype(BF16)
    kr_ref[0, 0] = rope(k_ref[0]).astype(BF16)
    vt = v_ref[0].T
    ext = (lax.broadcasted_iota(jnp.int32, (2 * SUBLANES, tm), 0) == 0).astype(F32)
    for h in range(GROUP_HEADS):
        vt_ref[0, h, 0] = jnp.concatenate([vt[h * HEAD_DIM:(h + 1) * HEAD_DIM], ext], axis=0).astype(BF16)


def _df_prep_call(p, cos, sin):
    b, t, _ = p.shape
    nt = t // DF_KC
    vrows = HEAD_DIM + 2 * SUBLANES
    blk = lambda col: pl.BlockSpec((1, DF_KC, GROUP_W), lambda i, j: (i, j, col))
    return pl.pallas_call(
        _df_prep_kernel,
        out_shape=(jax.ShapeDtypeStruct((b, 2 * GROUP_HEADS, GROUP_W, t), BF16),
                   jax.ShapeDtypeStruct((b, nt, DF_KC, GROUP_W), BF16),
                   jax.ShapeDtypeStruct((b, GROUP_HEADS, nt, vrows, DF_KC), BF16)),
        grid=(b, nt),
        in_specs=[blk(C_DF_Q), blk(C_DF_K), blk(C_DF_V),
                  pl.BlockSpec((DF_KC, GROUP_W), lambda i, j: (j, 0)),
                  pl.BlockSpec((DF_KC, GROUP_W), lambda i, j: (j, 0))],
        out_specs=(pl.BlockSpec((1, 2 * GROUP_HEADS, GROUP_W, DF_KC), lambda i, j: (i, 0, 0, j)),
                   pl.BlockSpec((1, 1, DF_KC, GROUP_W), lambda i, j: (i, j, 0, 0)),
                   pl.BlockSpec((1, GROUP_HEADS, 1, vrows, DF_KC), lambda i, j: (i, 0, j, 0, 0))),
        compiler_params=_cparams(("arbitrary", "arbitrary")),
        name="df_prep",
    )(p, p, p, cos, sin)


def _df_attn_kernel(qt_ref, k_ref, vt_ref, lam_ref, nw_ref, o_ref,
                    m_ref, acc_ref, s_ref, p_ref, al_ref, oj_ref, *, lam_init):
    nk = k_ref.shape[1]
    m_ref[...] = jnp.full(m_ref.shape, -jnp.inf, F32)
    acc_ref[...] = jnp.zeros_like(acc_ref)

    def scores(c, slot):
        s_ref[slot] = _mm(k_ref[0, c], qt_ref[0, 0])

    def probs(slot):
        m_old = m_ref[...]
        s = s_ref[slot]
        m_new = jnp.maximum(m_old, jnp.max(s, axis=0, keepdims=True))
        p_ref[slot] = jnp.exp2(s - m_new).astype(BF16)
        al_ref[slot] = jnp.exp2(m_old - m_new)
        m_ref[...] = m_new

    def accum(c, slot):
        acc_ref[...] = acc_ref[...] * al_ref[slot] + _mm(vt_ref[0, 0, c], p_ref[slot])

    def iteration(i, slot, do_scores=True, do_probs=True, do_accum=True):
        if do_accum:
            accum(i - 2, slot)
        if do_probs:
            probs(1 - slot)
        if do_scores:
            scores(i, slot)

    for i in range(min(2, nk + 2)):
        iteration(i, i % 2, i < nk, 1 <= i <= nk, False)
    n_steady = max(nk - 2, 0)

    def group(k, carry):
        for u in range(DF_UNROLL):
            iteration(2 + DF_UNROLL * k + u, u % 2)
        return carry

    n_groups = n_steady // DF_UNROLL
    lax.fori_loop(0, n_groups, group, 0)
    for i in range(2 + DF_UNROLL * n_groups, nk + 2):
        iteration(i, i % 2, i < nk, i <= nk, True)
    a = acc_ref[...]
    oj_ref[pl.program_id(3)] = a[0:HEAD_DIM] / a[HEAD_DIM:HEAD_DIM + 1]

    @pl.when(pl.program_id(3) == 1)
    def _():
        lp = lam_ref[...]
        lam = (jnp.exp(jnp.sum(lp[0:1] * lp[1:2], keepdims=True))
               - jnp.exp(jnp.sum(lp[2:3] * lp[3:4], keepdims=True)) + lam_init)
        o = oj_ref[0] - lam * oj_ref[1]
        ms = jnp.mean(o * o, axis=0, keepdims=True)
        o_ref[0] = o * lax.rsqrt(ms + EPS) * nw_ref[...] * (1.0 - lam_init)


def _df_attn_call(qt, kr, vt, lam_p, nw_col, lam_init, q_off, n_q, qb, k_off, nk):
    b = qt.shape[0]
    vrows = vt.shape[3]
    return pl.pallas_call(
        functools.partial(_df_attn_kernel, lam_init=lam_init),
        out_shape=jax.ShapeDtypeStruct((b, GROUP_W, n_q), F32),
        grid=(b, GROUP_HEADS, n_q // qb, 2),
        in_specs=[
            pl.BlockSpec((1, 1, GROUP_W, qb), lambda i, h, q, j: (i, 2 * h + j, 0, q_off // qb + q)),
            pl.BlockSpec((1, nk, DF_KC, GROUP_W), lambda i, h, q, j: (i, k_off // nk, 0, 0),
                         pipeline_mode=pl.Buffered(1)),
            pl.BlockSpec((1, 1, nk, vrows, DF_KC), lambda i, h, q, j: (i, h, k_off // nk, 0, 0)),
            pl.BlockSpec(lam_p.shape, lambda i, h, q, j: (0, 0)),
            pl.BlockSpec(nw_col.shape, lambda i, h, q, j: (0, 0)),
        ],
        out_specs=pl.BlockSpec((1, HEAD_DIM, qb), lambda i, h, q, j: (i, h, q)),
        scratch_shapes=[pltpu.VMEM((1, qb), F32), pltpu.VMEM((vrows, qb), F32),
                        pltpu.VMEM((2, DF_KC, qb), F32),
                        pltpu.VMEM((2, DF_KC, qb), BF16), pltpu.VMEM((2, 1, qb), F32),
                        pltpu.VMEM((2, HEAD_DIM, qb), F32)],
        compiler_params=_cparams(("arbitrary", "arbitrary", "arbitrary", "arbitrary")),
        name="df_attn",
    )(qt, kr, vt, lam_p, nw_col)


def _outproj_kernel(x_ref, mod_ref, dno_f, dno_b, dng_ref, na_ref, nac_ref, dft_ref, dftc_ref, hgo_f, hgo_b, hgg_ref,
                    dnw_ref, hgw_ref, g2_ref, w_ref, o_ref, *, nt_lat):
    inv = 1.0 / HEAD_DIM
    is_ctx = pl.program_id(1) == nt_lat

    def gated_norm(o, nw, gate):
        ms = _group_sum(o * o, g2_ref) * inv
        return o * lax.rsqrt(ms + EPS) * nw * _silu(gate)

    y_dn = gated_norm(dno_f[0] + dno_b[0], dnw_ref[...], dng_ref[0])
    y_hg = gated_norm(hgo_f[0] + hgo_b[0], hgw_ref[...], hgg_ref[0])
    y_df = jnp.where(is_ctx, dftc_ref[0], dft_ref[0]).T
    y_na = jnp.where(is_ctx, nac_ref[0], na_ref[0])
    acc = _mm(y_dn.astype(BF16), w_ref[0:GROUP_W, :])
    acc += _mm(y_na.astype(BF16), w_ref[GROUP_W:2 * GROUP_W, :])
    acc += _mm(y_df.astype(BF16), w_ref[2 * GROUP_W:3 * GROUP_W, :])
    acc += _mm(y_hg.astype(BF16), w_ref[3 * GROUP_W:4 * GROUP_W, :])
    o_ref[0] = x_ref[0] + mod_ref[0, 2:3, :] * acc


def _outproj_call(xall, mod, dn_o, p, y_na, y_na_ctx, y_dft, y_dft_ctx, hg_o, dnw, hgw, g2, w_bf, nt_lat, n_tiles):
    b, t, d = xall.shape
    row = lambda w, col=0: pl.BlockSpec((1, ROW_TILE, w), lambda i, j: (i, j, col))
    dirspec = lambda dd: row(GROUP_W)
    const = lambda a: pl.BlockSpec(a.shape, lambda i, j: (0,) * a.ndim)
    lat = lambda j: jnp.minimum(j, nt_lat - 1)
    return pl.pallas_call(
        functools.partial(_outproj_kernel, nt_lat=nt_lat),
        out_shape=jax.ShapeDtypeStruct((b, n_tiles * ROW_TILE, d), F32),
        grid=(b, n_tiles),
        in_specs=[row(d),
                  pl.BlockSpec((1, 6, d), lambda i, j: (i * 2 + j // nt_lat, 0, 0)),
                  dirspec(0), dirspec(1), row(GROUP_W, C_DN_GATE),
                  pl.BlockSpec((1, ROW_TILE, GROUP_W), lambda i, j: (i, lat(j), 0)),
                  pl.BlockSpec((1, ROW_TILE, GROUP_W), lambda i, j: (i, 0, 0)),
                  pl.BlockSpec((1, GROUP_W, ROW_TILE), lambda i, j: (i, 0, lat(j))),
                  pl.BlockSpec((1, GROUP_W, ROW_TILE), lambda i, j: (i, 0, 0)),
                  dirspec(0), dirspec(1), row(GROUP_W, C_HG_GATE),
                  const(dnw), const(hgw), const(g2),
                  pl.BlockSpec(w_bf.shape, lambda i, j: (0, 0), pipeline_mode=pl.Buffered(1))],
        out_specs=row(d),
        compiler_params=_cparams(("arbitrary", "arbitrary")),
        name="out_proj",
    )(xall, mod, dn_o[0], dn_o[1], p, y_na, y_na_ctx, y_dft, y_dft_ctx, hg_o[0], hg_o[1], p, dnw, hgw, g2, w_bf)


def _ffn_kernel(x_ref, xp_ref, xn_ref, mod_ref, nw_ref, wup_ref, cw_ref, wdn_ref, fw_ref, o_ref,
                *, n_lat, n_all, final):
    tm = x_ref.shape[1]
    d_ff = wdn_ref.shape[0]
    fc = d_ff // FF_SPLIT
    x = x_ref[0]
    xe = jnp.concatenate([x, xp_ref[0], xn_ref[0]], axis=0)
    hb = _rms_mod(xe, nw_ref[...], mod_ref[0, 3:4, :], mod_ref[0, 4:5, :]).astype(BF16)
    gpos = pl.program_id(1) * tm + lax.broadcasted_iota(jnp.int32, (tm, 1), 0)
    acc = jnp.zeros((tm, x.shape[1]), F32)
    for f in range(FF_SPLIT):
        g_all = _mm(hb, wup_ref[:, f * fc:(f + 1) * fc])
        val = _mm(hb[0:tm], wup_ref[:, d_ff + f * fc:d_ff + (f + 1) * fc])
        g = g_all[0:tm]
        gm, gp = _shift_rows(g, g_all[tm + SUBLANES - 1:tm + SUBLANES], g_all[tm + SUBLANES:tm + SUBLANES + 1],
                             gpos, (0, n_lat), (n_lat, n_all))
        cw = cw_ref[:, f * fc:(f + 1) * fc]
        a = _silu(gm * cw[0:1] + g * cw[1:2] + gp * cw[2:3]) * val
        acc += _mm(a.astype(BF16), wdn_ref[f * fc:(f + 1) * fc, :])
    y = x + mod_ref[0, 5:6, :] * acc
    if final:
        ms = jnp.mean(y * y, axis=-1, keepdims=True)
        y = y * lax.rsqrt(ms + EPS) * fw_ref[...]
    o_ref[0] = y


def _ffn_call(xmid, mod, nw, wup_bf, cw, wdn_bf, fw, nt_lat, n_tiles, n_lat, final):
    b, t, d = xmid.shape
    out_rows = n_tiles * ROW_TILE
    prev, nxt = _halo_specs(d, 0, out_rows, ROW_TILE)
    kern = functools.partial(_ffn_kernel, n_lat=n_lat, n_all=t, final=final)
    const = lambda a, **kw: pl.BlockSpec(a.shape, lambda i, j: (0,) * a.ndim, **kw)
    return pl.pallas_call(
        kern,
        out_shape=jax.ShapeDtypeStruct((b, out_rows, d), F32),
        grid=(b, n_tiles),
        in_specs=[pl.BlockSpec((1, ROW_TILE, d), lambda i, j: (i, j, 0)), prev, nxt,
                  pl.BlockSpec((1, 6, d), lambda i, j: (i * 2 + j // nt_lat, 0, 0)),
                  const(nw), const(wup_bf, pipeline_mode=pl.Buffered(1)), const(cw),
                  const(wdn_bf, pipeline_mode=pl.Buffered(1)), const(fw)],
        out_specs=pl.BlockSpec((1, ROW_TILE, d), lambda i, j: (i, j, 0)),
        compiler_params=_cparams(("arbitrary", "arbitrary")),
        name="conv_ffn",
    )(xmid, xmid, xmid, mod, nw, wup_bf, cw, wdn_bf, fw)


def _scan_constants():
    c = CHUNK
    n = GROUP_W
    t = np.arange(c)
    tri_f = (t[None, :] <= t[:, None]).astype(np.float32)
    tri = np.stack([tri_f, tri_f[::-1, ::-1]])
    tri3 = np.concatenate([tri] * 3, axis=2)

    idx = np.arange(n)
    hh, tt = idx // c, idx % c
    same = hh[:, None] == hh[None, :]
    ti, si = tt[:, None], tt[None, :]
    dmask = np.stack([np.stack([same & (si <= ti), same & (si < ti)]),
                      np.stack([same & (si >= ti), same & (si > ti)])]).astype(np.float32)
    lv = [same & (ti // 8 == si // 8)]
    for m in (8, 16, 32):
        lv.append(same & (ti // (2 * m) == si // (2 * m)) & (ti // m != si // m))
    lvmask = np.stack(lv).astype(np.float32)

    u = t[None, :]
    ws = [tri_f, np.ones((c, c), np.float32)]
    for m in HG_LEVELS:
        r = (t // m) * m
        ws.append(((u > r[:, None]) & (u <= t[:, None])).astype(np.float32))
    for m in HG_LEVELS:
        r2 = np.minimum((t // m + 1) * m, c - 1)
        ws.append(((u > t[:, None]) & (u <= r2[:, None])).astype(np.float32))
    w_f = np.concatenate(ws, axis=0)
    w_b = np.concatenate([w[::-1, ::-1] for w in ws], axis=0)
    w3 = np.stack([np.concatenate([w_f] * 3, axis=1), np.concatenate([w_b] * 3, axis=1)])

    lms = [np.eye(c, dtype=bool)]
    for m in HG_LEVELS:
        lms.append(((t[:, None] // m) % 2 == 1) & (t[None, :] // m == t[:, None] // m - 1))
    lm_f = np.stack(lms)
    lm_b = lm_f[:, ::-1, ::-1]
    lmask = np.stack([np.tile(lm_f, (1, GROUP_HEADS, 1)), np.tile(lm_b, (1, GROUP_HEADS, 1))]).astype(np.float32)

    g = same.astype(np.float32)
    g2 = np.concatenate([g, g], axis=0)
    e = np.zeros((LANES, 4 * GROUP_W), np.float32)
    for j in range(4 * GROUP_HEADS):
        e[j, j * HEAD_DIM:(j + 1) * HEAD_DIM] = 1.0
    e3 = np.concatenate([e, e, e], axis=0)
    return dict(tri3=jnp.asarray(tri3, BF16), dmask=jnp.asarray(dmask), lvmask=jnp.asarray(lvmask),
                w3=jnp.asarray(w3, BF16), lmask=jnp.asarray(lmask), same=jnp.asarray(g),
                g2=jnp.asarray(g2, BF16), e3=jnp.asarray(e3, BF16))


def _na_bias_table(rpb):
    cols = np.arange(GRID_W)
    c_start = np.clip(cols - NA_KW // 2, 0, GRID_W - NA_KW)
    kc = np.arange(GRID_W)
    valid = (kc[None, :] >= c_start[:, None]) & (kc[None, :] < c_start[:, None] + NA_KW)
    pad = GRID_W - NA_KW
    padded = jnp.pad(rpb.astype(F32), ((0, 0), (0, 0), (pad, pad)))
    toe = jnp.stack([padded[:, :, GRID_W - 1 - q:2 * GRID_W - 1 - q] for q in range(GRID_W)], axis=1)
    toe = jnp.where(valid[None, :, None, :], toe, -jnp.inf)
    variants = [toe[:, :, var:var + NA_KH, :].reshape(GROUP_HEADS * GRID_W, NA_KH * GRID_W) for var in range(NA_KH)]
    return jnp.stack(variants)


def _rope_tables(n_lat, n_ctx):
    pos = jnp.arange(n_lat)
    row = (pos // GRID_W).astype(F32)
    col = (pos % GRID_W).astype(F32)
    half = DF_HEAD_DIM // 2
    inv = 1.0 / (ROPE_BASE ** (jnp.arange(0, half, 2, dtype=F32) / half))
    lane = np.arange(GROUP_W)
    use_row = jnp.asarray((lane % DF_HEAD_DIM) < half)
    ang = jnp.where(use_row[None, :], row[:, None], col[:, None]) * inv[lane % (half // 2)][None, :]
    sign = jnp.asarray(np.where((lane % half) < half // 2, -1.0, 1.0), F32)
    cos = jnp.concatenate([jnp.cos(ang), jnp.ones((n_ctx, GROUP_W), F32)], axis=0)
    sin = jnp.concatenate([jnp.sin(ang) * sign[None, :], jnp.zeros((n_ctx, GROUP_W), F32)], axis=0)
    return cos, sin


def _reorder_w_in(w_in_l):
    d = w_in_l.shape[0]
    a = 4 * GROUP_W
    ba = w_in_l[:, a:a + 4 * GROUP_HEADS]
    rest = w_in_l[:, a + 4 * GROUP_HEADS:]
    pad = jnp.zeros((d, LANES - 4 * GROUP_HEADS), w_in_l.dtype)
    return jnp.concatenate([w_in_l[:, :a], rest, ba, pad], axis=1)


def kernel(x, c, ctx, c_ctx, w_ada, b_ada, norm1_w, norm2_w, w_in, dn_conv_w, dn_a_log, dn_dt_bias, dn_norm_w,
           na_rpb, df_lambda, df_norm_w, hg_lb_raw, hg_norm_w, w_out, w_up, ffn_conv_w, w_down, final_norm_w):
    b, n_lat, d = x.shape
    n_ctx = ctx.shape[1]
    depth = w_ada.shape[0]
    t = n_lat + n_ctx
    assert n_ctx == ROW_TILE == DF_KC and n_lat % (NA_ROWS * GRID_W) == 0 and n_lat % DF_QB == 0
    assert b <= SUBLANES - 1
    nt_lat = n_lat // ROW_TILE
    nt_all = t // ROW_TILE

    consts = _scan_constants()
    cos, sin = _rope_tables(n_lat, n_ctx)

    c_all = jnp.zeros((SUBLANES, d), F32).at[:b].set(c).at[b].set(c_ctx)
    ada = _ada_call(c_all, w_ada.astype(BF16), b_ada).reshape(depth, SUBLANES, 6, d)

    lb_all = jnp.cumsum(jax.nn.softmax(hg_lb_raw.astype(F32), axis=0), axis=0)
    lb_all = (lb_all - lb_all[0]).reshape(depth, 2, 1, GROUP_W)

    xall = jnp.concatenate([x, ctx], axis=1)
    out = None
    for l in range(depth):
        need_ctx = l < depth - 1
        n_tiles = nt_all if need_ctx else nt_lat
        lam_init = 0.8 - 0.6 * math.exp(-0.3 * l)
        mod = jnp.stack([ada[l, :b], jnp.broadcast_to(ada[l, b], (b, 6, d))], axis=1).reshape(2 * b, 6, d)

        p = _inproj_call(xall, mod, norm1_w[l].reshape(1, d), _reorder_w_in(w_in[l]).astype(BF16), nt_lat)

        lane = np.arange(LANES)
        gate_idx = np.clip(lane - 2 * GROUP_HEADS, 0, 2 * GROUP_HEADS - 1)
        is_decay = jnp.asarray((lane >= 2 * GROUP_HEADS) & (lane < 4 * GROUP_HEADS))
        aneg = jnp.where(is_decay, -jnp.exp(dn_a_log[l].astype(F32).reshape(-1)[gate_idx]), 0.0).reshape(1, LANES)
        dtb = jnp.where(is_decay, dn_dt_bias[l].astype(F32).reshape(-1)[gate_idx], 0.0).reshape(1, LANES)
        dn_qkv, dn_aux = _dn_prep_call(p, dn_conv_w[l], aneg, dtb, consts["e3"], consts["g2"], n_lat)
        dn_o = _dn_scan_call(dn_qkv, dn_aux, consts["dmask"], consts["lvmask"], consts["tri3"], nt_lat)

        hg_o = _hg_scan_call(p, lb_all[l], consts["w3"], consts["lmask"], consts["same"], nt_lat)

        y_na = _na_call(p, _na_bias_table(na_rpb[l]), consts["same"], n_lat, n_ctx)
        y_na_ctx = _na_ctx_call(p, n_lat, n_ctx) if need_ctx else y_na

        qt, kr, vt = _df_prep_call(p, cos, sin)
        nw_col = df_norm_w[l].reshape(HEAD_DIM, 1)
        lam_p = jnp.pad(df_lambda[l].astype(F32), ((0, 4 * SUBLANES - 4), (0, LANES - DF_HEAD_DIM)))
        y_dft = _df_attn_call(qt, kr, vt, lam_p, nw_col, lam_init, 0, n_lat, DF_QB, 0, t // DF_KC)
        y_dft_ctx = (_df_attn_call(qt, kr, vt, lam_p, nw_col, lam_init, n_lat, n_ctx, n_ctx,
                                   n_lat // DF_KC, 1) if need_ctx else y_dft)

        xmid = _outproj_call(xall, mod, dn_o, p, y_na, y_na_ctx, y_dft, y_dft_ctx, hg_o,
                             jnp.tile(dn_norm_w[l], GROUP_HEADS).reshape(1, GROUP_W),
                             jnp.tile(hg_norm_w[l], GROUP_HEADS).reshape(1, GROUP_W),
                             consts["g2"], w_out[l].astype(BF16), nt_lat, n_tiles)
        out = _ffn_call(xmid, mod, norm2_w[l].reshape(1, d), w_up[l].astype(BF16), ffn_conv_w[l],
                        w_down[l].astype(BF16), final_norm_w.reshape(1, d), nt_lat, n_tiles, n_lat,
                        final=not need_ctx)
        xall = out
    return out
```
